```python
import jax, jax.numpy as jnp
from jax import lax
import numpy as np

D_MODEL = 2048
BATCH = 8
SEQ = 2048
DEPTH = 1

D_MIX = D_MODEL
A_HEADS = 8
A_HEAD_DIM = D_MIX // 2 // A_HEADS
A_WIDTH = A_HEADS * A_HEAD_DIM
CHUNK = 128
B_GROUPS = 8
B_WIDTH = D_MIX - A_WIDTH
CONV_W = 3
IN_COLS = 2 * A_WIDTH + 3 * B_WIDTH
N_GROUPS = 4
EXPERTS_PER_GROUP = 8
N_EXPERTS = N_GROUPS * EXPERTS_PER_GROUP
TOP_K = 2
D_EXPERT = D_MODEL // 4
MOE_BLOCK = 128
N_MOD = 6
EPS = 1e-6

kernel_name = "hybrid_gmlp_shortconv_hmoe_adaln"


def rms_norm(x, g):
    xf = x.astype(jnp.float32)
    y = xf * lax.rsqrt(jnp.mean(xf * xf, axis=-1, keepdims=True) + EPS)
    return (y * g.astype(jnp.float32)).astype(x.dtype)


def modulate(h, shift, scale):
    return h * (1 + scale[:, None, :]) + shift[:, None, :]


def gmlp_chunk_mixer(z, w_s, b_s, v_gain):
    bsz, s, _ = z.shape
    u, v = jnp.split(z, 2, axis=-1)
    v = v.reshape(bsz, s // CHUNK, CHUNK, A_HEADS, A_HEAD_DIM)
    v = rms_norm(v, v_gain)
    mask = jnp.tril(jnp.ones((CHUNK, CHUNK), dtype=bool))
    w = jnp.where(mask[None], w_s, jnp.zeros_like(w_s))
    zs = jnp.einsum('htq,bcqhd->bcthd', w, v) + b_s.T[None, None, :, :, None]
    return u * zs.reshape(bsz, s, A_WIDTH)


def short_conv_mixer(z, conv_w, conv_b):
    s = z.shape[1]
    bg, cg, xin = jnp.split(z, 3, axis=-1)
    pre = cg * xin
    p = jnp.pad(pre, ((0, 0), (CONV_W - 1, 0), (0, 0)))
    conv = conv_w[0] * p[:, 0:s]
    for k in range(1, CONV_W):
        conv = conv + conv_w[k] * p[:, k:k + s]
    return bg * (conv + conv_b)


def hier_moe(h, w_router_group, b_router_group, w_router_expert, b_router_expert, w_gate_up, w_down):
    bsz, s, d = h.shape
    t = bsz * s
    xt = h.reshape(t, d)
    xf = xt.astype(jnp.float32)
    g_logits = xf @ w_router_group.astype(jnp.float32) + b_router_group.astype(jnp.float32)
    g_prob = jax.nn.softmax(g_logits, axis=-1)
    grp = jnp.argmax(g_logits, axis=-1)
    p_grp = jnp.take_along_axis(g_prob, grp[:, None], axis=-1)
    e_logits = (xf @ w_router_expert.astype(jnp.float32) + b_router_expert.astype(jnp.float32))
    e_logits = e_logits.reshape(t, N_GROUPS, EXPERTS_PER_GROUP)
    e_logits_g = jnp.take_along_axis(e_logits, grp[:, None, None], axis=1)[:, 0]
    top_val, top_idx = lax.top_k(e_logits_g, TOP_K)
    gates = p_grp * jax.nn.softmax(top_val, axis=-1)
    expert_id = grp[:, None] * EXPERTS_PER_GROUP + top_idx
    n_assign = t * TOP_K
    flat_e = expert_id.reshape(-1).astype(jnp.int32)
    flat_tok = jnp.repeat(jnp.arange(t, dtype=jnp.int32), TOP_K)
    flat_gate = gates.reshape(-1)
    order = jnp.argsort(flat_e)
    se = flat_e[order]
    counts = jnp.bincount(flat_e, length=N_EXPERTS)
    padded = ((counts + MOE_BLOCK - 1) // MOE_BLOCK) * MOE_BLOCK
    starts = jnp.cumsum(counts) - counts
    pends = jnp.cumsum(padded)
    pstarts = pends - padded
    dest = pstarts[se] + (jnp.arange(n_assign, dtype=jnp.int32) - starts[se])
    n_pad = n_assign + N_EXPERTS * MOE_BLOCK
    n_blocks = n_pad // MOE_BLOCK
    buf_tok = jnp.zeros((n_pad,), jnp.int32).at[dest].set(flat_tok[order])
    buf_gate = jnp.zeros((n_pad,), jnp.float32).at[dest].set(flat_gate[order])
    block_expert = jnp.minimum(
        jnp.searchsorted(pends, jnp.arange(n_blocks, dtype=jnp.int32) * MOE_BLOCK, side='right'),
        N_EXPERTS - 1).astype(jnp.int32)
    xb = xt[buf_tok].reshape(n_blocks, MOE_BLOCK, d)

    def expert_block(args):
        xblk, e = args
        gu = xblk @ w_gate_up[e]
        g, u = jnp.split(gu, 2, axis=-1)
        return (jax.nn.silu(g) * u) @ w_down[e]

    yb = lax.map(expert_block, (xb, block_expert))
    y = yb.reshape(n_pad, d) * buf_gate[:, None].astype(h.dtype)
    out = jnp.zeros((t, d), h.dtype).at[buf_tok].add(y)
    return out.reshape(bsz, s, d)


def setup_inputs(seed: int = 0) -> dict:
    key = jax.random.key(seed)
    ks = jax.random.split(key, 24)
    f32 = jnp.float32
    nrm = lambda k, shape, sc: jax.random.normal(k, shape, f32) * sc
    L = DEPTH
    return {
        "x": jax.random.normal(ks[0], (BATCH, SEQ, D_MODEL), f32),
        "c": jax.random.normal(ks[1], (BATCH, D_MODEL), f32),
        "w_ada": nrm(ks[2], (L, D_MODEL, N_MOD * D_MODEL), 0.5 * D_MODEL ** -0.5),
        "b_ada": nrm(ks[3], (L, N_MOD * D_MODEL), 0.02),
        "norm1_g": 1.0 + nrm(ks[4], (L, D_MODEL), 0.02),
        "w_in": nrm(ks[5], (L, D_MODEL, IN_COLS), D_MODEL ** -0.5),
        "w_out": nrm(ks[6], (L, D_MIX, D_MODEL), D_MIX ** -0.5),
        "gmlp_w_s": nrm(ks[7], (L, A_HEADS, CHUNK, CHUNK), CHUNK ** -0.5),
        "gmlp_b_s": 1.0 + nrm(ks[8], (L, A_HEADS, CHUNK), 0.02),
        "gmlp_v_gain": 1.0 + nrm(ks[9], (L, A_HEADS, A_HEAD_DIM), 0.02),
        "conv_w": nrm(ks[10], (L, CONV_W, B_WIDTH), CONV_W ** -0.5),
        "conv_b": nrm(ks[11], (L, B_WIDTH), 0.02),
        "norm2_g": 1.0 + nrm(ks[12], (L, D_MODEL), 0.02),
        "w_router_group": nrm(ks[13], (L, D_MODEL, N_GROUPS), D_MODEL ** -0.5),
        "b_router_group": nrm(ks[14], (L, N_GROUPS), 0.01),
        "w_router_expert": nrm(ks[15], (L, D_MODEL, N_EXPERTS), D_MODEL ** -0.5),
        "b_router_expert": nrm(ks[16], (L, N_EXPERTS), 0.01),
        "w_gate_up": nrm(ks[17], (L, N_EXPERTS, D_MODEL, 2 * D_EXPERT), D_MODEL ** -0.5),
        "w_down": nrm(ks[18], (L, N_EXPERTS, D_EXPERT, D_MODEL), D_EXPERT ** -0.5),
        "w_ada_final": nrm(ks[19], (D_MODEL, 2 * D_MODEL), 0.5 * D_MODEL ** -0.5),
        "b_ada_final": nrm(ks[20], (2 * D_MODEL,), 0.02),
        "norm_f_g": 1.0 + nrm(ks[21], (D_MODEL,), 0.02),
    }


def reference(x, c, w_ada, b_ada, norm1_g, w_in, w_out, gmlp_w_s, gmlp_b_s, gmlp_v_gain,
              conv_w, conv_b, norm2_g, w_router_group, b_router_group, w_router_expert,
              b_router_expert, w_gate_up, w_down, w_ada_final, b_ada_final, norm_f_g):
    c_act = jax.nn.silu(c)
    for l in range(DEPTH):
        mod = c_act @ w_ada[l] + b_ada[l]
        shift1, scale1, gate1, shift2, scale2, gate2 = jnp.split(mod, N_MOD, axis=-1)
        h = modulate(rms_norm(x, norm1_g[l]), shift1, scale1)
        proj = h @ w_in[l]
        y_a = gmlp_chunk_mixer(jax.nn.gelu(proj[..., :2 * A_WIDTH]),
                               gmlp_w_s[l], gmlp_b_s[l], gmlp_v_gain[l])
        y_b = short_conv_mixer(proj[..., 2 * A_WIDTH:], conv_w[l], conv_b[l])
        mix = jnp.concatenate([y_a, y_b], axis=-1) @ w_out[l]
        x = x + gate1[:, None, :] * mix
        h = modulate(rms_norm(x, norm2_g[l]), shift2, scale2)
        ffn = hier_moe(h, w_router_group[l], b_router_group[l], w_router_expert[l],
                       b_router_expert[l], w_gate_up[l], w_down[l])
        x = x + gate2[:, None, :] * ffn
    shift_f, scale_f = jnp.split(c_act @ w_ada_final + b_ada_final, 2, axis=-1)
    return modulate(rms_norm(x, norm_f_g), shift_f, scale_f)
```

```python
import functools

import jax
import jax.numpy as jnp
from jax import lax
from jax.experimental import pallas as pl
from jax.experimental.pallas import tpu as pltpu

F32 = jnp.float32
BF16 = jnp.bfloat16
I32 = jnp.int32

A_HEADS = 8
CHUNK = 128
N_GROUPS = 4
TOP_K = 2
EPS = 1e-6

V7X_LANES = 128
V7X_SUBLANES = 8
V7X_VMEM_BYTES = 64 * 1024 * 1024

TOKEN_TILE = 256
EXPERT_TILE = 256
ROUTER_ROWS = 40


def _rms(x, g):
    y = x * lax.rsqrt(jnp.mean(x * x, axis=-1, keepdims=True) + EPS)
    return y * g


def _params(vmem_bytes, semantics=None):
    kw = dict(vmem_limit_bytes=int(vmem_bytes))
    if semantics is not None:
        kw["dimension_semantics"] = semantics
    return pltpu.CompilerParams(**kw)


def _ada_body(c_ref, w_ref, b_ref, o_ref):
    ca = jax.nn.silu(c_ref[...]).astype(BF16)
    o_ref[...] = jnp.dot(ca, w_ref[...].astype(BF16), preferred_element_type=F32) + b_ref[...]


def _ada(c, w, b, tn=1024):
    bsz, d = c.shape
    n = w.shape[1]
    return pl.pallas_call(
        _ada_body,
        grid=(n // tn,),
        in_specs=[
            pl.BlockSpec((bsz, d), lambda j: (0, 0)),
            pl.BlockSpec((d, tn), lambda j: (0, j)),
            pl.BlockSpec((1, tn), lambda j: (0, j)),
        ],
        out_specs=pl.BlockSpec((bsz, tn), lambda j: (0, j)),
        out_shape=jax.ShapeDtypeStruct((bsz, n), F32),
        compiler_params=_params(4 * d * tn * 4, ("arbitrary",)),
        name="ada",
    )(c, w, b.reshape(1, n))


def _route(lt, tm):
    n_e = 32
    epg = n_e // N_GROUPS
    row = lax.broadcasted_iota(I32, (V7X_SUBLANES, tm), 0).astype(F32)
    gl = lt[n_e:n_e + V7X_SUBLANES]
    gvalid = row < float(N_GROUPS)
    glm = jnp.where(gvalid, gl, -jnp.inf)
    gmax = jnp.max(glm, axis=0, keepdims=True)
    garg = jnp.min(jnp.where(glm == gmax, row, float(V7X_SUBLANES)), axis=0, keepdims=True)
    gsum = jnp.sum(jnp.where(gvalid, jnp.exp(gl - gmax), 0.0), axis=0, keepdims=True)
    p_grp = 1.0 / gsum
    es = lt[0:epg]
    for g in range(1, N_GROUPS):
        es = jnp.where(garg == float(g), lt[g * epg:(g + 1) * epg], es)
    m1 = jnp.max(es, axis=0, keepdims=True)
    i1 = jnp.min(jnp.where(es == m1, row, float(epg)), axis=0, keepdims=True)
    es2 = jnp.where(row == i1, -jnp.inf, es)
    m2 = jnp.max(es2, axis=0, keepdims=True)
    i2 = jnp.min(jnp.where(es2 == m2, row, float(epg)), axis=0, keepdims=True)
    z = jnp.exp(m2 - m1)
    den = 1.0 + z
    g0 = p_grp / den
    g1 = p_grp * z / den
    e0 = garg * float(epg) + i1
    e1 = garg * float(epg) + i2
    zero = jnp.zeros_like(g0)
    return jnp.concatenate([g0, g1, e0, e1, zero, zero, zero, zero], axis=0)


def _mix_body(x_ref, mod_ref, g1_ref, g2_ref, win_ref, wout_ref, ws_ref, bst_ref, vg_ref,
              cw_ref, cb_ref, wrt_ref, br_ref,
              x1_ref, h2_ref, eid_ref, gcol_ref, pre_scr, *, tiles_per_seq):
    tm, d = x_ref.shape
    aw = vg_ref.shape[1]
    bw = cb_ref.shape[1]
    hd = aw // A_HEADS
    nch = tm // CHUNK
    i = pl.program_id(0)

    x = x_ref[...]
    mod = mod_ref[0]
    shift1, scale1, gate1 = mod[:, 0:d], mod[:, d:2 * d], mod[:, 2 * d:3 * d]
    shift2, scale2 = mod[:, 3 * d:4 * d], mod[:, 4 * d:5 * d]

    h = _rms(x, g1_ref[...]) * (1.0 + scale1) + shift1
    hb = h.astype(BF16)

    uv = jax.nn.gelu(jnp.dot(hb, win_ref[:, 0:2 * aw], preferred_element_type=F32))
    u, v = uv[:, :aw], uv[:, aw:]
    tq_r = lax.broadcasted_iota(I32, (CHUNK, CHUNK), 0)
    tq_c = lax.broadcasted_iota(I32, (CHUNK, CHUNK), 1)
    causal = tq_c <= tq_r
    ya = []
    for hh in range(A_HEADS):
        sl = slice(hh * hd, (hh + 1) * hd)
        vh = _rms(v[:, sl], vg_ref[:, sl]).astype(BF16)
        rhs = jnp.concatenate([vh[c * CHUNK:(c + 1) * CHUNK] for c in range(nch)], axis=1)
        w = jnp.where(causal, ws_ref[hh], 0.0).astype(BF16)
        zs = jnp.dot(w, rhs, preferred_element_type=F32) + bst_ref[:, hh:hh + 1]
        zs = jnp.concatenate([zs[:, c * hd:(c + 1) * hd] for c in range(nch)], axis=0)
        ya.append(u[:, sl] * zs)

    bcx = jnp.dot(hb, win_ref[:, 2 * aw:], preferred_element_type=F32)
    bg, cg, xin = bcx[:, :bw], bcx[:, bw:2 * bw], bcx[:, 2 * bw:]
    pre = cg * xin

    @pl.when(i % tiles_per_seq == 0)
    def _():
        pre_scr[0:V7X_SUBLANES, :] = jnp.zeros((V7X_SUBLANES, bw), F32)

    pre_scr[V7X_SUBLANES:V7X_SUBLANES + tm, :] = pre
    p1 = pre_scr[V7X_SUBLANES - 1:V7X_SUBLANES - 1 + tm, :]
    p2 = pre_scr[V7X_SUBLANES - 2:V7X_SUBLANES - 2 + tm, :]
    conv = cw_ref[0:1, :] * p2 + cw_ref[1:2, :] * p1 + cw_ref[2:3, :] * pre + cb_ref[...]
    yb = bg * conv
    pre_scr[0:V7X_SUBLANES, :] = pre_scr[tm:tm + V7X_SUBLANES, :]

    y = jnp.concatenate(ya + [yb], axis=1).astype(BF16)
    mix = jnp.dot(y, wout_ref[...], preferred_element_type=F32)
    x1 = x + gate1 * mix
    x1_ref[...] = x1

    h2 = _rms(x1, g2_ref[...]) * (1.0 + scale2) + shift2
    h2_ref[...] = h2

    lt = lax.dot_general(wrt_ref[...], h2.astype(BF16), (((1,), (1,)), ((), ())),
                         preferred_element_type=F32) + br_ref[...]
    slab = _route(lt, tm)
    eid_ref[0] = slab[2:4].astype(I32)
    wide = jnp.concatenate([slab, jnp.zeros((V7X_LANES - V7X_SUBLANES, tm), F32)], axis=0)
    gcol_ref[...] = wide.T


def _mix(xt, mod3, g1, g2, win, wout, ws, bst, vg, cw, cb, wrt, br, seq):
    t, d = xt.shape
    tm = TOKEN_TILE
    nt = t // tm
    in_cols = win.shape[1]
    bw = cb.shape[1]
    const2 = lambda i: (0, 0)
    single = dict(pipeline_mode=pl.Buffered(1))
    body = functools.partial(_mix_body, tiles_per_seq=seq // tm)
    vmem = (2 * d * (in_cols + d)
            + 3 * 2 * tm * d * 4
            + (tm + V7X_SUBLANES) * bw * 4
            + tm * (in_cols + 4 * d) * 4
            + 8 * 1024 * 1024)
    return pl.pallas_call(
        body,
        grid=(nt,),
        in_specs=[
            pl.BlockSpec((tm, d), lambda i: (i, 0)),
            pl.BlockSpec((1, 1, mod3.shape[2]), lambda i: (i // (seq // tm), 0, 0)),
            pl.BlockSpec((1, d), const2),
            pl.BlockSpec((1, d), const2),
            pl.BlockSpec(win.shape, const2, **single),
            pl.BlockSpec(wout.shape, const2, **single),
            pl.BlockSpec(ws.shape, lambda i: (0, 0, 0)),
            pl.BlockSpec(bst.shape, const2),
            pl.BlockSpec(vg.shape, const2),
            pl.BlockSpec(cw.shape, const2),
            pl.BlockSpec(cb.shape, const2),
            pl.BlockSpec(wrt.shape, const2),
            pl.BlockSpec(br.shape, const2),
        ],
        out_specs=[
            pl.BlockSpec((tm, d), lambda i: (i, 0)),
            pl.BlockSpec((tm, d), lambda i: (i, 0)),
            pl.BlockSpec((1, TOP_K, tm), lambda i: (i, 0, 0)),
            pl.BlockSpec((tm, V7X_LANES), lambda i: (i, 0)),
        ],
        out_shape=[
            jax.ShapeDtypeStruct((t, d), F32),
            jax.ShapeDtypeStruct((t, d), F32),
            jax.ShapeDtypeStruct((nt, TOP_K, tm), I32),
            jax.ShapeDtypeStruct((t, V7X_LANES), F32),
        ],
        scratch_shapes=[pltpu.VMEM((tm + V7X_SUBLANES, bw), F32)],
        compiler_params=_params(min(vmem, V7X_VMEM_BYTES - 6 * 1024 * 1024), ("arbitrary",)),
        name="mix",
    )(xt, mod3, g1, g2, win, wout, ws, bst, vg, cw, cb, wrt, br)


def _dispatch_body(eid_ref, dest_ref, texp_ref, pstart_ref, plen_ref, rank_scr, *, n_e, n_tiles):
    nch, _, c = eid_ref.shape
    te = EXPERT_TILE
    r = lax.broadcasted_iota(I32, (c, c), 0)
    q = lax.broadcasted_iota(I32, (c, c), 1)
    before = (r < q).astype(BF16)
    e_iota = lax.broadcasted_iota(I32, (n_e, c), 0)

    def count_body(ch, carry):
        e2 = eid_ref[ch]
        ranks = []
        for k in range(TOP_K):
            oh = e_iota == e2[k:k + 1]
            ohf = oh.astype(F32)
            pref = jnp.dot(ohf.astype(BF16), before, preferred_element_type=F32)
            ranks.append(jnp.sum(jnp.where(oh, pref + carry, 0.0), axis=0, keepdims=True))
            carry = carry + jnp.sum(ohf, axis=1, keepdims=True)
        rank_scr[ch] = jnp.concatenate(ranks, axis=0)
        return carry

    counts = lax.fori_loop(0, nch, count_body, jnp.zeros((n_e, 1), F32))

    padded = jnp.floor((counts + float(te - 1)) / float(te)) * float(te)
    sub = lax.broadcasted_iota(I32, (n_e, V7X_LANES), 0)
    lane = lax.broadcasted_iota(I32, (n_e, V7X_LANES), 1)
    pstart_row = jnp.sum(jnp.where(sub < lane, padded, 0.0), axis=0, keepdims=True)
    pstart = jnp.sum(jnp.where(sub == lane, pstart_row, 0.0), axis=1, keepdims=True)
    pend = pstart + padded
    total = jnp.max(pend, axis=0, keepdims=True)
    last_e = jnp.max(jnp.where(counts > 0.0, sub[:, 0:1].astype(F32), -1.0), axis=0, keepdims=True)

    def dest_body(ch, carry):
        e2 = eid_ref[ch]
        rk = rank_scr[ch]
        rows = []
        for k in range(TOP_K):
            oh = e_iota == e2[k:k + 1]
            off = jnp.sum(jnp.where(oh, pstart, 0.0), axis=0, keepdims=True)
            rows.append(off + rk[k:k + 1])
        dest_ref[ch] = jnp.concatenate(rows, axis=0).astype(I32)
        return carry

    lax.fori_loop(0, nch, dest_body, 0)

    w = texp_ref.shape[1]
    tj = lax.broadcasted_iota(I32, (n_e, w), 1).astype(F32) * float(te)
    texp = jnp.sum((pend <= tj).astype(F32), axis=0, keepdims=True)
    tj1 = tj[0:1]
    texp = jnp.where(tj1 < total, jnp.minimum(texp, float(n_e - 1)), last_e)
    lane_w = lax.broadcasted_iota(I32, (1, w), 1)
    texp = jnp.where(lane_w == n_tiles, total / float(te), texp)
    texp_ref[...] = texp.astype(I32)
    pstart_ref[...] = jnp.broadcast_to(pstart + counts, pstart_ref.shape).astype(I32)
    plen_ref[...] = jnp.broadcast_to(padded - counts, plen_ref.shape).astype(I32)


def _dispatch(eid, n_e, n_tiles):
    nch, k, c = eid.shape
    body = functools.partial(_dispatch_body, n_e=n_e, n_tiles=n_tiles)
    w = 2 * V7X_LANES
    assert n_tiles < w
    return pl.pallas_call(
        body,
        out_shape=[
            jax.ShapeDtypeStruct((nch, k, c), I32),
            jax.ShapeDtypeStruct((1, w), I32),
            jax.ShapeDtypeStruct((n_e, V7X_LANES), I32),
            jax.ShapeDtypeStruct((n_e, V7X_LANES), I32),
        ],
        scratch_shapes=[pltpu.VMEM((nch, k, c), F32)],
        compiler_params=_params(16 * 1024 * 1024),
        name="dispatch",
    )(eid)


def _pad_bits():
    b = EXPERT_TILE // 2
    while b >= V7X_SUBLANES:
        yield b
        b //= 2


def _scatter_body(dest_ref, pstart_ref, plen_ref, texp_ref, h2_ref, xs_ref, zbuf, sem, zsem,
                  *, n_e, n_tiles):
    tm = h2_ref.shape[0]
    te = EXPERT_TILE
    i = pl.program_id(0)

    def zero_fill(wait):
        def run(cp):
            if wait:
                cp.wait()
            else:
                cp.start()

        def per_expert(e, carry):
            start = pstart_ref[e]
            ln = plen_ref[e]
            head = jnp.minimum((-start) & (V7X_SUBLANES - 1), ln)
            for r in range(V7X_SUBLANES - 1):
                @pl.when(r < head)
                def _(r=r):
                    run(pltpu.make_async_copy(zbuf.at[pl.ds(0, 1), :], xs_ref.at[pl.ds(start + r, 1), :], zsem))
            pos = start + head
            rest = ln - head
            for b in _pad_bits():
                @pl.when((rest & b) != 0)
                def _(pos=pos, b=b):
                    dst = pl.multiple_of(pos, V7X_SUBLANES)
                    run(pltpu.make_async_copy(zbuf.at[pl.ds(0, b), :], xs_ref.at[pl.ds(dst, b), :], zsem))
                pos = pos + (rest & b)
            return carry

        lax.fori_loop(0, n_e, per_expert, 0)

        def tail(j, carry):
            run(pltpu.make_async_copy(zbuf, xs_ref.at[pl.ds(j * te, te), :], zsem))
            return carry

        lax.fori_loop(texp_ref[n_tiles], n_tiles, tail, 0)

    @pl.when(i == 0)
    def _():
        zbuf[...] = jnp.zeros(zbuf.shape, F32)
        zero_fill(False)

    def issue(r, carry):
        for k in range(TOP_K):
            dst = dest_ref[(i * TOP_K + k) * tm + r]
            pltpu.make_async_copy(h2_ref.at[pl.ds(r, 1), :], xs_ref.at[pl.ds(dst, 1), :], sem).start()
        return carry

    lax.fori_loop(0, tm, issue, 0, unroll=8)
    for k in range(TOP_K):
        pltpu.make_async_copy(h2_ref, xs_ref.at[pl.ds(0, tm), :], sem).wait()

    @pl.when(i == pl.num_programs(0) - 1)
    def _():
        zero_fill(True)


def _scatter(dest, pstart, plen, texp, h2, n_e, n_tiles):
    t, d = h2.shape
    tm = TOKEN_TILE
    body = functools.partial(_scatter_body, n_e=n_e, n_tiles=n_tiles)
    return pl.pallas_call(
        body,
        grid_spec=pltpu.PrefetchScalarGridSpec(
            num_scalar_prefetch=4,
            grid=(t // tm,),
            in_specs=[pl.BlockSpec((tm, d), lambda i, *_: (i, 0))],
            out_specs=pl.BlockSpec(memory_space=pl.ANY),
            scratch_shapes=[
                pltpu.VMEM((EXPERT_TILE, d), F32),
                pltpu.SemaphoreType.DMA,
                pltpu.SemaphoreType.DMA,
            ],
        ),
        out_shape=jax.ShapeDtypeStruct((n_tiles * EXPERT_TILE, d), F32),
        compiler_params=_params(16 * 1024 * 1024, ("arbitrary",)),
        name="scatter",
    )(dest, pstart, plen, texp, h2)


def _expert_body(texp_ref, x_ref, wgu_ref, wd_ref, o_ref, wgu_s, wd_s, *, n_tiles):
    j = pl.program_id(0)
    de = wd_s.shape[0]
    e = texp_ref[j]
    e_prev = texp_ref[jnp.maximum(j - 1, 0)]

    @pl.when(jnp.logical_or(j == 0, e != e_prev))
    def _():
        wgu_s[...] = wgu_ref[0].astype(BF16)
        wd_s[...] = wd_ref[0].astype(BF16)

    @pl.when(j < texp_ref[n_tiles])
    def _():
        gu = jnp.dot(x_ref[...].astype(BF16), wgu_s[...], preferred_element_type=F32)
        a = (jax.nn.silu(gu[:, :de]) * gu[:, de:]).astype(BF16)
        o_ref[...] = jnp.dot(a, wd_s[...], preferred_element_type=F32)

    @pl.when(j >= texp_ref[n_tiles])
    def _():
        o_ref[...] = jnp.zeros(o_ref.shape, F32)


def _experts(texp, xs, wgu, wd, n_tiles):
    rows, d = xs.shape
    te = EXPERT_TILE
    _, _, n_gu = wgu.shape
    de = wd.shape[1]
    body = functools.partial(_expert_body, n_tiles=n_tiles)
    vmem = (2 * (d * n_gu + de * d) * 4 + (d * n_gu + de * d) * 2 + 4 * te * d * 4
            + te * (n_gu + d) * 4 + 4 * 1024 * 1024)
    return pl.pallas_call(
        body,
        grid_spec=pltpu.PrefetchScalarGridSpec(
            num_scalar_prefetch=1,
            grid=(n_tiles,),
            in_specs=[
                pl.BlockSpec((te, d), lambda j, texp: (j, 0)),
                pl.BlockSpec((1, d, n_gu), lambda j, texp: (texp[j], 0, 0)),
                pl.BlockSpec((1, de, d), lambda j, texp: (texp[j], 0, 0)),
            ],
            out_specs=pl.BlockSpec((te, d), lambda j, texp: (j, 0)),
            scratch_shapes=[pltpu.VMEM((d, n_gu), BF16), pltpu.VMEM((de, d), BF16)],
        ),
        out_shape=jax.ShapeDtypeStruct((rows, d), F32),
        compiler_params=_params(vmem, ("arbitrary",)),
        name="experts",
    )(texp, xs, wgu, wd)


def _combine_body(dest_ref, x1_ref, gcol_ref, mod_ref, modf_ref, gf_ref, ys_ref, o_ref, ybuf, sem):
    tm, d = x1_ref.shape
    i = pl.program_id(0)

    def issue(r, carry):
        for k in range(TOP_K):
            src = dest_ref[(i * TOP_K + k) * tm + r]
            pltpu.make_async_copy(ys_ref.at[pl.ds(src, 1), :], ybuf.at[k, pl.ds(r, 1), :], sem).start()
        return carry

    lax.fori_loop(0, tm, issue, 0, unroll=8)
    for k in range(TOP_K):
        pltpu.make_async_copy(ys_ref.at[pl.ds(0, tm), :], ybuf.at[k], sem).wait()

    gate2 = mod_ref[0][:, 5 * d:6 * d]
    modf = modf_ref[0]
    shift_f, scale_f = modf[:, 0:d], modf[:, d:2 * d]
    ffn = ybuf[0] * gcol_ref[:, 0:1] + ybuf[1] * gcol_ref[:, 1:2]
    x2 = x1_ref[...] + gate2 * ffn
    o_ref[...] = _rms(x2, gf_ref[...]) * (1.0 + scale_f) + shift_f


def _combine(dest, x1, gcol, mod3, modf3, gf, ys, seq):
    t, d = x1.shape
    tm = TOKEN_TILE
    per_seq = seq // tm
    return pl.pallas_call(
        _combine_body,
        grid_spec=pltpu.PrefetchScalarGridSpec(
            num_scalar_prefetch=1,
            grid=(t // tm,),
            in_specs=[
                pl.BlockSpec((tm, d), lambda i, dest: (i, 0)),
                pl.BlockSpec((tm, V7X_LANES), lambda i, dest: (i, 0)),
                pl.BlockSpec((1, 1, mod3.shape[2]), lambda i, dest: (i // per_seq, 0, 0)),
                pl.BlockSpec((1, 1, modf3.shape[2]), lambda i, dest: (i // per_seq, 0, 0)),
                pl.BlockSpec((1, d), lambda i, dest: (0, 0)),
                pl.BlockSpec(memory_space=pl.ANY),
            ],
            out_specs=pl.BlockSpec((tm, d), lambda i, dest: (i, 0)),
            scratch_shapes=[pltpu.VMEM((TOP_K, tm, d), F32), pltpu.SemaphoreType.DMA],
        ),
        out_shape=jax.ShapeDtypeStruct((t, d), F32),
        compiler_params=_params(32 * 1024 * 1024, ("arbitrary",)),
        name="combine",
    )(dest, x1, gcol, mod3, modf3, gf, ys)


def kernel(x, c, w_ada, b_ada, norm1_g, w_in, w_out, gmlp_w_s, gmlp_b_s, gmlp_v_gain, conv_w, conv_b,
           norm2_g, w_router_group, b_router_group, w_router_expert, b_router_expert, w_gate_up, w_down,
           w_ada_final, b_ada_final, norm_f_g):
    bsz, seq, d = x.shape
    depth = w_ada.shape[0]
    n_e = w_router_expert.shape[2]
    t = bsz * seq
    assert seq % TOKEN_TILE == 0 and TOKEN_TILE % CHUNK == 0
    assert w_router_group.shape[2] == N_GROUPS and n_e + N_GROUPS <= ROUTER_ROWS
    n_tiles = (t * TOP_K + n_e * (EXPERT_TILE - 1)) // EXPERT_TILE

    modf3 = _ada(c, w_ada_final, b_ada_final).reshape(bsz, 1, 2 * d)
    xt = x.reshape(t, d)
    for l in range(depth):
        mod3 = _ada(c, w_ada[l], b_ada[l]).reshape(bsz, 1, -1)
        wr = jnp.concatenate([w_router_expert[l], w_router_group[l]], axis=1)
        wrt = jnp.pad(wr, ((0, 0), (0, ROUTER_ROWS - wr.shape[1]))).T.astype(BF16)
        br = jnp.concatenate([b_router_expert[l], b_router_group[l]])
        br = jnp.pad(br, (0, ROUTER_ROWS - br.shape[0])).reshape(ROUTER_ROWS, 1)
        x1, h2, eid, gcol = _mix(
            xt, mod3, norm1_g[l].reshape(1, d), norm2_g[l].reshape(1, d),
            w_in[l].astype(BF16), w_out[l].astype(BF16), gmlp_w_s[l], gmlp_b_s[l].T,
            gmlp_v_gain[l].reshape(1, -1), conv_w[l], conv_b[l].reshape(1, -1), wrt, br, seq)
        dest, texp, pstart, plen = _dispatch(eid, n_e, n_tiles)
        dest = dest.reshape(-1)
        texp = texp.reshape(-1)
        xs = _scatter(dest, pstart[:, 0], plen[:, 0], texp, h2, n_e, n_tiles)
        ys = _experts(texp, xs, w_gate_up[l], w_down[l], n_tiles)
        assert depth == 1
        xt = _combine(dest, x1, gcol, mod3, modf3, norm_f_g.reshape(1, d), ys, seq)
    return xt.reshape(bsz, seq, d)
```

```python
import functools

import jax
import jax.numpy as jnp
from jax import lax
from jax.experimental import pallas as pl
from jax.experimental.pallas import tpu as pltpu

F32 = jnp.float32
BF16 = jnp.bfloat16
I32 = jnp.int32

A_HEADS = 8
CHUNK = 128
N_GROUPS = 4
TOP_K = 2
EPS = 1e-6

V7X_LANES = 128
V7X_SUBLANES = 8
V7X_VMEM_BYTES = 64 * 1024 * 1024

TOKEN_TILE = 256
EXPERT_TILE = 256
ROUTER_ROWS = 40


def _rms(x, g):
    y = x * lax.rsqrt(jnp.mean(x * x, axis=-1, keepdims=True) + EPS)
    return y * g


def _params(vmem_bytes, semantics=None):
    kw = dict(vmem_limit_bytes=int(vmem_bytes))
    if semantics is not None:
        kw["dimension_semantics"] = semantics
    return pltpu.CompilerParams(**kw)


def _ada_body(c_ref, w_ref, b_ref, o_ref):
    ca = jax.nn.silu(c_ref[...]).astype(BF16)
    o_ref[...] = jnp.dot(ca, w_ref[...].astype(BF16), preferred_element_type=F32) + b_ref[...]


def _ada(c, w, b, tn=1024):
    bsz, d = c.shape
    n = w.shape[1]
    return pl.pallas_call(
        _ada_body,
        grid=(n // tn,),
        in_specs=[
            pl.BlockSpec((bsz, d), lambda j: (0, 0)),
            pl.BlockSpec((d, tn), lambda j: (0, j)),
            pl.BlockSpec((1, tn), lambda j: (0, j)),
        ],
        out_specs=pl.BlockSpec((bsz, tn), lambda j: (0, j)),
        out_shape=jax.ShapeDtypeStruct((bsz, n), F32),
        compiler_params=_params(4 * d * tn * 4, ("arbitrary",)),
        name="ada",
    )(c, w, b.reshape(1, n))


def _route(lt, tm):
    n_e = 32
    epg = n_e // N_GROUPS
    row = lax.broadcasted_iota(I32, (V7X_SUBLANES, tm), 0).astype(F32)
    gl = lt[n_e:n_e + V7X_SUBLANES]
    gvalid = row < float(N_GROUPS)
    glm = jnp.where(gvalid, gl, -jnp.inf)
    gmax = jnp.max(glm, axis=0, keepdims=True)
    garg = jnp.min(jnp.where(glm == gmax, row, float(V7X_SUBLANES)), axis=0, keepdims=True)
    gsum = jnp.sum(jnp.where(gvalid, jnp.exp(gl - gmax), 0.0), axis=0, keepdims=True)
    p_grp = 1.0 / gsum
    es = lt[0:epg]
    for g in range(1, N_GROUPS):
        es = jnp.where(garg == float(g), lt[g * epg:(g + 1) * epg], es)
    m1 = jnp.max(es, axis=0, keepdims=True)
    i1 = jnp.min(jnp.where(es == m1, row, float(epg)), axis=0, keepdims=True)
    es2 = jnp.where(row == i1, -jnp.inf, es)
    m2 = jnp.max(es2, axis=0, keepdims=True)
    i2 = jnp.min(jnp.where(es2 == m2, row, float(epg)), axis=0, keepdims=True)
    z = jnp.exp(m2 - m1)
    den = 1.0 + z
    g0 = p_grp / den
    g1 = p_grp * z / den
    e0 = garg * float(epg) + i1
    e1 = garg * float(epg) + i2
    zero = jnp.zeros_like(g0)
    return jnp.concatenate([g0, g1, e0, e1, zero, zero, zero, zero], axis=0)


def _mix_body(x_ref, mod_ref, g1_ref, g2_ref, win_ref, wout_ref, ws_ref, bst_ref, vg_ref,
              cw_ref, cb_ref, wrt_ref, br_ref,
              x1_ref, h2_ref, eid_ref, gcol_ref, pre_scr, *, tiles_per_seq):
    tm, d = x_ref.shape
    aw = vg_ref.shape[1]
    bw = cb_ref.shape[1]
    hd = aw // A_HEADS
    nch = tm // CHUNK
    i = pl.program_id(0)

    x = x_ref[...]
    mod = mod_ref[0]
    shift1, scale1, gate1 = mod[:, 0:d], mod[:, d:2 * d], mod[:, 2 * d:3 * d]
    shift2, scale2 = mod[:, 3 * d:4 * d], mod[:, 4 * d:5 * d]

    h = _rms(x, g1_ref[...]) * (1.0 + scale1) + shift1
    hb = h.astype(BF16)

    uv = jax.nn.gelu(jnp.dot(hb, win_ref[:, 0:2 * aw], preferred_element_type=F32))
    u, v = uv[:, :aw], uv[:, aw:]
    tq_r = lax.broadcasted_iota(I32, (CHUNK, CHUNK), 0)
    tq_c = lax.broadcasted_iota(I32, (CHUNK, CHUNK), 1)
    causal = tq_c <= tq_r
    ya = []
    for hh in range(A_HEADS):
        sl = slice(hh * hd, (hh + 1) * hd)
        vh = _rms(v[:, sl], vg_ref[:, sl]).astype(BF16)
        rhs = jnp.concatenate([vh[c * CHUNK:(c + 1) * CHUNK] for c in range(nch)], axis=1)
        w = jnp.where(causal, ws_ref[hh], 0.0).astype(BF16)
        zs = jnp.dot(w, rhs, preferred_element_type=F32) + bst_ref[:, hh:hh + 1]
        zs = jnp.concatenate([zs[:, c * hd:(c + 1) * hd] for c in range(nch)], axis=0)
        ya.append(u[:, sl] * zs)

    bcx = jnp.dot(hb, win_ref[:, 2 * aw:], preferred_element_type=F32)
    bg, cg, xin = bcx[:, :bw], bcx[:, bw:2 * bw], bcx[:, 2 * bw:]
    pre = cg * xin

    @pl.when(i % tiles_per_seq == 0)
    def _():
        pre_scr[0:V7X_SUBLANES, :] = jnp.zeros((V7X_SUBLANES, bw), F32)

    pre_scr[V7X_SUBLANES:V7X_SUBLANES + tm, :] = pre
    p1 = pre_scr[V7X_SUBLANES - 1:V7X_SUBLANES - 1 + tm, :]
    p2 = pre_scr[V7X_SUBLANES - 2:V7X_SUBLANES - 2 + tm, :]
    conv = cw_ref[0:1, :] * p2 + cw_ref[1:2, :] * p1 + cw_ref[2:3, :] * pre + cb_ref[...]
    yb = bg * conv
    pre_scr[0:V7X_SUBLANES, :] = pre_scr[tm:tm + V7X_SUBLANES, :]

    y = jnp.concatenate(ya + [yb], axis=1).astype(BF16)
    mix = jnp.dot(y, wout_ref[...], preferred_element_type=F32)
    x1 = x + gate1 * mix
    x1_ref[...] = x1

    h2 = _rms(x1, g2_ref[...]) * (1.0 + scale2) + shift2
    h2_ref[...] = h2

    lt = lax.dot_general(wrt_ref[...], h2.astype(BF16), (((1,), (1,)), ((), ())),
                         preferred_element_type=F32) + br_ref[...]
    slab = _route(lt, tm)
    eid_ref[0] = slab[2:4].astype(I32)
    wide = jnp.concatenate([slab, jnp.zeros((V7X_LANES - V7X_SUBLANES, tm), F32)], axis=0)
    gcol_ref[...] = wide.T


def _mix(xt, mod3, g1, g2, win, wout, ws, bst, vg, cw, cb, wrt, br, seq):
    t, d = xt.shape
    tm = TOKEN_TILE
    nt = t // tm
    in_cols = win.shape[1]
    bw = cb.shape[1]
    const2 = lambda i: (0, 0)
    single = dict(pipeline_mode=pl.Buffered(1))
    body = functools.partial(_mix_body, tiles_per_seq=seq // tm)
    vmem = (2 * d * (in_cols + d)
            + 3 * 2 * tm * d * 4
            + (tm + V7X_SUBLANES) * bw * 4
            + tm * (in_cols + 4 * d) * 4
            + 8 * 1024 * 1024)
    return pl.pallas_call(
        body,
        grid=(nt,),
        in_specs=[
            pl.BlockSpec((tm, d), lambda i: (i, 0)),
            pl.BlockSpec((1, 1, mod3.shape[2]), lambda i: (i // (seq // tm), 0, 0)),
            pl.BlockSpec((1, d), const2),
            pl.BlockSpec((1, d), const2),
            pl.BlockSpec(win.shape, const2, **single),
            pl.BlockSpec(wout.shape, const2, **single),
            pl.BlockSpec(ws.shape, lambda i: (0, 0, 0)),
            pl.BlockSpec(bst.shape, const2),
            pl.BlockSpec(vg.shape, const2),
            pl.BlockSpec(cw.shape, const2),
            pl.BlockSpec(cb.shape, const2),
            pl.BlockSpec(wrt.shape, const2),
            pl.BlockSpec(br.shape, const2),
        ],
        out_specs=[
            pl.BlockSpec((tm, d), lambda i: (i, 0)),
            pl.BlockSpec((tm, d), lambda i: (i, 0)),
            pl.BlockSpec((1, TOP_K, tm), lambda i: (i, 0, 0)),
            pl.BlockSpec((tm, V7X_LANES), lambda i: (i, 0)),
        ],
        out_shape=[
            jax.ShapeDtypeStruct((t, d), F32),
            jax.ShapeDtypeStruct((t, d), F32),
            jax.ShapeDtypeStruct((nt, TOP_K, tm), I32),
            jax.ShapeDtypeStruct((t, V7X_LANES), F32),
        ],
        scratch_shapes=[pltpu.VMEM((tm + V7X_SUBLANES, bw), F32)],
        compiler_params=_params(min(vmem, V7X_VMEM_BYTES - 6 * 1024 * 1024), ("arbitrary",)),
        name="mix",
    )(xt, mod3, g1, g2, win, wout, ws, bst, vg, cw, cb, wrt, br)


def _dispatch_body(eid_ref, texp_ref, tab_ref, rank_scr, dest_v, dest_s, init_v, sem,
                   *, n_e, n_tiles, n_tok):
    nch, _, c = eid_ref.shape
    te = EXPERT_TILE
    r = lax.broadcasted_iota(I32, (c, c), 0)
    q = lax.broadcasted_iota(I32, (c, c), 1)
    before = (r < q).astype(BF16)
    e_iota = lax.broadcasted_iota(I32, (n_e, c), 0)

    init_v[...] = (lax.broadcasted_iota(I32, init_v.shape, 0) & (te - 1)) + TOP_K * n_tok
    cp_i = pltpu.make_async_copy(init_v, tab_ref, sem.at[0])
    cp_i.start()

    def count_body(ch, carry):
        e2 = eid_ref[ch]
        ranks = []
        for k in range(TOP_K):
            oh = e_iota == e2[k:k + 1]
            ohf = oh.astype(F32)
            pref = jnp.dot(ohf.astype(BF16), before, preferred_element_type=F32)
            ranks.append(jnp.sum(jnp.where(oh, pref + carry, 0.0), axis=0, keepdims=True))
            carry = carry + jnp.sum(ohf, axis=1, keepdims=True)
        rank_scr[ch] = jnp.concatenate(ranks, axis=0)
        return carry

    counts = lax.fori_loop(0, nch, count_body, jnp.zeros((n_e, 1), F32))

    padded = jnp.floor((counts + float(te - 1)) / float(te)) * float(te)
    sub = lax.broadcasted_iota(I32, (n_e, V7X_LANES), 0)
    lane = lax.broadcasted_iota(I32, (n_e, V7X_LANES), 1)
    pstart_row = jnp.sum(jnp.where(sub < lane, padded, 0.0), axis=0, keepdims=True)
    pstart = jnp.sum(jnp.where(sub == lane, pstart_row, 0.0), axis=1, keepdims=True)
    pend = pstart + padded
    total = jnp.max(pend, axis=0, keepdims=True)
    last_e = jnp.max(jnp.where(counts > 0.0, sub[:, 0:1].astype(F32), -1.0), axis=0, keepdims=True)

    group = V7X_SUBLANES // TOP_K

    def dest_body(g, carry):
        rows = []
        for j in range(group):
            e2 = eid_ref[g * group + j]
            rk = rank_scr[g * group + j]
            for k in range(TOP_K):
                oh = e_iota == e2[k:k + 1]
                off = jnp.sum(jnp.where(oh, pstart, 0.0), axis=0, keepdims=True)
                rows.append(off + rk[k:k + 1])
        row0 = pl.multiple_of(g * V7X_SUBLANES, V7X_SUBLANES)
        dest_v[pl.ds(row0, V7X_SUBLANES), :] = jnp.concatenate(rows, axis=0).astype(I32)
        return carry

    lax.fori_loop(0, nch // group, dest_body, 0)
    cp_d = pltpu.make_async_copy(dest_v, dest_s, sem.at[1])
    cp_d.start()

    w = texp_ref.shape[1]
    tj = lax.broadcasted_iota(I32, (n_e, w), 1).astype(F32) * float(te)
    texp = jnp.sum((pend <= tj).astype(F32), axis=0, keepdims=True)
    tj1 = tj[0:1]
    texp = jnp.where(tj1 < total, jnp.minimum(texp, float(n_e - 1)), last_e)
    lane_w = lax.broadcasted_iota(I32, (1, w), 1)
    texp = jnp.where(lane_w == n_tiles, total / float(te), texp)
    texp_ref[...] = texp.astype(I32)

    cp_i.wait()
    cp_d.wait()

    def row_body(rw, carry):
        a0 = (rw % TOP_K) * n_tok + (rw // TOP_K) * c
        for col in range(c):
            tab_ref[dest_s[rw, col]] = a0 + col
        return carry

    lax.fori_loop(0, nch * TOP_K, row_body, 0)


def _dispatch(eid, n_e, n_tiles, n_tok):
    nch, k, c = eid.shape
    assert c == EXPERT_TILE and EXPERT_TILE & (EXPERT_TILE - 1) == 0
    body = functools.partial(_dispatch_body, n_e=n_e, n_tiles=n_tiles, n_tok=n_tok)
    w = 2 * V7X_LANES
    assert n_tiles < w
    rows = n_tiles + V7X_SUBLANES
    smem = pl.BlockSpec(memory_space=pltpu.SMEM)
    return pl.pallas_call(
        body,
        out_specs=[pl.BlockSpec(memory_space=pltpu.VMEM), smem],
        out_shape=[
            jax.ShapeDtypeStruct((1, w), I32),
            jax.ShapeDtypeStruct((rows * EXPERT_TILE,), I32),
        ],
        scratch_shapes=[
            pltpu.VMEM((nch, k, c), F32),
            pltpu.VMEM((nch * k, c), I32),
            pltpu.SMEM((nch * k, c), I32),
            pltpu.VMEM((rows * EXPERT_TILE,), I32),
            pltpu.SemaphoreType.DMA((2,)),
        ],
        compiler_params=_params(16 * 1024 * 1024),
        name="dispatch",
    )(eid)


def _expert_body(tab_ref, texp_ref, h2_hbm, wgu_hbm, wd_hbm, y2_hbm,
                 xa, xb, ya, yb, wgu_f, wd_f, wgu_s, wd_s, gsem, ssem, wsem, *, n_tiles):
    te = EXPERT_TILE
    de = wd_s.shape[0]
    n_tok = h2_hbm.shape[0]
    p = pl.program_id(0)
    n_used = texp_ref[n_tiles]
    ta = 2 * p
    tb = ta + 1

    def gather(tile, buf, sem):
        for r in range(te):
            tok = tab_ref[tile * te + r] & (n_tok - 1)
            pltpu.make_async_copy(h2_hbm.at[pl.ds(tok, 1), :], buf.at[pl.ds(r, 1), :], sem).start()

    def scatter(tile, buf, sem):
        for r in range(te):
            row = tab_ref[tile * te + r]
            pltpu.make_async_copy(buf.at[pl.ds(r, 1), :], y2_hbm.at[pl.ds(row, 1), :], sem).start()

    def wait_rows(buf, sem):
        pltpu.make_async_copy(h2_hbm.at[pl.ds(0, te), :], buf, sem).wait()

    def weight_copies(e):
        return (pltpu.make_async_copy(wgu_hbm.at[e], wgu_f, wsem.at[0]),
                pltpu.make_async_copy(wd_hbm.at[e], wd_f, wsem.at[1]))

    def switch_weights(tile):
        e = texp_ref[tile]
        first = jnp.logical_or(tile == 0, texp_ref[jnp.maximum(tile - 1, 0)] != e)

        @pl.when(jnp.logical_and(first, tile < n_used))
        def _():
            @pl.when(tile == 0)
            def _():
                for cp in weight_copies(e):
                    cp.start()

            for cp in weight_copies(e):
                cp.wait()
            wgu_s[...] = wgu_f[...].astype(BF16)
            wd_s[...] = wd_f[...].astype(BF16)
            nxt = lax.while_loop(
                lambda k: jnp.logical_and(k < n_used, texp_ref[jnp.minimum(k, n_tiles - 1)] == e),
                lambda k: k + 1, tile + 1)

            @pl.when(nxt < n_used)
            def _():
                for cp in weight_copies(texp_ref[jnp.minimum(nxt, n_tiles - 1)]):
                    cp.start()

    def compute(xbuf, ybuf):
        gu = jnp.dot(xbuf[...].astype(BF16), wgu_s[...], preferred_element_type=F32)
        a = (jax.nn.silu(gu[:, :de]) * gu[:, de:]).astype(BF16)
        ybuf[...] = jnp.dot(a, wd_s[...], preferred_element_type=F32)

    @pl.when(ta < n_used)
    def _():
        @pl.when(p == 0)
        def _():
            gather(0, xa, gsem.at[0])
            yb[...] = jnp.zeros(yb.shape, F32)

        switch_weights(ta)
        wait_rows(xa, gsem.at[0])

        @pl.when(p > 0)
        def _():
            wait_rows(ya, ssem.at[0])

        gather(tb, xb, gsem.at[1])
        scatter(jnp.where(p == 0, n_tiles, tb - 2), yb, ssem.at[1])
        compute(xa, ya)

        switch_weights(tb)
        wait_rows(xb, gsem.at[1])
        wait_rows(yb, ssem.at[1])

        scatter(ta, ya, ssem.at[0])
        gather(ta + 2, xa, gsem.at[0])
        compute(xb, yb)

        @pl.when(ta + 2 >= n_used)
        def _():
            wait_rows(ya, ssem.at[0])
            scatter(tb, yb, ssem.at[1])
            wait_rows(xa, gsem.at[0])
            wait_rows(yb, ssem.at[1])


def _experts(tab, texp, h2, wgu, wd, n_tiles):
    t, d = h2.shape
    te = EXPERT_TILE
    _, _, n_gu = wgu.shape
    de = wd.shape[1]
    assert n_tiles % 2 == 0 and t & (t - 1) == 0
    body = functools.partial(_expert_body, n_tiles=n_tiles)
    vmem = ((d * n_gu + de * d) * (4 + 2) + 4 * te * d * 4 + te * (n_gu + d) * 4 + 4 * 1024 * 1024)
    hbm = pl.BlockSpec(memory_space=pl.ANY)
    return pl.pallas_call(
        body,
        grid_spec=pltpu.PrefetchScalarGridSpec(
            num_scalar_prefetch=2,
            grid=(n_tiles // 2,),
            in_specs=[hbm, hbm, hbm],
            out_specs=hbm,
            scratch_shapes=[
                pltpu.VMEM((te, d), F32), pltpu.VMEM((te, d), F32),
                pltpu.VMEM((te, d), F32), pltpu.VMEM((te, d), F32),
                pltpu.VMEM((d, n_gu), F32), pltpu.VMEM((de, d), F32),
                pltpu.VMEM((d, n_gu), BF16), pltpu.VMEM((de, d), BF16),
                pltpu.SemaphoreType.DMA((2,)), pltpu.SemaphoreType.DMA((2,)), pltpu.SemaphoreType.DMA((2,)),
            ],
        ),
        out_shape=jax.ShapeDtypeStruct((TOP_K * t + te, d), F32),
        compiler_params=_params(vmem, ("arbitrary",)),
        name="experts",
    )(tab, texp, h2, wgu, wd)


def _combine_body(x1_ref, y0_ref, y1_ref, gcol_ref, mod_ref, modf_ref, gf_ref, o_ref):
    d = x1_ref.shape[1]
    gate2 = mod_ref[0][:, 5 * d:6 * d]
    modf = modf_ref[0]
    shift_f, scale_f = modf[:, 0:d], modf[:, d:2 * d]
    ffn = y0_ref[...] * gcol_ref[:, 0:1] + y1_ref[...] * gcol_ref[:, 1:2]
    x2 = x1_ref[...] + gate2 * ffn
    o_ref[...] = _rms(x2, gf_ref[...]) * (1.0 + scale_f) + shift_f


def _combine(x1, y2, gcol, mod3, modf3, gf, seq):
    t, d = x1.shape
    tm = TOKEN_TILE
    per_seq = seq // tm
    return pl.pallas_call(
        _combine_body,
        grid=(t // tm,),
        in_specs=[
            pl.BlockSpec((tm, d), lambda i: (i, 0)),
            pl.BlockSpec((tm, d), lambda i: (i, 0)),
            pl.BlockSpec((tm, d), lambda i: (t // tm + i, 0)),
            pl.BlockSpec((tm, V7X_LANES), lambda i: (i, 0)),
            pl.BlockSpec((1, 1, mod3.shape[2]), lambda i: (i // per_seq, 0, 0)),
            pl.BlockSpec((1, 1, modf3.shape[2]), lambda i: (i // per_seq, 0, 0)),
            pl.BlockSpec((1, d), lambda i: (0, 0)),
        ],
        out_specs=pl.BlockSpec((tm, d), lambda i: (i, 0)),
        out_shape=jax.ShapeDtypeStruct((t, d), F32),
        compiler_params=_params(32 * 1024 * 1024, ("arbitrary",)),
        name="combine",
    )(x1, y2, y2, gcol, mod3, modf3, gf)


def kernel(x, c, w_ada, b_ada, norm1_g, w_in, w_out, gmlp_w_s, gmlp_b_s, gmlp_v_gain, conv_w, conv_b,
           norm2_g, w_router_group, b_router_group, w_router_expert, b_router_expert, w_gate_up, w_down,
           w_ada_final, b_ada_final, norm_f_g):
    bsz, seq, d = x.shape
    depth = w_ada.shape[0]
    n_e = w_router_expert.shape[2]
    t = bsz * seq
    assert seq % TOKEN_TILE == 0 and TOKEN_TILE % CHUNK == 0
    assert w_router_group.shape[2] == N_GROUPS and n_e + N_GROUPS <= ROUTER_ROWS
    n_tiles = -(-(t * TOP_K + n_e * (EXPERT_TILE - 1)) // EXPERT_TILE // 2) * 2

    modf3 = _ada(c, w_ada_final, b_ada_final).reshape(bsz, 1, 2 * d)
    xt = x.reshape(t, d)
    for l in range(depth):
        mod3 = _ada(c, w_ada[l], b_ada[l]).reshape(bsz, 1, -1)
        wr = jnp.concatenate([w_router_expert[l], w_router_group[l]], axis=1)
        wrt = jnp.pad(wr, ((0, 0), (0, ROUTER_ROWS - wr.shape[1]))).T.astype(BF16)
        br = jnp.concatenate([b_router_expert[l], b_router_group[l]])
        br = jnp.pad(br, (0, ROUTER_ROWS - br.shape[0])).reshape(ROUTER_ROWS, 1)
        x1, h2, eid, gcol = _mix(
            xt, mod3, norm1_g[l].reshape(1, d), norm2_g[l].reshape(1, d),
            w_in[l].astype(BF16), w_out[l].astype(BF16), gmlp_w_s[l], gmlp_b_s[l].T,
            gmlp_v_gain[l].reshape(1, -1), conv_w[l], conv_b[l].reshape(1, -1), wrt, br, seq)
        texp, tab = _dispatch(eid, n_e, n_tiles, t)
        y2 = _experts(tab, texp.reshape(-1), h2, w_gate_up[l], w_down[l], n_tiles)
        assert depth == 1
        xt = _combine(x1, y2, gcol, mod3, modf3, norm_f_g.reshape(1, d), seq)
    return xt.reshape(bsz, seq, d)
```

```python
import functools

import jax
import jax.numpy as jnp
from jax import lax
from jax.experimental import pallas as pl
from jax.experimental.pallas import tpu as pltpu

F32 = jnp.float32
BF16 = jnp.bfloat16
I32 = jnp.int32
U32 = jnp.uint32

A_HEADS = 8
CHUNK = 128
N_GROUPS = 4
TOP_K = 2
EPS = 1e-6

V7X_LANES = 128
V7X_SUBLANES = 8
V7X_VMEM_BYTES = 64 * 1024 * 1024

TOKEN_TILE = 256
EXPERT_TILE = 256
ROUTER_ROWS = 40


def _rms(x, g):
    y = x * lax.rsqrt(jnp.mean(x * x, axis=-1, keepdims=True) + EPS)
    return y * g


def _pack_bf16_pairs(x):
    n = x.shape[1] // 2
    lo = lax.bitcast_convert_type(x[:, :n].astype(BF16).astype(F32), U32)
    hi = lax.bitcast_convert_type(x[:, n:].astype(BF16).astype(F32), U32)
    return hi | (lo >> 16)


def _unpack_bf16_pairs(w):
    lo = lax.bitcast_convert_type(w << 16, F32)
    hi = lax.bitcast_convert_type(w & jnp.uint32(0xFFFF0000), F32)
    return jnp.concatenate([lo, hi], axis=1)


def _params(vmem_bytes, semantics=None):
    kw = dict(vmem_limit_bytes=int(vmem_bytes))
    if semantics is not None:
        kw["dimension_semantics"] = semantics
    return pltpu.CompilerParams(**kw)


def _ada_body(c_ref, w_ref, b_ref, o_ref):
    ca = jax.nn.silu(c_ref[...]).astype(BF16)
    o_ref[...] = jnp.dot(ca, w_ref[...].astype(BF16), preferred_element_type=F32) + b_ref[...]


def _ada(c, w, b, tn=1024):
    bsz, d = c.shape
    n = w.shape[1]
    return pl.pallas_call(
        _ada_body,
        grid=(n // tn,),
        in_specs=[
            pl.BlockSpec((bsz, d), lambda j: (0, 0)),
            pl.BlockSpec((d, tn), lambda j: (0, j)),
            pl.BlockSpec((1, tn), lambda j: (0, j)),
        ],
        out_specs=pl.BlockSpec((bsz, tn), lambda j: (0, j)),
        out_shape=jax.ShapeDtypeStruct((bsz, n), F32),
        compiler_params=_params(4 * d * tn * 4, ("arbitrary",)),
        name="ada",
    )(c, w, b.reshape(1, n))


def _route(lt, tm):
    n_e = 32
    epg = n_e // N_GROUPS
    row = lax.broadcasted_iota(I32, (V7X_SUBLANES, tm), 0).astype(F32)
    gl = lt[n_e:n_e + V7X_SUBLANES]
    gvalid = row < float(N_GROUPS)
    glm = jnp.where(gvalid, gl, -jnp.inf)
    gmax = jnp.max(glm, axis=0, keepdims=True)
    garg = jnp.min(jnp.where(glm == gmax, row, float(V7X_SUBLANES)), axis=0, keepdims=True)
    gsum = jnp.sum(jnp.where(gvalid, jnp.exp(gl - gmax), 0.0), axis=0, keepdims=True)
    p_grp = 1.0 / gsum
    es = lt[0:epg]
    for g in range(1, N_GROUPS):
        es = jnp.where(garg == float(g), lt[g * epg:(g + 1) * epg], es)
    m1 = jnp.max(es, axis=0, keepdims=True)
    i1 = jnp.min(jnp.where(es == m1, row, float(epg)), axis=0, keepdims=True)
    es2 = jnp.where(row == i1, -jnp.inf, es)
    m2 = jnp.max(es2, axis=0, keepdims=True)
    i2 = jnp.min(jnp.where(es2 == m2, row, float(epg)), axis=0, keepdims=True)
    z = jnp.exp(m2 - m1)
    den = 1.0 + z
    g0 = p_grp / den
    g1 = p_grp * z / den
    e0 = garg * float(epg) + i1
    e1 = garg * float(epg) + i2
    zero = jnp.zeros_like(g0)
    return jnp.concatenate([g0, g1, e0, e1, zero, zero, zero, zero], axis=0)


def _mix_body(x_ref, mod_ref, g1_ref, g2_ref, win_ref, wout_ref, ws_ref, bst_ref, vg_ref,
              cw_ref, cb_ref, wrt_ref, br_ref,
              x1_ref, h2_ref, eid_ref, gcol_ref, pre_scr, *, tiles_per_seq):
    tm, d = x_ref.shape
    aw = vg_ref.shape[1]
    bw = cb_ref.shape[1]
    hd = aw // A_HEADS
    nch = tm // CHUNK
    i = pl.program_id(0)

    x = x_ref[...]
    mod = mod_ref[0]
    shift1, scale1, gate1 = mod[:, 0:d], mod[:, d:2 * d], mod[:, 2 * d:3 * d]
    shift2, scale2 = mod[:, 3 * d:4 * d], mod[:, 4 * d:5 * d]

    h = _rms(x, g1_ref[...]) * (1.0 + scale1) + shift1
    hb = h.astype(BF16)

    uv = jax.nn.gelu(jnp.dot(hb, win_ref[:, 0:2 * aw], preferred_element_type=F32))
    u, v = uv[:, :aw], uv[:, aw:]
    tq_r = lax.broadcasted_iota(I32, (CHUNK, CHUNK), 0)
    tq_c = lax.broadcasted_iota(I32, (CHUNK, CHUNK), 1)
    causal = tq_c <= tq_r
    ya = []
    for hh in range(A_HEADS):
        sl = slice(hh * hd, (hh + 1) * hd)
        vh = _rms(v[:, sl], vg_ref[:, sl]).astype(BF16)
        rhs = jnp.concatenate([vh[c * CHUNK:(c + 1) * CHUNK] for c in range(nch)], axis=1)
        w = jnp.where(causal, ws_ref[hh], 0.0).astype(BF16)
        zs = jnp.dot(w, rhs, preferred_element_type=F32) + bst_ref[:, hh:hh + 1]
        zs = jnp.concatenate([zs[:, c * hd:(c + 1) * hd] for c in range(nch)], axis=0)
        ya.append(u[:, sl] * zs)

    bcx = jnp.dot(hb, win_ref[:, 2 * aw:], preferred_element_type=F32)
    bg, cg, xin = bcx[:, :bw], bcx[:, bw:2 * bw], bcx[:, 2 * bw:]
    pre = cg * xin

    @pl.when(i % tiles_per_seq == 0)
    def _():
        pre_scr[0:V7X_SUBLANES, :] = jnp.zeros((V7X_SUBLANES, bw), F32)

    pre_scr[V7X_SUBLANES:V7X_SUBLANES + tm, :] = pre
    p1 = pre_scr[V7X_SUBLANES - 1:V7X_SUBLANES - 1 + tm, :]
    p2 = pre_scr[V7X_SUBLANES - 2:V7X_SUBLANES - 2 + tm, :]
    conv = cw_ref[0:1, :] * p2 + cw_ref[1:2, :] * p1 + cw_ref[2:3, :] * pre + cb_ref[...]
    yb = bg * conv
    pre_scr[0:V7X_SUBLANES, :] = pre_scr[tm:tm + V7X_SUBLANES, :]

    y = jnp.concatenate(ya + [yb], axis=1).astype(BF16)
    mix = jnp.dot(y, wout_ref[...], preferred_element_type=F32)
    x1 = x + gate1 * mix
    x1_ref[...] = x1

    h2 = _rms(x1, g2_ref[...]) * (1.0 + scale2) + shift2
    h2_ref[...] = _pack_bf16_pairs(h2)

    lt = lax.dot_general(wrt_ref[...], h2.astype(BF16), (((1,), (1,)), ((), ())),
                         preferred_element_type=F32) + br_ref[...]
    slab = _route(lt, tm)
    eid_ref[0] = slab[2:4].astype(I32)
    wide = jnp.concatenate([slab, jnp.zeros((V7X_LANES - V7X_SUBLANES, tm), F32)], axis=0)
    gcol_ref[...] = wide.T


def _mix(xt, mod3, g1, g2, win, wout, ws, bst, vg, cw, cb, wrt, br, seq):
    t, d = xt.shape
    tm = TOKEN_TILE
    nt = t // tm
    in_cols = win.shape[1]
    bw = cb.shape[1]
    const2 = lambda i: (0, 0)
    single = dict(pipeline_mode=pl.Buffered(1))
    body = functools.partial(_mix_body, tiles_per_seq=seq // tm)
    vmem = (2 * d * (in_cols + d)
            + 3 * 2 * tm * d * 4
            + (tm + V7X_SUBLANES) * bw * 4
            + tm * (in_cols + 4 * d) * 4
            + 8 * 1024 * 1024)
    return pl.pallas_call(
        body,
        grid=(nt,),
        in_specs=[
            pl.BlockSpec((tm, d), lambda i: (i, 0)),
            pl.BlockSpec((1, 1, mod3.shape[2]), lambda i: (i // (seq // tm), 0, 0)),
            pl.BlockSpec((1, d), const2),
            pl.BlockSpec((1, d), const2),
            pl.BlockSpec(win.shape, const2, **single),
            pl.BlockSpec(wout.shape, const2, **single),
            pl.BlockSpec(ws.shape, lambda i: (0, 0, 0)),
            pl.BlockSpec(bst.shape, const2),
            pl.BlockSpec(vg.shape, const2),
            pl.BlockSpec(cw.shape, const2),
            pl.BlockSpec(cb.shape, const2),
            pl.BlockSpec(wrt.shape, const2),
            pl.BlockSpec(br.shape, const2),
        ],
        out_specs=[
            pl.BlockSpec((tm, d), lambda i: (i, 0)),
            pl.BlockSpec((tm, d // 2), lambda i: (i, 0)),
            pl.BlockSpec((1, TOP_K, tm), lambda i: (i, 0, 0)),
            pl.BlockSpec((tm, V7X_LANES), lambda i: (i, 0)),
        ],
        out_shape=[
            jax.ShapeDtypeStruct((t, d), F32),
            jax.ShapeDtypeStruct((t, d // 2), U32),
            jax.ShapeDtypeStruct((nt, TOP_K, tm), I32),
            jax.ShapeDtypeStruct((t, V7X_LANES), F32),
        ],
        scratch_shapes=[pltpu.VMEM((tm + V7X_SUBLANES, bw), F32)],
        compiler_params=_params(min(vmem, V7X_VMEM_BYTES - 6 * 1024 * 1024), ("arbitrary",)),
        name="mix",
    )(xt, mod3, g1, g2, win, wout, ws, bst, vg, cw, cb, wrt, br)


def _dispatch_body(eid_ref, texp_ref, tab_ref, rank_scr, dest_v, dest_s, init_v, sem,
                   *, n_e, n_tiles, n_tok):
    nch, _, c = eid_ref.shape
    te = EXPERT_TILE
    r = lax.broadcasted_iota(I32, (c, c), 0)
    q = lax.broadcasted_iota(I32, (c, c), 1)
    before = (r < q).astype(BF16)
    e_iota = lax.broadcasted_iota(I32, (n_e, c), 0)

    init_v[...] = (lax.broadcasted_iota(I32, init_v.shape, 0) & (te - 1)) + TOP_K * n_tok
    cp_i = pltpu.make_async_copy(init_v, tab_ref, sem.at[0])
    cp_i.start()

    def count_body(ch, carry):
        e2 = eid_ref[ch]
        ranks = []
        for k in range(TOP_K):
            oh = e_iota == e2[k:k + 1]
            ohf = oh.astype(F32)
            pref = jnp.dot(ohf.astype(BF16), before, preferred_element_type=F32)
            ranks.append(jnp.sum(jnp.where(oh, pref + carry, 0.0), axis=0, keepdims=True))
            carry = carry + jnp.sum(ohf, axis=1, keepdims=True)
        rank_scr[ch] = jnp.concatenate(ranks, axis=0)
        return carry

    counts = lax.fori_loop(0, nch, count_body, jnp.zeros((n_e, 1), F32))

    padded = jnp.floor((counts + float(te - 1)) / float(te)) * float(te)
    sub = lax.broadcasted_iota(I32, (n_e, V7X_LANES), 0)
    lane = lax.broadcasted_iota(I32, (n_e, V7X_LANES), 1)
    pstart_row = jnp.sum(jnp.where(sub < lane, padded, 0.0), axis=0, keepdims=True)
    pstart = jnp.sum(jnp.where(sub == lane, pstart_row, 0.0), axis=1, keepdims=True)
    pend = pstart + padded
    total = jnp.max(pend, axis=0, keepdims=True)
    last_e = jnp.max(jnp.where(counts > 0.0, sub[:, 0:1].astype(F32), -1.0), axis=0, keepdims=True)

    group = V7X_SUBLANES // TOP_K

    def dest_body(g, carry):
        rows = []
        for j in range(group):
            e2 = eid_ref[g * group + j]
            rk = rank_scr[g * group + j]
            for k in range(TOP_K):
                oh = e_iota == e2[k:k + 1]
                off = jnp.sum(jnp.where(oh, pstart, 0.0), axis=0, keepdims=True)
                rows.append(off + rk[k:k + 1])
        row0 = pl.multiple_of(g * V7X_SUBLANES, V7X_SUBLANES)
        dest_v[pl.ds(row0, V7X_SUBLANES), :] = jnp.concatenate(rows, axis=0).astype(I32)
        return carry

    lax.fori_loop(0, nch // group, dest_body, 0)
    cp_d = pltpu.make_async_copy(dest_v, dest_s, sem.at[1])
    cp_d.start()

    w = texp_ref.shape[1]
    tj = lax.broadcasted_iota(I32, (n_e, w), 1).astype(F32) * float(te)
    texp = jnp.sum((pend <= tj).astype(F32), axis=0, keepdims=True)
    tj1 = tj[0:1]
    texp = jnp.where(tj1 < total, jnp.minimum(texp, float(n_e - 1)), last_e)
    lane_w = lax.broadcasted_iota(I32, (1, w), 1)
    texp = jnp.where(lane_w == n_tiles, total / float(te), texp)
    texp_ref[...] = texp.astype(I32)

    cp_i.wait()
    cp_d.wait()

    def row_body(rw, carry):
        a0 = (rw % TOP_K) * n_tok + (rw // TOP_K) * c
        for col in range(c):
            tab_ref[dest_s[rw, col]] = a0 + col
        return carry

    lax.fori_loop(0, nch * TOP_K, row_body, 0)


def _dispatch(eid, n_e, n_tiles, n_tok):
    nch, k, c = eid.shape
    assert c == EXPERT_TILE and EXPERT_TILE & (EXPERT_TILE - 1) == 0
    body = functools.partial(_dispatch_body, n_e=n_e, n_tiles=n_tiles, n_tok=n_tok)
    w = 2 * V7X_LANES
    assert n_tiles < w
    rows = n_tiles + V7X_SUBLANES
    smem = pl.BlockSpec(memory_space=pltpu.SMEM)
    return pl.pallas_call(
        body,
        out_specs=[pl.BlockSpec(memory_space=pltpu.VMEM), smem],
        out_shape=[
            jax.ShapeDtypeStruct((1, w), I32),
            jax.ShapeDtypeStruct((rows * EXPERT_TILE,), I32),
        ],
        scratch_shapes=[
            pltpu.VMEM((nch, k, c), F32),
            pltpu.VMEM((nch * k, c), I32),
            pltpu.SMEM((nch * k, c), I32),
            pltpu.VMEM((rows * EXPERT_TILE,), I32),
            pltpu.SemaphoreType.DMA((2,)),
        ],
        compiler_params=_params(16 * 1024 * 1024),
        name="dispatch",
    )(eid)


def _expert_body(tab_ref, texp_ref, h2_hbm, wgu_hbm, wd_hbm, y2_hbm,
                 xa, xb, ya, yb, wgu_f, wd_f, wgu_s, wd_s, gsem, ssem, wsem, *, n_tiles):
    te = EXPERT_TILE
    de = wd_s.shape[0]
    n_tok = h2_hbm.shape[0]
    p = pl.program_id(0)
    n_used = texp_ref[n_tiles]
    ta = 2 * p
    tb = ta + 1

    def gather(tile, buf, sem):
        for r in range(te):
            tok = tab_ref[tile * te + r] & (n_tok - 1)
            pltpu.make_async_copy(h2_hbm.at[pl.ds(tok, 1), :], buf.at[pl.ds(r, 1), :], sem).start()

    def scatter(tile, buf, sem):
        for r in range(te):
            row = tab_ref[tile * te + r]
            pltpu.make_async_copy(buf.at[pl.ds(r, 1), :], y2_hbm.at[pl.ds(row, 1), :], sem).start()

    def wait_rows(buf, sem):
        pltpu.make_async_copy(h2_hbm.at[pl.ds(0, te), :], buf, sem).wait()

    def weight_copies(e):
        return (pltpu.make_async_copy(wgu_hbm.at[e], wgu_f, wsem.at[0]),
                pltpu.make_async_copy(wd_hbm.at[e], wd_f, wsem.at[1]))

    def switch_weights(tile):
        e = texp_ref[tile]
        first = jnp.logical_or(tile == 0, texp_ref[jnp.maximum(tile - 1, 0)] != e)

        @pl.when(jnp.logical_and(first, tile < n_used))
        def _():
            @pl.when(tile == 0)
            def _():
                for cp in weight_copies(e):
                    cp.start()

            for cp in weight_copies(e):
                cp.wait()
            wgu_s[...] = wgu_f[...].astype(BF16)
            wd_s[...] = wd_f[...].astype(BF16)
            nxt = lax.while_loop(
                lambda k: jnp.logical_and(k < n_used, texp_ref[jnp.minimum(k, n_tiles - 1)] == e),
                lambda k: k + 1, tile + 1)

            @pl.when(nxt < n_used)
            def _():
                for cp in weight_copies(texp_ref[jnp.minimum(nxt, n_tiles - 1)]):
                    cp.start()

    def compute(xbuf, ybuf):
        xb16 = _unpack_bf16_pairs(xbuf[...]).astype(BF16)
        gu = jnp.dot(xb16, wgu_s[...], preferred_element_type=F32)
        a = (jax.nn.silu(gu[:, :de]) * gu[:, de:]).astype(BF16)
        ybuf[...] = _pack_bf16_pairs(jnp.dot(a, wd_s[...], preferred_element_type=F32))

    @pl.when(ta < n_used)
    def _():
        @pl.when(p == 0)
        def _():
            gather(0, xa, gsem.at[0])
            yb[...] = jnp.zeros(yb.shape, U32)

        switch_weights(ta)
        wait_rows(xa, gsem.at[0])

        @pl.when(p > 0)
        def _():
            wait_rows(ya, ssem.at[0])

        gather(tb, xb, gsem.at[1])
        scatter(jnp.where(p == 0, n_tiles, tb - 2), yb, ssem.at[1])
        compute(xa, ya)

        switch_weights(tb)
        wait_rows(xb, gsem.at[1])
        wait_rows(yb, ssem.at[1])

        scatter(ta, ya, ssem.at[0])
        gather(ta + 2, xa, gsem.at[0])
        compute(xb, yb)

        @pl.when(ta + 2 >= n_used)
        def _():
            wait_rows(ya, ssem.at[0])
            scatter(tb, yb, ssem.at[1])
            wait_rows(xa, gsem.at[0])
            wait_rows(yb, ssem.at[1])


def _experts(tab, texp, h2p, wgu, wd, n_tiles):
    t, dp = h2p.shape
    te = EXPERT_TILE
    _, d, n_gu = wgu.shape
    de = wd.shape[1]
    assert n_tiles % 2 == 0 and t & (t - 1) == 0
    body = functools.partial(_expert_body, n_tiles=n_tiles)
    vmem = ((d * n_gu + de * d) * (4 + 2) + 4 * te * dp * 4 + te * (n_gu + 3 * d) * 4 + 4 * 1024 * 1024)
    hbm = pl.BlockSpec(memory_space=pl.ANY)
    return pl.pallas_call(
        body,
        grid_spec=pltpu.PrefetchScalarGridSpec(
            num_scalar_prefetch=2,
            grid=(n_tiles // 2,),
            in_specs=[hbm, hbm, hbm],
            out_specs=hbm,
            scratch_shapes=[
                pltpu.VMEM((te, dp), U32), pltpu.VMEM((te, dp), U32),
                pltpu.VMEM((te, dp), U32), pltpu.VMEM((te, dp), U32),
                pltpu.VMEM((d, n_gu), F32), pltpu.VMEM((de, d), F32),
                pltpu.VMEM((d, n_gu), BF16), pltpu.VMEM((de, d), BF16),
                pltpu.SemaphoreType.DMA((2,)), pltpu.SemaphoreType.DMA((2,)), pltpu.SemaphoreType.DMA((2,)),
            ],
        ),
        out_shape=jax.ShapeDtypeStruct((TOP_K * t + te, dp), U32),
        compiler_params=_params(vmem, ("arbitrary",)),
        name="experts",
    )(tab, texp, h2p, wgu, wd)


def _combine_body(x1_ref, y0_ref, y1_ref, gcol_ref, mod_ref, modf_ref, gf_ref, o_ref):
    d = x1_ref.shape[1]
    gate2 = mod_ref[0][:, 5 * d:6 * d]
    modf = modf_ref[0]
    shift_f, scale_f = modf[:, 0:d], modf[:, d:2 * d]
    y0 = _unpack_bf16_pairs(y0_ref[...])
    y1 = _unpack_bf16_pairs(y1_ref[...])
    ffn = y0 * gcol_ref[:, 0:1] + y1 * gcol_ref[:, 1:2]
    x2 = x1_ref[...] + gate2 * ffn
    o_ref[...] = _rms(x2, gf_ref[...]) * (1.0 + scale_f) + shift_f


def _combine(x1, y2, gcol, mod3, modf3, gf, seq):
    t, d = x1.shape
    tm = TOKEN_TILE
    per_seq = seq // tm
    return pl.pallas_call(
        _combine_body,
        grid=(t // tm,),
        in_specs=[
            pl.BlockSpec((tm, d), lambda i: (i, 0)),
            pl.BlockSpec((tm, d // 2), lambda i: (i, 0)),
            pl.BlockSpec((tm, d // 2), lambda i: (t // tm + i, 0)),
            pl.BlockSpec((tm, V7X_LANES), lambda i: (i, 0)),
            pl.BlockSpec((1, 1, mod3.shape[2]), lambda i: (i // per_seq, 0, 0)),
            pl.BlockSpec((1, 1, modf3.shape[2]), lambda i: (i // per_seq, 0, 0)),
            pl.BlockSpec((1, d), lambda i: (0, 0)),
        ],
        out_specs=pl.BlockSpec((tm, d), lambda i: (i, 0)),
        out_shape=jax.ShapeDtypeStruct((t, d), F32),
        compiler_params=_params(32 * 1024 * 1024, ("arbitrary",)),
        name="combine",
    )(x1, y2, y2, gcol, mod3, modf3, gf)


def kernel(x, c, w_ada, b_ada, norm1_g, w_in, w_out, gmlp_w_s, gmlp_b_s, gmlp_v_gain, conv_w, conv_b,
           norm2_g, w_router_group, b_router_group, w_router_expert, b_router_expert, w_gate_up, w_down,
           w_ada_final, b_ada_final, norm_f_g):
    bsz, seq, d = x.shape
    depth = w_ada.shape[0]
    n_e = w_router_expert.shape[2]
    t = bsz * seq
    assert seq % TOKEN_TILE == 0 and TOKEN_TILE % CHUNK == 0
    assert w_router_group.shape[2] == N_GROUPS and n_e + N_GROUPS <= ROUTER_ROWS
    n_tiles = -(-(t * TOP_K + n_e * (EXPERT_TILE - 1)) // EXPERT_TILE // 2) * 2

    modf3 = _ada(c, w_ada_final, b_ada_final).reshape(bsz, 1, 2 * d)
    xt = x.reshape(t, d)
    for l in range(depth):
        mod3 = _ada(c, w_ada[l], b_ada[l]).reshape(bsz, 1, -1)
        wr = jnp.concatenate([w_router_expert[l], w_router_group[l]], axis=1)
        wrt = jnp.pad(wr, ((0, 0), (0, ROUTER_ROWS - wr.shape[1]))).T.astype(BF16)
        br = jnp.concatenate([b_router_expert[l], b_router_group[l]])
        br = jnp.pad(br, (0, ROUTER_ROWS - br.shape[0])).reshape(ROUTER_ROWS, 1)
        x1, h2, eid, gcol = _mix(
            xt, mod3, norm1_g[l].reshape(1, d), norm2_g[l].reshape(1, d),
            w_in[l].astype(BF16), w_out[l].astype(BF16), gmlp_w_s[l], gmlp_b_s[l].T,
            gmlp_v_gain[l].reshape(1, -1), conv_w[l], conv_b[l].reshape(1, -1), wrt, br, seq)
        texp, tab = _dispatch(eid, n_e, n_tiles, t)
        y2 = _experts(tab, texp.reshape(-1), h2, w_gate_up[l], w_down[l], n_tiles)
        assert depth == 1
        xt = _combine(x1, y2, gcol, mod3, modf3, norm_f_g.reshape(1, d), seq)
    return xt.reshape(bsz, seq, d)
```

```python
import functools

import jax
import jax.numpy as jnp
from jax import lax
from jax.experimental import pallas as pl
from jax.experimental.pallas import tpu as pltpu

F32 = jnp.float32
BF16 = jnp.bfloat16
I32 = jnp.int32
U32 = jnp.uint32

A_HEADS = 8
CHUNK = 128
N_GROUPS = 4
TOP_K = 2
EPS = 1e-6

V7X_LANES = 128
V7X_SUBLANES = 8
V7X_VMEM_BYTES = 64 * 1024 * 1024

TOKEN_TILE = 256
EXPERT_TILE = 256
ROUTER_ROWS = 40
EXPERT_RING = 3
DUMP_TILES = 2


def _rms(x, g):
    y = x * lax.rsqrt(jnp.mean(x * x, axis=-1, keepdims=True) + EPS)
    return y * g


def _pack_bf16_pairs(x):
    n = x.shape[1] // 2
    lo = lax.bitcast_convert_type(x[:, :n].astype(BF16).astype(F32), U32)
    hi = lax.bitcast_convert_type(x[:, n:].astype(BF16).astype(F32), U32)
    return hi | (lo >> 16)


def _unpack_bf16_pairs(w):
    lo = lax.bitcast_convert_type(w << 16, F32)
    hi = lax.bitcast_convert_type(w & jnp.uint32(0xFFFF0000), F32)
    return jnp.concatenate([lo, hi], axis=1)


def _params(vmem_bytes, semantics=None):
    kw = dict(vmem_limit_bytes=int(vmem_bytes))
    if semantics is not None:
        kw["dimension_semantics"] = semantics
    return pltpu.CompilerParams(**kw)


def _ada_body(c_ref, w_ref, b_ref, o_ref):
    ca = jax.nn.silu(c_ref[...]).astype(BF16)
    o_ref[...] = jnp.dot(ca, w_ref[...].astype(BF16), preferred_element_type=F32) + b_ref[...]


def _ada(c, w, b, tn=1024):
    bsz, d = c.shape
    n = w.shape[1]
    return pl.pallas_call(
        _ada_body,
        grid=(n // tn,),
        in_specs=[
            pl.BlockSpec((bsz, d), lambda j: (0, 0)),
            pl.BlockSpec((d, tn), lambda j: (0, j)),
            pl.BlockSpec((1, tn), lambda j: (0, j)),
        ],
        out_specs=pl.BlockSpec((bsz, tn), lambda j: (0, j)),
        out_shape=jax.ShapeDtypeStruct((bsz, n), F32),
        compiler_params=_params(4 * d * tn * 4, ("arbitrary",)),
        name="ada",
    )(c, w, b.reshape(1, n))


def _route(lt, tm):
    n_e = 32
    epg = n_e // N_GROUPS
    row = lax.broadcasted_iota(I32, (V7X_SUBLANES, tm), 0).astype(F32)
    gl = lt[n_e:n_e + V7X_SUBLANES]
    gvalid = row < float(N_GROUPS)
    glm = jnp.where(gvalid, gl, -jnp.inf)
    gmax = jnp.max(glm, axis=0, keepdims=True)
    garg = jnp.min(jnp.where(glm == gmax, row, float(V7X_SUBLANES)), axis=0, keepdims=True)
    gsum = jnp.sum(jnp.where(gvalid, jnp.exp(gl - gmax), 0.0), axis=0, keepdims=True)
    p_grp = 1.0 / gsum
    es = lt[0:epg]
    for g in range(1, N_GROUPS):
        es = jnp.where(garg == float(g), lt[g * epg:(g + 1) * epg], es)
    m1 = jnp.max(es, axis=0, keepdims=True)
    i1 = jnp.min(jnp.where(es == m1, row, float(epg)), axis=0, keepdims=True)
    es2 = jnp.where(row == i1, -jnp.inf, es)
    m2 = jnp.max(es2, axis=0, keepdims=True)
    i2 = jnp.min(jnp.where(es2 == m2, row, float(epg)), axis=0, keepdims=True)
    z = jnp.exp(m2 - m1)
    den = 1.0 + z
    g0 = p_grp / den
    g1 = p_grp * z / den
    e0 = garg * float(epg) + i1
    e1 = garg * float(epg) + i2
    zero = jnp.zeros_like(g0)
    return jnp.concatenate([g0, g1, e0, e1, zero, zero, zero, zero], axis=0)


def _mix_body(x_ref, mod_ref, g1_ref, g2_ref, win_ref, wout_ref, ws_ref, bst_ref, vg_ref,
              cw_ref, cb_ref, wrt_ref, br_ref,
              x1_ref, h2_ref, eid_ref, gcol_ref, pre_scr, *, tiles_per_seq):
    tm, d = x_ref.shape
    aw = vg_ref.shape[1]
    bw = cb_ref.shape[1]
    hd = aw // A_HEADS
    nch = tm // CHUNK
    i = pl.program_id(0)

    x = x_ref[...]
    mod = mod_ref[0]
    shift1, scale1, gate1 = mod[:, 0:d], mod[:, d:2 * d], mod[:, 2 * d:3 * d]
    shift2, scale2 = mod[:, 3 * d:4 * d], mod[:, 4 * d:5 * d]

    h = _rms(x, g1_ref[...]) * (1.0 + scale1) + shift1
    hb = h.astype(BF16)

    uv = jax.nn.gelu(jnp.dot(hb, win_ref[:, 0:2 * aw], preferred_element_type=F32))
    u, v = uv[:, :aw], uv[:, aw:]
    tq_r = lax.broadcasted_iota(I32, (CHUNK, CHUNK), 0)
    tq_c = lax.broadcasted_iota(I32, (CHUNK, CHUNK), 1)
    causal = tq_c <= tq_r
    ya = []
    for hh in range(A_HEADS):
        sl = slice(hh * hd, (hh + 1) * hd)
        vh = _rms(v[:, sl], vg_ref[:, sl]).astype(BF16)
        rhs = jnp.concatenate([vh[c * CHUNK:(c + 1) * CHUNK] for c in range(nch)], axis=1)
        w = jnp.where(causal, ws_ref[hh], 0.0).astype(BF16)
        zs = jnp.dot(w, rhs, preferred_element_type=F32) + bst_ref[:, hh:hh + 1]
        zs = jnp.concatenate([zs[:, c * hd:(c + 1) * hd] for c in range(nch)], axis=0)
        ya.append(u[:, sl] * zs)

    bcx = jnp.dot(hb, win_ref[:, 2 * aw:], preferred_element_type=F32)
    bg, cg, xin = bcx[:, :bw], bcx[:, bw:2 * bw], bcx[:, 2 * bw:]
    pre = cg * xin

    @pl.when(i % tiles_per_seq == 0)
    def _():
        pre_scr[0:V7X_SUBLANES, :] = jnp.zeros((V7X_SUBLANES, bw), F32)

    pre_scr[V7X_SUBLANES:V7X_SUBLANES + tm, :] = pre
    p1 = pre_scr[V7X_SUBLANES - 1:V7X_SUBLANES - 1 + tm, :]
    p2 = pre_scr[V7X_SUBLANES - 2:V7X_SUBLANES - 2 + tm, :]
    conv = cw_ref[0:1, :] * p2 + cw_ref[1:2, :] * p1 + cw_ref[2:3, :] * pre + cb_ref[...]
    yb = bg * conv
    pre_scr[0:V7X_SUBLANES, :] = pre_scr[tm:tm + V7X_SUBLANES, :]

    y = jnp.concatenate(ya + [yb], axis=1).astype(BF16)
    mix = jnp.dot(y, wout_ref[...], preferred_element_type=F32)
    x1 = x + gate1 * mix
    x1_ref[...] = x1

    h2 = _rms(x1, g2_ref[...]) * (1.0 + scale2) + shift2
    h2_ref[...] = _pack_bf16_pairs(h2)

    lt = lax.dot_general(wrt_ref[...], h2.astype(BF16), (((1,), (1,)), ((), ())),
                         preferred_element_type=F32) + br_ref[...]
    slab = _route(lt, tm)
    eid_ref[0] = slab[2:4].astype(I32)
    wide = jnp.concatenate([slab, jnp.zeros((V7X_LANES - V7X_SUBLANES, tm), F32)], axis=0)
    gcol_ref[...] = wide.T


def _mix(xt, mod3, g1, g2, win, wout, ws, bst, vg, cw, cb, wrt, br, seq):
    t, d = xt.shape
    tm = TOKEN_TILE
    nt = t // tm
    in_cols = win.shape[1]
    bw = cb.shape[1]
    const2 = lambda i: (0, 0)
    single = dict(pipeline_mode=pl.Buffered(1))
    body = functools.partial(_mix_body, tiles_per_seq=seq // tm)
    vmem = (2 * d * (in_cols + d)
            + 3 * 2 * tm * d * 4
            + (tm + V7X_SUBLANES) * bw * 4
            + tm * (in_cols + 4 * d) * 4
            + 8 * 1024 * 1024)
    return pl.pallas_call(
        body,
        grid=(nt,),
        in_specs=[
            pl.BlockSpec((tm, d), lambda i: (i, 0)),
            pl.BlockSpec((1, 1, mod3.shape[2]), lambda i: (i // (seq // tm), 0, 0)),
            pl.BlockSpec((1, d), const2),
            pl.BlockSpec((1, d), const2),
            pl.BlockSpec(win.shape, const2, **single),
            pl.BlockSpec(wout.shape, const2, **single),
            pl.BlockSpec(ws.shape, lambda i: (0, 0, 0)),
            pl.BlockSpec(bst.shape, const2),
            pl.BlockSpec(vg.shape, const2),
            pl.BlockSpec(cw.shape, const2),
            pl.BlockSpec(cb.shape, const2),
            pl.BlockSpec(wrt.shape, const2),
            pl.BlockSpec(br.shape, const2),
        ],
        out_specs=[
            pl.BlockSpec((tm, d), lambda i: (i, 0)),
            pl.BlockSpec((tm, d // 2), lambda i: (i, 0)),
            pl.BlockSpec((1, TOP_K, tm), lambda i: (i, 0, 0)),
            pl.BlockSpec((tm, V7X_LANES), lambda i: (i, 0)),
        ],
        out_shape=[
            jax.ShapeDtypeStruct((t, d), F32),
            jax.ShapeDtypeStruct((t, d // 2), U32),
            jax.ShapeDtypeStruct((nt, TOP_K, tm), I32),
            jax.ShapeDtypeStruct((t, V7X_LANES), F32),
        ],
        scratch_shapes=[pltpu.VMEM((tm + V7X_SUBLANES, bw), F32)],
        compiler_params=_params(min(vmem, V7X_VMEM_BYTES - 6 * 1024 * 1024), ("arbitrary",)),
        name="mix",
    )(xt, mod3, g1, g2, win, wout, ws, bst, vg, cw, cb, wrt, br)


def _dispatch_body(eid_ref, texp_ref, tab_ref, rank_scr, dest_v, dest_s, init_v, sem,
                   *, n_e, n_tiles, n_tok):
    nch, _, c = eid_ref.shape
    te = EXPERT_TILE
    r = lax.broadcasted_iota(I32, (c, c), 0)
    q = lax.broadcasted_iota(I32, (c, c), 1)
    before = (r < q).astype(BF16)
    e_iota = lax.broadcasted_iota(I32, (n_e, c), 0)

    init_v[...] = (lax.broadcasted_iota(I32, init_v.shape, 0) & (DUMP_TILES * te - 1)) + TOP_K * n_tok
    cp_i = pltpu.make_async_copy(init_v, tab_ref, sem.at[0])
    cp_i.start()

    def count_body(ch, carry):
        e2 = eid_ref[ch]
        ranks = []
        for k in range(TOP_K):
            oh = e_iota == e2[k:k + 1]
            ohf = oh.astype(F32)
            pref = jnp.dot(ohf.astype(BF16), before, preferred_element_type=F32)
            ranks.append(jnp.sum(jnp.where(oh, pref + carry, 0.0), axis=0, keepdims=True))
            carry = carry + jnp.sum(ohf, axis=1, keepdims=True)
        rank_scr[ch] = jnp.concatenate(ranks, axis=0)
        return carry

    counts = lax.fori_loop(0, nch, count_body, jnp.zeros((n_e, 1), F32))

    padded = jnp.floor((counts + float(te - 1)) / float(te)) * float(te)
    sub = lax.broadcasted_iota(I32, (n_e, V7X_LANES), 0)
    lane = lax.broadcasted_iota(I32, (n_e, V7X_LANES), 1)
    pstart_row = jnp.sum(jnp.where(sub < lane, padded, 0.0), axis=0, keepdims=True)
    pstart = jnp.sum(jnp.where(sub == lane, pstart_row, 0.0), axis=1, keepdims=True)
    pend = pstart + padded
    total = jnp.max(pend, axis=0, keepdims=True)
    last_e = jnp.max(jnp.where(counts > 0.0, sub[:, 0:1].astype(F32), -1.0), axis=0, keepdims=True)

    group = V7X_SUBLANES // TOP_K

    def dest_body(g, carry):
        rows = []
        for j in range(group):
            e2 = eid_ref[g * group + j]
            rk = rank_scr[g * group + j]
            for k in range(TOP_K):
                oh = e_iota == e2[k:k + 1]
                off = jnp.sum(jnp.where(oh, pstart, 0.0), axis=0, keepdims=True)
                rows.append(off + rk[k:k + 1])
        row0 = pl.multiple_of(g * V7X_SUBLANES, V7X_SUBLANES)
        dest_v[pl.ds(row0, V7X_SUBLANES), :] = jnp.concatenate(rows, axis=0).astype(I32)
        return carry

    lax.fori_loop(0, nch // group, dest_body, 0)
    cp_d = pltpu.make_async_copy(dest_v, dest_s, sem.at[1])
    cp_d.start()

    w = texp_ref.shape[1]
    tj = lax.broadcasted_iota(I32, (n_e, w), 1).astype(F32) * float(te)
    texp = jnp.sum((pend <= tj).astype(F32), axis=0, keepdims=True)
    tj1 = tj[0:1]
    texp = jnp.where(tj1 < total, jnp.minimum(texp, float(n_e - 1)), last_e)
    lane_w = lax.broadcasted_iota(I32, (1, w), 1)
    texp = jnp.where(lane_w == n_tiles, total / float(te), texp)
    texp_ref[...] = texp.astype(I32)

    cp_i.wait()
    cp_d.wait()

    def row_body(rw, carry):
        a0 = (rw % TOP_K) * n_tok + (rw // TOP_K) * c
        for col in range(c):
            tab_ref[dest_s[rw, col]] = a0 + col
        return carry

    lax.fori_loop(0, nch * TOP_K, row_body, 0)


def _dispatch(eid, n_e, n_tiles, n_tok):
    nch, k, c = eid.shape
    assert c == EXPERT_TILE and EXPERT_TILE & (EXPERT_TILE - 1) == 0
    body = functools.partial(_dispatch_body, n_e=n_e, n_tiles=n_tiles, n_tok=n_tok)
    w = 2 * V7X_LANES
    assert n_tiles < w
    rows = n_tiles + V7X_SUBLANES
    rows += (-rows) % 4
    smem = pl.BlockSpec(memory_space=pltpu.SMEM)
    return pl.pallas_call(
        body,
        out_specs=[pl.BlockSpec(memory_space=pltpu.VMEM), smem],
        out_shape=[
            jax.ShapeDtypeStruct((1, w), I32),
            jax.ShapeDtypeStruct((rows * EXPERT_TILE,), I32),
        ],
        scratch_shapes=[
            pltpu.VMEM((nch, k, c), F32),
            pltpu.VMEM((nch * k, c), I32),
            pltpu.SMEM((nch * k, c), I32),
            pltpu.VMEM((rows * EXPERT_TILE,), I32),
            pltpu.SemaphoreType.DMA((2,)),
        ],
        compiler_params=_params(16 * 1024 * 1024),
        name="dispatch",
    )(eid)


def _expert_body(tab_ref, texp_ref, h2_hbm, wgu_hbm, wd_hbm, y2_hbm,
                 x0, x1, x2, y0, y1, y2, wgu_f, wd_f, wgu_s, wd_s, gsem, ssem, wsem, zsem, *, n_tiles):
    te = EXPERT_TILE
    ring = EXPERT_RING
    de = wd_s.shape[0]
    n_tok = h2_hbm.shape[0]
    xs, ys = (x0, x1, x2), (y0, y1, y2)
    p = pl.program_id(0)
    n_used = texp_ref[n_tiles]
    t0 = ring * p

    def gather(tile, buf, sem):
        for r in range(te):
            tok = tab_ref[tile * te + r] & (n_tok - 1)
            pltpu.make_async_copy(h2_hbm.at[pl.ds(tok, 1), :], buf.at[pl.ds(r, 1), :], sem).start()

    def scatter(tile, buf, sem):
        for r in range(te):
            row = tab_ref[tile * te + r]
            pltpu.make_async_copy(buf.at[pl.ds(r, 1), :], y2_hbm.at[pl.ds(row, 1), :], sem).start()

    def wait_rows(buf, sem):
        pltpu.make_async_copy(h2_hbm.at[pl.ds(0, te), :], buf, sem).wait()

    def weight_copies(e):
        return (pltpu.make_async_copy(wgu_hbm.at[e], wgu_f, wsem.at[0]),
                pltpu.make_async_copy(wd_hbm.at[e], wd_f, wsem.at[1]))

    def dump_fill(h):
        rows = pl.ds(TOP_K * n_tok + h * te, te)
        return pltpu.make_async_copy(ys[ring - 1], y2_hbm.at[rows, :], zsem)

    def switch_weights(tile):
        e = texp_ref[tile]
        first = jnp.logical_or(tile == 0, texp_ref[jnp.maximum(tile - 1, 0)] != e)

        @pl.when(jnp.logical_and(first, tile < n_used))
        def _():
            @pl.when(tile == 0)
            def _():
                for cp in weight_copies(e):
                    cp.start()

            for cp in weight_copies(e):
                cp.wait()
            wgu_s[...] = wgu_f[...].astype(BF16)
            wd_s[...] = wd_f[...].astype(BF16)
            nxt = lax.while_loop(
                lambda k: jnp.logical_and(k < n_used, texp_ref[jnp.minimum(k, n_tiles - 1)] == e),
                lambda k: k + 1, tile + 1)

            @pl.when(nxt < n_used)
            def _():
                for cp in weight_copies(texp_ref[jnp.minimum(nxt, n_tiles - 1)]):
                    cp.start()

    def compute(xbuf, ybuf):
        xb16 = _unpack_bf16_pairs(xbuf[...]).astype(BF16)
        gu = jnp.dot(xb16, wgu_s[...], preferred_element_type=F32)
        a = (jax.nn.silu(gu[:, :de]) * gu[:, de:]).astype(BF16)
        ybuf[...] = _pack_bf16_pairs(jnp.dot(a, wd_s[...], preferred_element_type=F32))

    @pl.when(t0 < n_used)
    def _():
        @pl.when(p == 0)
        def _():
            ys[ring - 1][...] = jnp.zeros(ys[ring - 1].shape, U32)
            for h in range(DUMP_TILES):
                dump_fill(h).start()
            for h in range(DUMP_TILES):
                dump_fill(h).wait()
            for m in range(ring - 1):
                gather(m, xs[m], gsem.at[m])

        for m in range(ring):
            tile = t0 + m
            nm = (m + ring - 1) % ring
            switch_weights(tile)
            wait_rows(xs[m], gsem.at[m])
            if m == ring - 1:
                wait_rows(ys[m], ssem.at[m])
            else:
                @pl.when(p > 0)
                def _(m=m):
                    wait_rows(ys[m], ssem.at[m])
            gather(tile + ring - 1, xs[nm], gsem.at[nm])
            scatter(jnp.where(tile == 0, n_tiles, tile - 1), ys[nm], ssem.at[nm])
            compute(xs[m], ys[m])

        @pl.when(t0 + ring >= n_used)
        def _():
            for m in range(ring - 1):
                wait_rows(ys[m], ssem.at[m])
            scatter(t0 + ring - 1, ys[ring - 1], ssem.at[ring - 1])
            wait_rows(ys[ring - 1], ssem.at[ring - 1])
            for m in range(ring - 1):
                wait_rows(xs[m], gsem.at[m])


def _experts(tab, texp, h2p, wgu, wd, n_tiles):
    t, dp = h2p.shape
    te = EXPERT_TILE
    _, d, n_gu = wgu.shape
    de = wd.shape[1]
    assert n_tiles % EXPERT_RING == 0 and t & (t - 1) == 0
    body = functools.partial(_expert_body, n_tiles=n_tiles)
    vmem = ((d * n_gu + de * d) * (4 + 2) + 2 * EXPERT_RING * te * dp * 4 + te * (n_gu + 3 * d) * 4
            + 4 * 1024 * 1024)
    hbm = pl.BlockSpec(memory_space=pl.ANY)
    row_buf = pltpu.VMEM((te, dp), U32)
    return pl.pallas_call(
        body,
        grid_spec=pltpu.PrefetchScalarGridSpec(
            num_scalar_prefetch=2,
            grid=(n_tiles // EXPERT_RING,),
            in_specs=[hbm, hbm, hbm],
            out_specs=hbm,
            scratch_shapes=[row_buf] * (2 * EXPERT_RING) + [
                pltpu.VMEM((d, n_gu), F32), pltpu.VMEM((de, d), F32),
                pltpu.VMEM((d, n_gu), BF16), pltpu.VMEM((de, d), BF16),
                pltpu.SemaphoreType.DMA((EXPERT_RING,)), pltpu.SemaphoreType.DMA((EXPERT_RING,)),
                pltpu.SemaphoreType.DMA((2,)), pltpu.SemaphoreType.DMA,
            ],
        ),
        out_shape=jax.ShapeDtypeStruct((TOP_K * t + DUMP_TILES * te, dp), U32),
        compiler_params=_params(vmem, ("arbitrary",)),
        name="experts",
    )(tab, texp, h2p, wgu, wd)


def _combine_body(x1_ref, y0_ref, y1_ref, gcol_ref, mod_ref, modf_ref, gf_ref, o_ref):
    d = x1_ref.shape[1]
    gate2 = mod_ref[0][:, 5 * d:6 * d]
    modf = modf_ref[0]
    shift_f, scale_f = modf[:, 0:d], modf[:, d:2 * d]
    y0 = _unpack_bf16_pairs(y0_ref[...])
    y1 = _unpack_bf16_pairs(y1_ref[...])
    ffn = y0 * gcol_ref[:, 0:1] + y1 * gcol_ref[:, 1:2]
    x2 = x1_ref[...] + gate2 * ffn
    o_ref[...] = _rms(x2, gf_ref[...]) * (1.0 + scale_f) + shift_f


def _combine(x1, y2, gcol, mod3, modf3, gf, seq):
    t, d = x1.shape
    tm = TOKEN_TILE
    per_seq = seq // tm
    return pl.pallas_call(
        _combine_body,
        grid=(t // tm,),
        in_specs=[
            pl.BlockSpec((tm, d), lambda i: (i, 0)),
            pl.BlockSpec((tm, d // 2), lambda i: (i, 0)),
            pl.BlockSpec((tm, d // 2), lambda i: (t // tm + i, 0)),
            pl.BlockSpec((tm, V7X_LANES), lambda i: (i, 0)),
            pl.BlockSpec((1, 1, mod3.shape[2]), lambda i: (i // per_seq, 0, 0)),
            pl.BlockSpec((1, 1, modf3.shape[2]), lambda i: (i // per_seq, 0, 0)),
            pl.BlockSpec((1, d), lambda i: (0, 0)),
        ],
        out_specs=pl.BlockSpec((tm, d), lambda i: (i, 0)),
        out_shape=jax.ShapeDtypeStruct((t, d), F32),
        compiler_params=_params(32 * 1024 * 1024, ("arbitrary",)),
        name="combine",
    )(x1, y2, y2, gcol, mod3, modf3, gf)


def kernel(x, c, w_ada, b_ada, norm1_g, w_in, w_out, gmlp_w_s, gmlp_b_s, gmlp_v_gain, conv_w, conv_b,
           norm2_g, w_router_group, b_router_group, w_router_expert, b_router_expert, w_gate_up, w_down,
           w_ada_final, b_ada_final, norm_f_g):
    bsz, seq, d = x.shape
    depth = w_ada.shape[0]
    n_e = w_router_expert.shape[2]
    t = bsz * seq
    assert seq % TOKEN_TILE == 0 and TOKEN_TILE % CHUNK == 0
    assert w_router_group.shape[2] == N_GROUPS and n_e + N_GROUPS <= ROUTER_ROWS
    n_tiles = (t * TOP_K + n_e * (EXPERT_TILE - 1)) // EXPERT_TILE
    n_tiles = -(-n_tiles // EXPERT_RING) * EXPERT_RING

    modf3 = _ada(c, w_ada_final, b_ada_final).reshape(bsz, 1, 2 * d)
    xt = x.reshape(t, d)
    for l in range(depth):
        mod3 = _ada(c, w_ada[l], b_ada[l]).reshape(bsz, 1, -1)
        wr = jnp.concatenate([w_router_expert[l], w_router_group[l]], axis=1)
        wrt = jnp.pad(wr, ((0, 0), (0, ROUTER_ROWS - wr.shape[1]))).T.astype(BF16)
        br = jnp.concatenate([b_router_expert[l], b_router_group[l]])
        br = jnp.pad(br, (0, ROUTER_ROWS - br.shape[0])).reshape(ROUTER_ROWS, 1)
        x1, h2, eid, gcol = _mix(
            xt, mod3, norm1_g[l].reshape(1, d), norm2_g[l].reshape(1, d),
            w_in[l].astype(BF16), w_out[l].astype(BF16), gmlp_w_s[l], gmlp_b_s[l].T,
            gmlp_v_gain[l].reshape(1, -1), conv_w[l], conv_b[l].reshape(1, -1), wrt, br, seq)
        texp, tab = _dispatch(eid, n_e, n_tiles, t)
        y2 = _experts(tab, texp.reshape(-1), h2, w_gate_up[l], w_down[l], n_tiles)
        assert depth == 1
        xt = _combine(x1, y2, gcol, mod3, modf3, norm_f_g.reshape(1, d), seq)
    return xt.reshape(bsz, seq, d)
```

```python
import functools

import jax
import jax.numpy as jnp
from jax import lax
from jax.experimental import pallas as pl
from jax.experimental.pallas import tpu as pltpu

F32 = jnp.float32
BF16 = jnp.bfloat16
I32 = jnp.int32
U32 = jnp.uint32

A_HEADS = 8
CHUNK = 128
N_GROUPS = 4
TOP_K = 2
EPS = 1e-6

V7X_LANES = 128
V7X_SUBLANES = 8
V7X_VMEM_BYTES = 64 * 1024 * 1024

TOKEN_TILE = 256
EXPERT_TILE = 256
ROUTER_ROWS = 40
EXPERT_RING = 3
DUMP_TILES = 2


def _rms(x, g):
    y = x * lax.rsqrt(jnp.mean(x * x, axis=-1, keepdims=True) + EPS)
    return y * g


def _pack_bf16_pairs(x):
    n = x.shape[1] // 2
    lo = lax.bitcast_convert_type(x[:, :n].astype(BF16).astype(F32), U32)
    hi = lax.bitcast_convert_type(x[:, n:].astype(BF16).astype(F32), U32)
    return hi | (lo >> 16)


def _unpack_bf16_pairs(w):
    lo = lax.bitcast_convert_type(w << 16, F32)
    hi = lax.bitcast_convert_type(w & jnp.uint32(0xFFFF0000), F32)
    return jnp.concatenate([lo, hi], axis=1)


def _zero_after(anchor, n):
    bits = lax.bitcast_convert_type(anchor, U32)
    zero = lax.bitcast_convert_type((bits >> 16) >> 16, F32)
    return jnp.concatenate([zero] * (n // anchor.shape[1]), axis=1)


def _params(vmem_bytes, semantics=None):
    kw = dict(vmem_limit_bytes=int(vmem_bytes))
    if semantics is not None:
        kw["dimension_semantics"] = semantics
    return pltpu.CompilerParams(**kw)


def _ada_body(c_ref, w_ref, b_ref, o_ref):
    ca = jax.nn.silu(c_ref[...]).astype(BF16)
    o_ref[...] = jnp.dot(ca, w_ref[...].astype(BF16), preferred_element_type=F32) + b_ref[...]


def _ada(c, w, b, tn=1024):
    bsz, d = c.shape
    n = w.shape[1]
    return pl.pallas_call(
        _ada_body,
        grid=(n // tn,),
        in_specs=[
            pl.BlockSpec((bsz, d), lambda j: (0, 0)),
            pl.BlockSpec((d, tn), lambda j: (0, j)),
            pl.BlockSpec((1, tn), lambda j: (0, j)),
        ],
        out_specs=pl.BlockSpec((bsz, tn), lambda j: (0, j)),
        out_shape=jax.ShapeDtypeStruct((bsz, n), F32),
        compiler_params=_params(4 * d * tn * 4, ("arbitrary",)),
        name="ada",
    )(c, w, b.reshape(1, n))


def _route(lt, tm):
    n_e = 32
    epg = n_e // N_GROUPS
    row = lax.broadcasted_iota(I32, (V7X_SUBLANES, tm), 0).astype(F32)
    gl = lt[n_e:n_e + V7X_SUBLANES]
    gvalid = row < float(N_GROUPS)
    glm = jnp.where(gvalid, gl, -jnp.inf)
    gmax = jnp.max(glm, axis=0, keepdims=True)
    garg = jnp.min(jnp.where(glm == gmax, row, float(V7X_SUBLANES)), axis=0, keepdims=True)
    gsum = jnp.sum(jnp.where(gvalid, jnp.exp(gl - gmax), 0.0), axis=0, keepdims=True)
    p_grp = 1.0 / gsum
    es = lt[0:epg]
    for g in range(1, N_GROUPS):
        es = jnp.where(garg == float(g), lt[g * epg:(g + 1) * epg], es)
    m1 = jnp.max(es, axis=0, keepdims=True)
    i1 = jnp.min(jnp.where(es == m1, row, float(epg)), axis=0, keepdims=True)
    es2 = jnp.where(row == i1, -jnp.inf, es)
    m2 = jnp.max(es2, axis=0, keepdims=True)
    i2 = jnp.min(jnp.where(es2 == m2, row, float(epg)), axis=0, keepdims=True)
    z = jnp.exp(m2 - m1)
    den = 1.0 + z
    g0 = p_grp / den
    g1 = p_grp * z / den
    e0 = garg * float(epg) + i1
    e1 = garg * float(epg) + i2
    zero = jnp.zeros_like(g0)
    return jnp.concatenate([g0, g1, e0, e1, zero, zero, zero, zero], axis=0)


def _mix_body(x_ref, xp_ref, mod_ref, modp_ref, g1_ref, g2_ref, win_ref, wout_ref, ws_ref, bst_ref, vg_ref,
              cw_ref, cb_ref, wrt_ref, br_ref,
              x1_ref, h2_ref, eid_ref, gcol_ref, pre_scr, mix_scr, *, tiles_per_seq):
    tm, d = x_ref.shape
    aw = vg_ref.shape[1]
    bw = cb_ref.shape[1]
    hd = aw // A_HEADS
    nch = tm // CHUNK
    i = pl.program_id(0)

    @pl.when(i == 0)
    def _():
        mix_scr[...] = jnp.zeros(mix_scr.shape, F32)

    x = x_ref[...]
    mod = mod_ref[0]
    shift1, scale1 = mod[:, 0:d], mod[:, d:2 * d]
    h = _rms(x, g1_ref[...]) * (1.0 + scale1) + shift1
    hb = h.astype(BF16)

    uv = jnp.dot(hb, win_ref[:, 0:2 * aw], preferred_element_type=F32)

    modp = modp_ref[0]
    gate1p, shift2p, scale2p = modp[:, 2 * d:3 * d], modp[:, 3 * d:4 * d], modp[:, 4 * d:5 * d]
    gate1p = gate1p + _zero_after(uv[0:1, 0:V7X_LANES], d)
    x1 = xp_ref[...] + gate1p * mix_scr[...]
    x1_ref[...] = x1
    h2 = _rms(x1, g2_ref[...]) * (1.0 + scale2p) + shift2p
    h2_ref[...] = _pack_bf16_pairs(h2)
    lt = lax.dot_general(wrt_ref[...], h2.astype(BF16), (((1,), (1,)), ((), ())),
                         preferred_element_type=F32) + br_ref[...]
    slab = _route(lt, tm)
    eid_ref[0] = slab[2:4].astype(I32)
    wide = jnp.concatenate([slab, jnp.zeros((V7X_LANES - V7X_SUBLANES, tm), F32)], axis=0)
    gcol_ref[...] = wide.T

    uv = jax.nn.gelu(uv)
    u, v = uv[:, :aw], uv[:, aw:]
    tq_r = lax.broadcasted_iota(I32, (CHUNK, CHUNK), 0)
    tq_c = lax.broadcasted_iota(I32, (CHUNK, CHUNK), 1)
    causal = tq_c <= tq_r
    ya = []
    for hh in range(A_HEADS):
        sl = slice(hh * hd, (hh + 1) * hd)
        vh = _rms(v[:, sl], vg_ref[:, sl]).astype(BF16)
        rhs = jnp.concatenate([vh[c * CHUNK:(c + 1) * CHUNK] for c in range(nch)], axis=1)
        w = jnp.where(causal, ws_ref[hh], 0.0).astype(BF16)
        zs = jnp.dot(w, rhs, preferred_element_type=F32) + bst_ref[:, hh:hh + 1]
        zs = jnp.concatenate([zs[:, c * hd:(c + 1) * hd] for c in range(nch)], axis=0)
        ya.append(u[:, sl] * zs)

    bcx = jnp.dot(hb, win_ref[:, 2 * aw:], preferred_element_type=F32)
    bg, cg, xin = bcx[:, :bw], bcx[:, bw:2 * bw], bcx[:, 2 * bw:]
    pre = cg * xin

    @pl.when(i % tiles_per_seq == 0)
    def _():
        pre_scr[0:V7X_SUBLANES, :] = jnp.zeros((V7X_SUBLANES, bw), F32)

    pre_scr[V7X_SUBLANES:V7X_SUBLANES + tm, :] = pre
    p1 = pre_scr[V7X_SUBLANES - 1:V7X_SUBLANES - 1 + tm, :]
    p2 = pre_scr[V7X_SUBLANES - 2:V7X_SUBLANES - 2 + tm, :]
    conv = cw_ref[0:1, :] * p2 + cw_ref[1:2, :] * p1 + cw_ref[2:3, :] * pre + cb_ref[...]
    yb = bg * conv
    pre_scr[0:V7X_SUBLANES, :] = pre_scr[tm:tm + V7X_SUBLANES, :]

    y = jnp.concatenate(ya + [yb], axis=1).astype(BF16)
    mix_scr[...] = jnp.dot(y, wout_ref[...], preferred_element_type=F32)


def _mix(xt, mod3, g1, g2, win, wout, ws, bst, vg, cw, cb, wrt, br, seq):
    t, d = xt.shape
    tm = TOKEN_TILE
    nt = t // tm
    per_seq = seq // tm
    in_cols = win.shape[1]
    bw = cb.shape[1]
    const2 = lambda i: (0, 0)
    single = dict(pipeline_mode=pl.Buffered(1))
    body = functools.partial(_mix_body, tiles_per_seq=per_seq)
    cur = lambda i: jnp.minimum(i, nt - 1)
    prev = lambda i: jnp.maximum(i - 1, 0)
    vmem = (2 * d * (in_cols + d)
            + 3 * 2 * tm * d * 4
            + 2 * tm * d * 2
            + (tm + V7X_SUBLANES) * bw * 4 + tm * d * 4
            + tm * (in_cols + 4 * d) * 4
            + 8 * 1024 * 1024)
    return pl.pallas_call(
        body,
        grid=(nt + 1,),
        in_specs=[
            pl.BlockSpec((tm, d), lambda i: (cur(i), 0)),
            pl.BlockSpec((tm, d), lambda i: (prev(i), 0)),
            pl.BlockSpec((1, 1, mod3.shape[2]), lambda i: (cur(i) // per_seq, 0, 0)),
            pl.BlockSpec((1, 1, mod3.shape[2]), lambda i: (prev(i) // per_seq, 0, 0)),
            pl.BlockSpec((1, d), const2),
            pl.BlockSpec((1, d), const2),
            pl.BlockSpec(win.shape, const2, **single),
            pl.BlockSpec(wout.shape, const2, **single),
            pl.BlockSpec(ws.shape, lambda i: (0, 0, 0)),
            pl.BlockSpec(bst.shape, const2),
            pl.BlockSpec(vg.shape, const2),
            pl.BlockSpec(cw.shape, const2),
            pl.BlockSpec(cb.shape, const2),
            pl.BlockSpec(wrt.shape, const2),
            pl.BlockSpec(br.shape, const2),
        ],
        out_specs=[
            pl.BlockSpec((tm, d), lambda i: (prev(i), 0)),
            pl.BlockSpec((tm, d // 2), lambda i: (prev(i), 0)),
            pl.BlockSpec((1, TOP_K, tm), lambda i: (prev(i), 0, 0)),
            pl.BlockSpec((tm, V7X_LANES), lambda i: (prev(i), 0)),
        ],
        out_shape=[
            jax.ShapeDtypeStruct((t, d), F32),
            jax.ShapeDtypeStruct((t, d // 2), U32),
            jax.ShapeDtypeStruct((nt, TOP_K, tm), I32),
            jax.ShapeDtypeStruct((t, V7X_LANES), F32),
        ],
        scratch_shapes=[pltpu.VMEM((tm + V7X_SUBLANES, bw), F32), pltpu.VMEM((tm, d), F32)],
        compiler_params=_params(min(vmem, V7X_VMEM_BYTES - 6 * 1024 * 1024), ("arbitrary",)),
        name="mix",
    )(xt, xt, mod3, mod3, g1, g2, win, wout, ws, bst, vg, cw, cb, wrt, br)


def _dispatch_body(eid_ref, texp_ref, tab_ref, rank_scr, dest_v, dest_s, init_v, sem,
                   *, n_e, n_tiles, n_tok):
    nch, _, c = eid_ref.shape
    te = EXPERT_TILE
    r = lax.broadcasted_iota(I32, (c, c), 0)
    q = lax.broadcasted_iota(I32, (c, c), 1)
    before = (r < q).astype(BF16)
    e_iota = lax.broadcasted_iota(I32, (n_e, c), 0)

    init_v[...] = (lax.broadcasted_iota(I32, init_v.shape, 0) & (DUMP_TILES * te - 1)) + TOP_K * n_tok
    cp_i = pltpu.make_async_copy(init_v, tab_ref, sem.at[0])
    cp_i.start()

    def count_body(ch, carry):
        e2 = eid_ref[ch]
        ranks = []
        for k in range(TOP_K):
            oh = e_iota == e2[k:k + 1]
            ohf = oh.astype(F32)
            pref = jnp.dot(ohf.astype(BF16), before, preferred_element_type=F32)
            ranks.append(jnp.sum(jnp.where(oh, pref + carry, 0.0), axis=0, keepdims=True))
            carry = carry + jnp.sum(ohf, axis=1, keepdims=True)
        rank_scr[ch] = jnp.concatenate(ranks, axis=0)
        return carry

    counts = lax.fori_loop(0, nch, count_body, jnp.zeros((n_e, 1), F32))

    padded = jnp.floor((counts + float(te - 1)) / float(te)) * float(te)
    sub = lax.broadcasted_iota(I32, (n_e, V7X_LANES), 0)
    lane = lax.broadcasted_iota(I32, (n_e, V7X_LANES), 1)
    pstart_row = jnp.sum(jnp.where(sub < lane, padded, 0.0), axis=0, keepdims=True)
    pstart = jnp.sum(jnp.where(sub == lane, pstart_row, 0.0), axis=1, keepdims=True)
    pend = pstart + padded
    total = jnp.max(pend, axis=0, keepdims=True)
    last_e = jnp.max(jnp.where(counts > 0.0, sub[:, 0:1].astype(F32), -1.0), axis=0, keepdims=True)

    group = V7X_SUBLANES // TOP_K

    def dest_body(g, carry):
        rows = []
        for j in range(group):
            e2 = eid_ref[g * group + j]
            rk = rank_scr[g * group + j]
            for k in range(TOP_K):
                oh = e_iota == e2[k:k + 1]
                off = jnp.sum(jnp.where(oh, pstart, 0.0), axis=0, keepdims=True)
                rows.append(off + rk[k:k + 1])
        row0 = pl.multiple_of(g * V7X_SUBLANES, V7X_SUBLANES)
        dest_v[pl.ds(row0, V7X_SUBLANES), :] = jnp.concatenate(rows, axis=0).astype(I32)
        return carry

    lax.fori_loop(0, nch // group, dest_body, 0)
    cp_d = pltpu.make_async_copy(dest_v, dest_s, sem.at[1])
    cp_d.start()

    w = texp_ref.shape[1]
    tj = lax.broadcasted_iota(I32, (n_e, w), 1).astype(F32) * float(te)
    texp = jnp.sum((pend <= tj).astype(F32), axis=0, keepdims=True)
    tj1 = tj[0:1]
    texp = jnp.where(tj1 < total, jnp.minimum(texp, float(n_e - 1)), last_e)
    lane_w = lax.broadcasted_iota(I32, (1, w), 1)
    texp = jnp.where(lane_w == n_tiles, total / float(te), texp)
    texp_ref[...] = texp.astype(I32)

    cp_i.wait()
    cp_d.wait()

    def row_body(rw, carry):
        a0 = (rw % TOP_K) * n_tok + (rw // TOP_K) * c
        for col in range(c):
            tab_ref[dest_s[rw, col]] = a0 + col
        return carry

    lax.fori_loop(0, nch * TOP_K, row_body, 0)


def _dispatch(eid, n_e, n_tiles, n_tok):
    nch, k, c = eid.shape
    assert c == EXPERT_TILE and EXPERT_TILE & (EXPERT_TILE - 1) == 0
    body = functools.partial(_dispatch_body, n_e=n_e, n_tiles=n_tiles, n_tok=n_tok)
    w = 2 * V7X_LANES
    assert n_tiles < w
    rows = n_tiles + V7X_SUBLANES
    rows += (-rows) % 4
    smem = pl.BlockSpec(memory_space=pltpu.SMEM)
    return pl.pallas_call(
        body,
        out_specs=[pl.BlockSpec(memory_space=pltpu.VMEM), smem],
        out_shape=[
            jax.ShapeDtypeStruct((1, w), I32),
            jax.ShapeDtypeStruct((rows * EXPERT_TILE,), I32),
        ],
        scratch_shapes=[
            pltpu.VMEM((nch, k, c), F32),
            pltpu.VMEM((nch * k, c), I32),
            pltpu.SMEM((nch * k, c), I32),
            pltpu.VMEM((rows * EXPERT_TILE,), I32),
            pltpu.SemaphoreType.DMA((2,)),
        ],
        compiler_params=_params(16 * 1024 * 1024),
        name="dispatch",
    )(eid)


def _expert_body(tab_ref, texp_ref, h2_hbm, wgu_hbm, wd_hbm, y2_hbm,
                 x0, x1, x2, y0, y1, y2, wgu_f, wd_f, wgu_s, wd_s, gsem, ssem, wsem, zsem, *, n_tiles):
    te = EXPERT_TILE
    ring = EXPERT_RING
    de = wd_s.shape[0]
    n_tok = h2_hbm.shape[0]
    xs, ys = (x0, x1, x2), (y0, y1, y2)
    p = pl.program_id(0)
    n_used = texp_ref[n_tiles]
    t0 = ring * p

    def gather(tile, buf, sem):
        for r in range(te):
            tok = tab_ref[tile * te + r] & (n_tok - 1)
            pltpu.make_async_copy(h2_hbm.at[pl.ds(tok, 1), :], buf.at[pl.ds(r, 1), :], sem).start()

    def scatter(tile, buf, sem):
        for r in range(te):
            row = tab_ref[tile * te + r]
            pltpu.make_async_copy(buf.at[pl.ds(r, 1), :], y2_hbm.at[pl.ds(row, 1), :], sem).start()

    def wait_rows(buf, sem):
        pltpu.make_async_copy(h2_hbm.at[pl.ds(0, te), :], buf, sem).wait()

    def weight_copies(e):
        return (pltpu.make_async_copy(wgu_hbm.at[e], wgu_f, wsem.at[0]),
                pltpu.make_async_copy(wd_hbm.at[e], wd_f, wsem.at[1]))

    def dump_fill(h):
        rows = pl.ds(TOP_K * n_tok + h * te, te)
        return pltpu.make_async_copy(ys[ring - 1], y2_hbm.at[rows, :], zsem)

    def switch_weights(tile):
        e = texp_ref[tile]
        first = jnp.logical_or(tile == 0, texp_ref[jnp.maximum(tile - 1, 0)] != e)

        @pl.when(jnp.logical_and(first, tile < n_used))
        def _():
            @pl.when(tile == 0)
            def _():
                for cp in weight_copies(e):
                    cp.start(priority=1)

            for cp in weight_copies(e):
                cp.wait()
            wgu_s[...] = wgu_f[...].astype(BF16)
            wd_s[...] = wd_f[...].astype(BF16)
            nxt = lax.while_loop(
                lambda k: jnp.logical_and(k < n_used, texp_ref[jnp.minimum(k, n_tiles - 1)] == e),
                lambda k: k + 1, tile + 1)

            @pl.when(nxt < n_used)
            def _():
                for cp in weight_copies(texp_ref[jnp.minimum(nxt, n_tiles - 1)]):
                    cp.start(priority=1)

    def compute(xbuf, ybuf):
        xb16 = _unpack_bf16_pairs(xbuf[...]).astype(BF16)
        gu = jnp.dot(xb16, wgu_s[...], preferred_element_type=F32)
        a = (jax.nn.silu(gu[:, :de]) * gu[:, de:]).astype(BF16)
        ybuf[...] = _pack_bf16_pairs(jnp.dot(a, wd_s[...], preferred_element_type=F32))

    @pl.when(t0 < n_used)
    def _():
        @pl.when(p == 0)
        def _():
            ys[ring - 1][...] = jnp.zeros(ys[ring - 1].shape, U32)
            for h in range(DUMP_TILES):
                dump_fill(h).start()
            for h in range(DUMP_TILES):
                dump_fill(h).wait()
            for m in range(ring - 1):
                gather(m, xs[m], gsem.at[m])

        for m in range(ring):
            tile = t0 + m
            nm = (m + ring - 1) % ring
            switch_weights(tile)
            wait_rows(xs[m], gsem.at[m])
            if m == ring - 1:
                wait_rows(ys[m], ssem.at[m])
            else:
                @pl.when(p > 0)
                def _(m=m):
                    wait_rows(ys[m], ssem.at[m])
            gather(tile + ring - 1, xs[nm], gsem.at[nm])
            scatter(jnp.where(tile == 0, n_tiles, tile - 1), ys[nm], ssem.at[nm])
            compute(xs[m], ys[m])

        @pl.when(t0 + ring >= n_used)
        def _():
            for m in range(ring - 1):
                wait_rows(ys[m], ssem.at[m])
            scatter(t0 + ring - 1, ys[ring - 1], ssem.at[ring - 1])
            wait_rows(ys[ring - 1], ssem.at[ring - 1])
            for m in range(ring - 1):
                wait_rows(xs[m], gsem.at[m])


def _experts(tab, texp, h2p, wgu, wd, n_tiles):
    t, dp = h2p.shape
    te = EXPERT_TILE
    _, d, n_gu = wgu.shape
    de = wd.shape[1]
    assert n_tiles % EXPERT_RING == 0 and t & (t - 1) == 0
    body = functools.partial(_expert_body, n_tiles=n_tiles)
    vmem = ((d * n_gu + de * d) * (4 + 2) + 2 * EXPERT_RING * te * dp * 4 + te * (n_gu + 3 * d) * 4
            + 4 * 1024 * 1024)
    hbm = pl.BlockSpec(memory_space=pl.ANY)
    row_buf = pltpu.VMEM((te, dp), U32)
    return pl.pallas_call(
        body,
        grid_spec=pltpu.PrefetchScalarGridSpec(
            num_scalar_prefetch=2,
            grid=(n_tiles // EXPERT_RING,),
            in_specs=[hbm, hbm, hbm],
            out_specs=hbm,
            scratch_shapes=[row_buf] * (2 * EXPERT_RING) + [
                pltpu.VMEM((d, n_gu), F32), pltpu.VMEM((de, d), F32),
                pltpu.VMEM((d, n_gu), BF16), pltpu.VMEM((de, d), BF16),
                pltpu.SemaphoreType.DMA((EXPERT_RING,)), pltpu.SemaphoreType.DMA((EXPERT_RING,)),
                pltpu.SemaphoreType.DMA((2,)), pltpu.SemaphoreType.DMA,
            ],
        ),
        out_shape=jax.ShapeDtypeStruct((TOP_K * t + DUMP_TILES * te, dp), U32),
        compiler_params=_params(vmem, ("arbitrary",)),
        name="experts",
    )(tab, texp, h2p, wgu, wd)


def _combine_body(x1_ref, y0_ref, y1_ref, gcol_ref, mod_ref, modf_ref, gf_ref, o_ref):
    d = x1_ref.shape[1]
    gate2 = mod_ref[0][:, 5 * d:6 * d]
    modf = modf_ref[0]
    shift_f, scale_f = modf[:, 0:d], modf[:, d:2 * d]
    y0 = _unpack_bf16_pairs(y0_ref[...])
    y1 = _unpack_bf16_pairs(y1_ref[...])
    ffn = y0 * gcol_ref[:, 0:1] + y1 * gcol_ref[:, 1:2]
    x2 = x1_ref[...] + gate2 * ffn
    o_ref[...] = _rms(x2, gf_ref[...]) * (1.0 + scale_f) + shift_f


def _combine(x1, y2, gcol, mod3, modf3, gf, seq):
    t, d = x1.shape
    tm = TOKEN_TILE
    per_seq = seq // tm
    return pl.pallas_call(
        _combine_body,
        grid=(t // tm,),
        in_specs=[
            pl.BlockSpec((tm, d), lambda i: (i, 0)),
            pl.BlockSpec((tm, d // 2), lambda i: (i, 0)),
            pl.BlockSpec((tm, d // 2), lambda i: (t // tm + i, 0)),
            pl.BlockSpec((tm, V7X_LANES), lambda i: (i, 0)),
            pl.BlockSpec((1, 1, mod3.shape[2]), lambda i: (i // per_seq, 0, 0)),
            pl.BlockSpec((1, 1, modf3.shape[2]), lambda i: (i // per_seq, 0, 0)),
            pl.BlockSpec((1, d), lambda i: (0, 0)),
        ],
        out_specs=pl.BlockSpec((tm, d), lambda i: (i, 0)),
        out_shape=jax.ShapeDtypeStruct((t, d), F32),
        compiler_params=_params(32 * 1024 * 1024, ("arbitrary",)),
        name="combine",
    )(x1, y2, y2, gcol, mod3, modf3, gf)


def kernel(x, c, w_ada, b_ada, norm1_g, w_in, w_out, gmlp_w_s, gmlp_b_s, gmlp_v_gain, conv_w, conv_b,
           norm2_g, w_router_group, b_router_group, w_router_expert, b_router_expert, w_gate_up, w_down,
           w_ada_final, b_ada_final, norm_f_g):
    bsz, seq, d = x.shape
    depth = w_ada.shape[0]
    n_e = w_router_expert.shape[2]
    t = bsz * seq
    assert seq % TOKEN_TILE == 0 and TOKEN_TILE % CHUNK == 0
    assert w_router_group.shape[2] == N_GROUPS and n_e + N_GROUPS <= ROUTER_ROWS
    n_tiles = (t * TOP_K + n_e * (EXPERT_TILE - 1)) // EXPERT_TILE
    n_tiles = -(-n_tiles // EXPERT_RING) * EXPERT_RING

    modf3 = _ada(c, w_ada_final, b_ada_final).reshape(bsz, 1, 2 * d)
    xt = x.reshape(t, d)
    for l in range(depth):
        mod3 = _ada(c, w_ada[l], b_ada[l]).reshape(bsz, 1, -1)
        wr = jnp.concatenate([w_router_expert[l], w_router_group[l]], axis=1)
        wrt = jnp.pad(wr, ((0, 0), (0, ROUTER_ROWS - wr.shape[1]))).T.astype(BF16)
        br = jnp.concatenate([b_router_expert[l], b_router_group[l]])
        br = jnp.pad(br, (0, ROUTER_ROWS - br.shape[0])).reshape(ROUTER_ROWS, 1)
        x1, h2, eid, gcol = _mix(
            xt, mod3, norm1_g[l].reshape(1, d), norm2_g[l].reshape(1, d),
            w_in[l].astype(BF16), w_out[l].astype(BF16), gmlp_w_s[l], gmlp_b_s[l].T,
            gmlp_v_gain[l].reshape(1, -1), conv_w[l], conv_b[l].reshape(1, -1), wrt, br, seq)
        texp, tab = _dispatch(eid, n_e, n_tiles, t)
        y2 = _experts(tab, texp.reshape(-1), h2, w_gate_up[l], w_down[l], n_tiles)
        assert depth == 1
        xt = _combine(x1, y2, gcol, mod3, modf3, norm_f_g.reshape(1, d), seq)
    return xt.reshape(bsz, seq, d)
```

```python
import functools

import jax
import jax.numpy as jnp
from jax import lax
from jax.experimental import pallas as pl
from jax.experimental.pallas import tpu as pltpu

F32 = jnp.float32
BF16 = jnp.bfloat16
I32 = jnp.int32
U32 = jnp.uint32

A_HEADS = 8
CHUNK = 128
N_GROUPS = 4
TOP_K = 2
EPS = 1e-6

V7X_LANES = 128
V7X_SUBLANES = 8
V7X_VMEM_BYTES = 64 * 1024 * 1024

TOKEN_TILE = 256
EXPERT_TILE = 256
COMBINE_TILE = 512
ROUTER_ROWS = 40
EXPERT_RING = 3
DUMP_TILES = 2


def _rms(x, g):
    y = x * lax.rsqrt(jnp.mean(x * x, axis=-1, keepdims=True) + EPS)
    return y * g


def _pack_bf16_pairs(x):
    n = x.shape[1] // 2
    lo = lax.bitcast_convert_type(x[:, :n].astype(BF16).astype(F32), U32)
    hi = lax.bitcast_convert_type(x[:, n:].astype(BF16).astype(F32), U32)
    return hi | (lo >> 16)


def _unpack_bf16_pairs(w):
    lo = lax.bitcast_convert_type(w << 16, F32)
    hi = lax.bitcast_convert_type(w & jnp.uint32(0xFFFF0000), F32)
    return jnp.concatenate([lo, hi], axis=1)


def _zero_after(anchor, n):
    bits = lax.bitcast_convert_type(anchor, U32)
    zero = lax.bitcast_convert_type((bits >> 16) >> 16, F32)
    return jnp.concatenate([zero] * (n // anchor.shape[1]), axis=1)


def _params(vmem_bytes, semantics=None):
    kw = dict(vmem_limit_bytes=int(vmem_bytes))
    if semantics is not None:
        kw["dimension_semantics"] = semantics
    return pltpu.CompilerParams(**kw)


def _ada_body(c_ref, w_ref, b_ref, o_ref):
    ca = jax.nn.silu(c_ref[...]).astype(BF16)
    o_ref[...] = jnp.dot(ca, w_ref[...].astype(BF16), preferred_element_type=F32) + b_ref[...]


def _ada(c, w, b, tn=1024):
    bsz, d = c.shape
    n = w.shape[1]
    return pl.pallas_call(
        _ada_body,
        grid=(n // tn,),
        in_specs=[
            pl.BlockSpec((bsz, d), lambda j: (0, 0)),
            pl.BlockSpec((d, tn), lambda j: (0, j)),
            pl.BlockSpec((1, tn), lambda j: (0, j)),
        ],
        out_specs=pl.BlockSpec((bsz, tn), lambda j: (0, j)),
        out_shape=jax.ShapeDtypeStruct((bsz, n), F32),
        compiler_params=_params(4 * d * tn * 4, ("arbitrary",)),
        name="ada",
    )(c, w, b.reshape(1, n))


def _route(lt, tm):
    n_e = 32
    epg = n_e // N_GROUPS
    row = lax.broadcasted_iota(I32, (V7X_SUBLANES, tm), 0).astype(F32)
    gl = lt[n_e:n_e + V7X_SUBLANES]
    gvalid = row < float(N_GROUPS)
    glm = jnp.where(gvalid, gl, -jnp.inf)
    gmax = jnp.max(glm, axis=0, keepdims=True)
    garg = jnp.min(jnp.where(glm == gmax, row, float(V7X_SUBLANES)), axis=0, keepdims=True)
    gsum = jnp.sum(jnp.where(gvalid, jnp.exp(gl - gmax), 0.0), axis=0, keepdims=True)
    p_grp = 1.0 / gsum
    es = lt[0:epg]
    for g in range(1, N_GROUPS):
        es = jnp.where(garg == float(g), lt[g * epg:(g + 1) * epg], es)
    m1 = jnp.max(es, axis=0, keepdims=True)
    i1 = jnp.min(jnp.where(es == m1, row, float(epg)), axis=0, keepdims=True)
    es2 = jnp.where(row == i1, -jnp.inf, es)
    m2 = jnp.max(es2, axis=0, keepdims=True)
    i2 = jnp.min(jnp.where(es2 == m2, row, float(epg)), axis=0, keepdims=True)
    z = jnp.exp(m2 - m1)
    den = 1.0 + z
    g0 = p_grp / den
    g1 = p_grp * z / den
    e0 = garg * float(epg) + i1
    e1 = garg * float(epg) + i2
    zero = jnp.zeros_like(g0)
    return jnp.concatenate([g0, g1, e0, e1, zero, zero, zero, zero], axis=0)


def _mix_body(x_ref, xp_ref, mod_ref, modp_ref, g1_ref, g2_ref, win_ref, wout_ref, ws_ref, bst_ref, vg_ref,
              cw_ref, cb_ref, wrt_ref, br_ref,
              x1_ref, h2_ref, eid_ref, gcol_ref, pre_scr, mix_scr, *, tiles_per_seq):
    tm, d = x_ref.shape
    aw = vg_ref.shape[1]
    bw = cb_ref.shape[1]
    hd = aw // A_HEADS
    nch = tm // CHUNK
    i = pl.program_id(0)

    @pl.when(i == 0)
    def _():
        mix_scr[...] = jnp.zeros(mix_scr.shape, F32)

    x = x_ref[...]
    mod = mod_ref[0]
    shift1, scale1 = mod[:, 0:d], mod[:, d:2 * d]
    h = _rms(x, g1_ref[...]) * (1.0 + scale1) + shift1
    hb = h.astype(BF16)

    uv = jnp.dot(hb, win_ref[:, 0:2 * aw], preferred_element_type=F32)

    modp = modp_ref[0]
    gate1p, shift2p, scale2p = modp[:, 2 * d:3 * d], modp[:, 3 * d:4 * d], modp[:, 4 * d:5 * d]
    gate1p = gate1p + _zero_after(uv[0:1, 0:V7X_LANES], d)
    x1 = xp_ref[...] + gate1p * mix_scr[...]
    x1_ref[...] = x1
    h2 = _rms(x1, g2_ref[...]) * (1.0 + scale2p) + shift2p
    h2_ref[...] = _pack_bf16_pairs(h2)
    lt = lax.dot_general(wrt_ref[...], h2.astype(BF16), (((1,), (1,)), ((), ())),
                         preferred_element_type=F32) + br_ref[...]
    slab = _route(lt, tm)
    eid_ref[0] = slab[2:4].astype(I32)
    wide = jnp.concatenate([slab, jnp.zeros((V7X_LANES - V7X_SUBLANES, tm), F32)], axis=0)
    gcol_ref[...] = wide.T

    uv = jax.nn.gelu(uv)
    u, v = uv[:, :aw], uv[:, aw:]
    tq_r = lax.broadcasted_iota(I32, (CHUNK, CHUNK), 0)
    tq_c = lax.broadcasted_iota(I32, (CHUNK, CHUNK), 1)
    causal = tq_c <= tq_r
    ya = []
    for hh in range(A_HEADS):
        sl = slice(hh * hd, (hh + 1) * hd)
        vh = _rms(v[:, sl], vg_ref[:, sl]).astype(BF16)
        rhs = jnp.concatenate([vh[c * CHUNK:(c + 1) * CHUNK] for c in range(nch)], axis=1)
        w = jnp.where(causal, ws_ref[hh], 0.0).astype(BF16)
        zs = jnp.dot(w, rhs, preferred_element_type=F32) + bst_ref[:, hh:hh + 1]
        zs = jnp.concatenate([zs[:, c * hd:(c + 1) * hd] for c in range(nch)], axis=0)
        ya.append(u[:, sl] * zs)

    bcx = jnp.dot(hb, win_ref[:, 2 * aw:], preferred_element_type=F32)
    bg, cg, xin = bcx[:, :bw], bcx[:, bw:2 * bw], bcx[:, 2 * bw:]
    pre = cg * xin

    @pl.when(i % tiles_per_seq == 0)
    def _():
        pre_scr[0:V7X_SUBLANES, :] = jnp.zeros((V7X_SUBLANES, bw), F32)

    pre_scr[V7X_SUBLANES:V7X_SUBLANES + tm, :] = pre
    p1 = pre_scr[V7X_SUBLANES - 1:V7X_SUBLANES - 1 + tm, :]
    p2 = pre_scr[V7X_SUBLANES - 2:V7X_SUBLANES - 2 + tm, :]
    conv = cw_ref[0:1, :] * p2 + cw_ref[1:2, :] * p1 + cw_ref[2:3, :] * pre + cb_ref[...]
    yb = bg * conv
    pre_scr[0:V7X_SUBLANES, :] = pre_scr[tm:tm + V7X_SUBLANES, :]

    y = jnp.concatenate(ya + [yb], axis=1).astype(BF16)
    mix_scr[...] = jnp.dot(y, wout_ref[...], preferred_element_type=F32)


def _mix(xt, mod3, g1, g2, win, wout, ws, bst, vg, cw, cb, wrt, br, seq):
    t, d = xt.shape
    tm = TOKEN_TILE
    nt = t // tm
    per_seq = seq // tm
    in_cols = win.shape[1]
    bw = cb.shape[1]
    const2 = lambda i: (0, 0)
    single = dict(pipeline_mode=pl.Buffered(1))
    body = functools.partial(_mix_body, tiles_per_seq=per_seq)
    cur = lambda i: jnp.minimum(i, nt - 1)
    prev = lambda i: jnp.maximum(i - 1, 0)
    vmem = (2 * d * (in_cols + d)
            + 3 * 2 * tm * d * 4
            + 2 * tm * d * 2
            + (tm + V7X_SUBLANES) * bw * 4 + tm * d * 4
            + tm * (in_cols + 4 * d) * 4
            + 8 * 1024 * 1024)
    return pl.pallas_call(
        body,
        grid=(nt + 1,),
        in_specs=[
            pl.BlockSpec((tm, d), lambda i: (cur(i), 0)),
            pl.BlockSpec((tm, d), lambda i: (prev(i), 0)),
            pl.BlockSpec((1, 1, mod3.shape[2]), lambda i: (cur(i) // per_seq, 0, 0)),
            pl.BlockSpec((1, 1, mod3.shape[2]), lambda i: (prev(i) // per_seq, 0, 0)),
            pl.BlockSpec((1, d), const2),
            pl.BlockSpec((1, d), const2),
            pl.BlockSpec(win.shape, const2, **single),
            pl.BlockSpec(wout.shape, const2, **single),
            pl.BlockSpec(ws.shape, lambda i: (0, 0, 0)),
            pl.BlockSpec(bst.shape, const2),
            pl.BlockSpec(vg.shape, const2),
            pl.BlockSpec(cw.shape, const2),
            pl.BlockSpec(cb.shape, const2),
            pl.BlockSpec(wrt.shape, const2),
            pl.BlockSpec(br.shape, const2),
        ],
        out_specs=[
            pl.BlockSpec((tm, d), lambda i: (prev(i), 0)),
            pl.BlockSpec((tm, d // 2), lambda i: (prev(i), 0)),
            pl.BlockSpec((1, TOP_K, tm), lambda i: (prev(i), 0, 0)),
            pl.BlockSpec((tm, V7X_LANES), lambda i: (prev(i), 0)),
        ],
        out_shape=[
            jax.ShapeDtypeStruct((t, d), F32),
            jax.ShapeDtypeStruct((t, d // 2), U32),
            jax.ShapeDtypeStruct((nt, TOP_K, tm), I32),
            jax.ShapeDtypeStruct((t, V7X_LANES), F32),
        ],
        scratch_shapes=[pltpu.VMEM((tm + V7X_SUBLANES, bw), F32), pltpu.VMEM((tm, d), F32)],
        compiler_params=_params(min(vmem, V7X_VMEM_BYTES - 6 * 1024 * 1024), ("arbitrary",)),
        name="mix",
    )(xt, xt, mod3, mod3, g1, g2, win, wout, ws, bst, vg, cw, cb, wrt, br)


def _dispatch_body(eid_ref, texp_ref, tab_ref, rank_scr, dest_v, dest_s, init_v, sem,
                   *, n_e, n_tiles, n_tok):
    nch, _, c = eid_ref.shape
    te = EXPERT_TILE
    r = lax.broadcasted_iota(I32, (c, c), 0)
    q = lax.broadcasted_iota(I32, (c, c), 1)
    before = (r < q).astype(BF16)
    e_iota = lax.broadcasted_iota(I32, (n_e, c), 0)

    init_v[...] = (lax.broadcasted_iota(I32, init_v.shape, 0) & (DUMP_TILES * te - 1)) + TOP_K * n_tok
    cp_i = pltpu.make_async_copy(init_v, tab_ref, sem.at[0])
    cp_i.start()

    def count_body(ch, carry):
        e2 = eid_ref[ch]
        ranks = []
        for k in range(TOP_K):
            oh = e_iota == e2[k:k + 1]
            ohf = oh.astype(F32)
            pref = jnp.dot(ohf.astype(BF16), before, preferred_element_type=F32)
            ranks.append(jnp.sum(jnp.where(oh, pref + carry, 0.0), axis=0, keepdims=True))
            carry = carry + jnp.sum(ohf, axis=1, keepdims=True)
        rank_scr[ch] = jnp.concatenate(ranks, axis=0)
        return carry

    counts = lax.fori_loop(0, nch, count_body, jnp.zeros((n_e, 1), F32))

    padded = jnp.floor((counts + float(te - 1)) / float(te)) * float(te)
    sub = lax.broadcasted_iota(I32, (n_e, V7X_LANES), 0)
    lane = lax.broadcasted_iota(I32, (n_e, V7X_LANES), 1)
    pstart_row = jnp.sum(jnp.where(sub < lane, padded, 0.0), axis=0, keepdims=True)
    pstart = jnp.sum(jnp.where(sub == lane, pstart_row, 0.0), axis=1, keepdims=True)
    pend = pstart + padded
    total = jnp.max(pend, axis=0, keepdims=True)
    last_e = jnp.max(jnp.where(counts > 0.0, sub[:, 0:1].astype(F32), -1.0), axis=0, keepdims=True)

    group = V7X_SUBLANES // TOP_K

    def dest_body(g, carry):
        rows = []
        for j in range(group):
            e2 = eid_ref[g * group + j]
            rk = rank_scr[g * group + j]
            for k in range(TOP_K):
                oh = e_iota == e2[k:k + 1]
                off = jnp.sum(jnp.where(oh, pstart, 0.0), axis=0, keepdims=True)
                rows.append(off + rk[k:k + 1])
        row0 = pl.multiple_of(g * V7X_SUBLANES, V7X_SUBLANES)
        dest_v[pl.ds(row0, V7X_SUBLANES), :] = jnp.concatenate(rows, axis=0).astype(I32)
        return carry

    lax.fori_loop(0, nch // group, dest_body, 0)
    cp_d = pltpu.make_async_copy(dest_v, dest_s, sem.at[1])
    cp_d.start()

    w = texp_ref.shape[1]
    tj = lax.broadcasted_iota(I32, (n_e, w), 1).astype(F32) * float(te)
    texp = jnp.sum((pend <= tj).astype(F32), axis=0, keepdims=True)
    tj1 = tj[0:1]
    texp = jnp.where(tj1 < total, jnp.minimum(texp, float(n_e - 1)), last_e)
    lane_w = lax.broadcasted_iota(I32, (1, w), 1)
    texp = jnp.where(lane_w == n_tiles, total / float(te), texp)
    texp_ref[...] = texp.astype(I32)

    cp_i.wait()
    cp_d.wait()

    def row_body(rw, carry):
        a0 = (rw % TOP_K) * n_tok + (rw // TOP_K) * c
        for col in range(c):
            tab_ref[dest_s[rw, col]] = a0 + col
        return carry

    lax.fori_loop(0, nch * TOP_K, row_body, 0)


def _dispatch(eid, n_e, n_tiles, n_tok):
    nch, k, c = eid.shape
    assert c == EXPERT_TILE and EXPERT_TILE & (EXPERT_TILE - 1) == 0
    body = functools.partial(_dispatch_body, n_e=n_e, n_tiles=n_tiles, n_tok=n_tok)
    w = 2 * V7X_LANES
    assert n_tiles < w
    rows = n_tiles + V7X_SUBLANES
    rows += (-rows) % 4
    smem = pl.BlockSpec(memory_space=pltpu.SMEM)
    return pl.pallas_call(
        body,
        out_specs=[pl.BlockSpec(memory_space=pltpu.VMEM), smem],
        out_shape=[
            jax.ShapeDtypeStruct((1, w), I32),
            jax.ShapeDtypeStruct((rows * EXPERT_TILE,), I32),
        ],
        scratch_shapes=[
            pltpu.VMEM((nch, k, c), F32),
            pltpu.VMEM((nch * k, c), I32),
            pltpu.SMEM((nch * k, c), I32),
            pltpu.VMEM((rows * EXPERT_TILE,), I32),
            pltpu.SemaphoreType.DMA((2,)),
        ],
        compiler_params=_params(16 * 1024 * 1024),
        name="dispatch",
    )(eid)


def _expert_body(tab_ref, texp_ref, h2_hbm, wgu_hbm, wd_hbm, y2_hbm,
                 x0, x1, x2, y0, y1, y2, wgu_f, wd_f, wgu_s, wd_s, gsem, ssem, wsem, zsem, *, n_tiles):
    te = EXPERT_TILE
    ring = EXPERT_RING
    de = wd_s.shape[0]
    n_tok = h2_hbm.shape[0]
    xs, ys = (x0, x1, x2), (y0, y1, y2)
    p = pl.program_id(0)
    n_used = texp_ref[n_tiles]
    t0 = ring * p

    def gather(tile, buf, sem):
        for r in range(te):
            tok = tab_ref[tile * te + r] & (n_tok - 1)
            pltpu.make_async_copy(h2_hbm.at[pl.ds(tok, 1), :], buf.at[pl.ds(r, 1), :], sem).start()

    def scatter(tile, buf, sem):
        for r in range(te):
            row = tab_ref[tile * te + r]
            pltpu.make_async_copy(buf.at[pl.ds(r, 1), :], y2_hbm.at[pl.ds(row, 1), :], sem).start(priority=r % 2)

    def wait_rows(buf, sem):
        pltpu.make_async_copy(h2_hbm.at[pl.ds(0, te), :], buf, sem).wait()

    def weight_copies(e):
        return (pltpu.make_async_copy(wgu_hbm.at[e], wgu_f, wsem.at[0]),
                pltpu.make_async_copy(wd_hbm.at[e], wd_f, wsem.at[1]))

    def dump_fill(h):
        rows = pl.ds(TOP_K * n_tok + h * te, te)
        return pltpu.make_async_copy(ys[ring - 1], y2_hbm.at[rows, :], zsem)

    def switch_weights(tile):
        e = texp_ref[tile]
        first = jnp.logical_or(tile == 0, texp_ref[jnp.maximum(tile - 1, 0)] != e)

        @pl.when(jnp.logical_and(first, tile < n_used))
        def _():
            @pl.when(tile == 0)
            def _():
                for cp in weight_copies(e):
                    cp.start(priority=1)

            for cp in weight_copies(e):
                cp.wait()
            wgu_s[...] = wgu_f[...].astype(BF16)
            wd_s[...] = wd_f[...].astype(BF16)
            nxt = lax.while_loop(
                lambda k: jnp.logical_and(k < n_used, texp_ref[jnp.minimum(k, n_tiles - 1)] == e),
                lambda k: k + 1, tile + 1)

            @pl.when(nxt < n_used)
            def _():
                for cp in weight_copies(texp_ref[jnp.minimum(nxt, n_tiles - 1)]):
                    cp.start(priority=1)

    def compute(xbuf, ybuf):
        xb16 = _unpack_bf16_pairs(xbuf[...]).astype(BF16)
        gu = jnp.dot(xb16, wgu_s[...], preferred_element_type=F32)
        a = (jax.nn.silu(gu[:, :de]) * gu[:, de:]).astype(BF16)
        ybuf[...] = _pack_bf16_pairs(jnp.dot(a, wd_s[...], preferred_element_type=F32))

    @pl.when(t0 < n_used)
    def _():
        @pl.when(p == 0)
        def _():
            ys[ring - 1][...] = jnp.zeros(ys[ring - 1].shape, U32)
            for h in range(DUMP_TILES):
                dump_fill(h).start()
            for h in range(DUMP_TILES):
                dump_fill(h).wait()
            for m in range(ring - 1):
                gather(m, xs[m], gsem.at[m])

        for m in range(ring):
            tile = t0 + m
            nm = (m + ring - 1) % ring
            switch_weights(tile)
            wait_rows(xs[m], gsem.at[m])
            if m == ring - 1:
                wait_rows(ys[m], ssem.at[m])
            else:
                @pl.when(p > 0)
                def _(m=m):
                    wait_rows(ys[m], ssem.at[m])
            gather(tile + ring - 1, xs[nm], gsem.at[nm])
            scatter(jnp.where(tile == 0, n_tiles, tile - 1), ys[nm], ssem.at[nm])
            compute(xs[m], ys[m])

        @pl.when(t0 + ring >= n_used)
        def _():
            for m in range(ring - 1):
                wait_rows(ys[m], ssem.at[m])
            scatter(t0 + ring - 1, ys[ring - 1], ssem.at[ring - 1])
            wait_rows(ys[ring - 1], ssem.at[ring - 1])
            for m in range(ring - 1):
                wait_rows(xs[m], gsem.at[m])


def _experts(tab, texp, h2p, wgu, wd, n_tiles):
    t, dp = h2p.shape
    te = EXPERT_TILE
    _, d, n_gu = wgu.shape
    de = wd.shape[1]
    assert n_tiles % EXPERT_RING == 0 and t & (t - 1) == 0
    body = functools.partial(_expert_body, n_tiles=n_tiles)
    vmem = ((d * n_gu + de * d) * (4 + 2) + 2 * EXPERT_RING * te * dp * 4 + te * (n_gu + 3 * d) * 4
            + 4 * 1024 * 1024)
    hbm = pl.BlockSpec(memory_space=pl.ANY)
    row_buf = pltpu.VMEM((te, dp), U32)
    return pl.pallas_call(
        body,
        grid_spec=pltpu.PrefetchScalarGridSpec(
            num_scalar_prefetch=2,
            grid=(n_tiles // EXPERT_RING,),
            in_specs=[hbm, hbm, hbm],
            out_specs=hbm,
            scratch_shapes=[row_buf] * (2 * EXPERT_RING) + [
                pltpu.VMEM((d, n_gu), F32), pltpu.VMEM((de, d), F32),
                pltpu.VMEM((d, n_gu), BF16), pltpu.VMEM((de, d), BF16),
                pltpu.SemaphoreType.DMA((EXPERT_RING,)), pltpu.SemaphoreType.DMA((EXPERT_RING,)),
                pltpu.SemaphoreType.DMA((2,)), pltpu.SemaphoreType.DMA,
            ],
        ),
        out_shape=jax.ShapeDtypeStruct((TOP_K * t + DUMP_TILES * te, dp), U32),
        compiler_params=_params(vmem, ("arbitrary",)),
        name="experts",
    )(tab, texp, h2p, wgu, wd)


def _combine_body(x1_ref, y0_ref, y1_ref, gcol_ref, mod_ref, modf_ref, gf_ref, o_ref):
    d = x1_ref.shape[1]
    gate2 = mod_ref[0][:, 5 * d:6 * d]
    modf = modf_ref[0]
    shift_f, scale_f = modf[:, 0:d], modf[:, d:2 * d]
    y0 = _unpack_bf16_pairs(y0_ref[...])
    y1 = _unpack_bf16_pairs(y1_ref[...])
    ffn = y0 * gcol_ref[:, 0:1] + y1 * gcol_ref[:, 1:2]
    x2 = x1_ref[...] + gate2 * ffn
    o_ref[...] = _rms(x2, gf_ref[...]) * (1.0 + scale_f) + shift_f


def _combine(x1, y2, gcol, mod3, modf3, gf, seq):
    t, d = x1.shape
    tm = COMBINE_TILE
    per_seq = seq // tm
    return pl.pallas_call(
        _combine_body,
        grid=(t // tm,),
        in_specs=[
            pl.BlockSpec((tm, d), lambda i: (i, 0)),
            pl.BlockSpec((tm, d // 2), lambda i: (i, 0)),
            pl.BlockSpec((tm, d // 2), lambda i: (t // tm + i, 0)),
            pl.BlockSpec((tm, V7X_LANES), lambda i: (i, 0)),
            pl.BlockSpec((1, 1, mod3.shape[2]), lambda i: (i // per_seq, 0, 0)),
            pl.BlockSpec((1, 1, modf3.shape[2]), lambda i: (i // per_seq, 0, 0)),
            pl.BlockSpec((1, d), lambda i: (0, 0)),
        ],
        out_specs=pl.BlockSpec((tm, d), lambda i: (i, 0)),
        out_shape=jax.ShapeDtypeStruct((t, d), F32),
        compiler_params=_params(2 * tm * (3 * d + V7X_LANES) * 4 + 4 * tm * d * 4, ("arbitrary",)),
        name="combine",
    )(x1, y2, y2, gcol, mod3, modf3, gf)


def kernel(x, c, w_ada, b_ada, norm1_g, w_in, w_out, gmlp_w_s, gmlp_b_s, gmlp_v_gain, conv_w, conv_b,
           norm2_g, w_router_group, b_router_group, w_router_expert, b_router_expert, w_gate_up, w_down,
           w_ada_final, b_ada_final, norm_f_g):
    bsz, seq, d = x.shape
    depth = w_ada.shape[0]
    n_e = w_router_expert.shape[2]
    t = bsz * seq
    assert seq % TOKEN_TILE == 0 and TOKEN_TILE % CHUNK == 0 and seq % COMBINE_TILE == 0
    assert w_router_group.shape[2] == N_GROUPS and n_e + N_GROUPS <= ROUTER_ROWS
    n_tiles = (t * TOP_K + n_e * (EXPERT_TILE - 1)) // EXPERT_TILE
    n_tiles = -(-n_tiles // EXPERT_RING) * EXPERT_RING

    modf3 = _ada(c, w_ada_final, b_ada_final).reshape(bsz, 1, 2 * d)
    xt = x.reshape(t, d)
    for l in range(depth):
        mod3 = _ada(c, w_ada[l], b_ada[l]).reshape(bsz, 1, -1)
        wr = jnp.concatenate([w_router_expert[l], w_router_group[l]], axis=1)
        wrt = jnp.pad(wr, ((0, 0), (0, ROUTER_ROWS - wr.shape[1]))).T.astype(BF16)
        br = jnp.concatenate([b_router_expert[l], b_router_group[l]])
        br = jnp.pad(br, (0, ROUTER_ROWS - br.shape[0])).reshape(ROUTER_ROWS, 1)
        x1, h2, eid, gcol = _mix(
            xt, mod3, norm1_g[l].reshape(1, d), norm2_g[l].reshape(1, d),
            w_in[l].astype(BF16), w_out[l].astype(BF16), gmlp_w_s[l], gmlp_b_s[l].T,
            gmlp_v_gain[l].reshape(1, -1), conv_w[l], conv_b[l].reshape(1, -1), wrt, br, seq)
        texp, tab = _dispatch(eid, n_e, n_tiles, t)
        y2 = _experts(tab, texp.reshape(-1), h2, w_gate_up[l], w_down[l], n_tiles)
        assert depth == 1
        xt = _combine(x1, y2, gcol, mod3, modf3, norm_f_g.reshape(1, d), seq)
    return xt.reshape(bsz, seq, d)
```

```python
import functools

import jax
import jax.numpy as jnp
from jax import lax
from jax.experimental import pallas as pl
from jax.experimental.pallas import tpu as pltpu

F32 = jnp.float32
BF16 = jnp.bfloat16
I32 = jnp.int32
U32 = jnp.uint32

A_HEADS = 8
CHUNK = 128
N_GROUPS = 4
TOP_K = 2
EPS = 1e-6

V7X_LANES = 128
V7X_SUBLANES = 8
V7X_VMEM_BYTES = 64 * 1024 * 1024

TOKEN_TILE = 256
EXPERT_TILE = 256
COMBINE_TILE = 512
ROUTER_ROWS = 40
EXPERT_RING = 3
DUMP_TILES = 2


def _rms(x, g):
    y = x * lax.rsqrt(jnp.mean(x * x, axis=-1, keepdims=True) + EPS)
    return y * g


def _pack_bf16_pairs(x):
    n = x.shape[1] // 2
    lo = lax.bitcast_convert_type(x[:, :n].astype(BF16).astype(F32), U32)
    hi = lax.bitcast_convert_type(x[:, n:].astype(BF16).astype(F32), U32)
    return hi | (lo >> 16)


def _unpack_bf16_pairs(w):
    lo = lax.bitcast_convert_type(w << 16, F32)
    hi = lax.bitcast_convert_type(w & jnp.uint32(0xFFFF0000), F32)
    return jnp.concatenate([lo, hi], axis=1)


def _zero_after(anchor, n):
    bits = lax.bitcast_convert_type(anchor, U32)
    zero = lax.bitcast_convert_type((bits >> 16) >> 16, F32)
    return jnp.concatenate([zero] * (n // anchor.shape[1]), axis=1)


def _params(vmem_bytes, semantics=None):
    kw = dict(vmem_limit_bytes=int(vmem_bytes))
    if semantics is not None:
        kw["dimension_semantics"] = semantics
    return pltpu.CompilerParams(**kw)


def _ada_body(c_ref, w_ref, b_ref, o_ref):
    ca = jax.nn.silu(c_ref[...]).astype(BF16)
    o_ref[...] = jnp.dot(ca, w_ref[...].astype(BF16), preferred_element_type=F32) + b_ref[...]


def _ada(c, w, b, tn=1024):
    bsz, d = c.shape
    n = w.shape[1]
    return pl.pallas_call(
        _ada_body,
        grid=(n // tn,),
        in_specs=[
            pl.BlockSpec((bsz, d), lambda j: (0, 0)),
            pl.BlockSpec((d, tn), lambda j: (0, j)),
            pl.BlockSpec((1, tn), lambda j: (0, j)),
        ],
        out_specs=pl.BlockSpec((bsz, tn), lambda j: (0, j)),
        out_shape=jax.ShapeDtypeStruct((bsz, n), F32),
        compiler_params=_params(4 * d * tn * 4, ("arbitrary",)),
        name="ada",
    )(c, w, b.reshape(1, n))


def _route(lt, tm):
    n_e = 32
    epg = n_e // N_GROUPS
    row = lax.broadcasted_iota(I32, (V7X_SUBLANES, tm), 0).astype(F32)
    gl = lt[n_e:n_e + V7X_SUBLANES]
    gvalid = row < float(N_GROUPS)
    glm = jnp.where(gvalid, gl, -jnp.inf)
    gmax = jnp.max(glm, axis=0, keepdims=True)
    garg = jnp.min(jnp.where(glm == gmax, row, float(V7X_SUBLANES)), axis=0, keepdims=True)
    gsum = jnp.sum(jnp.where(gvalid, jnp.exp(gl - gmax), 0.0), axis=0, keepdims=True)
    p_grp = 1.0 / gsum
    es = lt[0:epg]
    for g in range(1, N_GROUPS):
        es = jnp.where(garg == float(g), lt[g * epg:(g + 1) * epg], es)
    m1 = jnp.max(es, axis=0, keepdims=True)
    i1 = jnp.min(jnp.where(es == m1, row, float(epg)), axis=0, keepdims=True)
    es2 = jnp.where(row == i1, -jnp.inf, es)
    m2 = jnp.max(es2, axis=0, keepdims=True)
    i2 = jnp.min(jnp.where(es2 == m2, row, float(epg)), axis=0, keepdims=True)
    z = jnp.exp(m2 - m1)
    den = 1.0 + z
    g0 = p_grp / den
    g1 = p_grp * z / den
    e0 = garg * float(epg) + i1
    e1 = garg * float(epg) + i2
    zero = jnp.zeros_like(g0)
    return jnp.concatenate([g0, g1, e0, e1, zero, zero, zero, zero], axis=0)


def _mix_body(x_ref, xp_ref, mod_ref, modp_ref, g1_ref, g2_ref, win_ref, wout_ref, ws_ref, bst_ref, vg_ref,
              cw_ref, cb_ref, wrt_ref, br_ref,
              x1_ref, eid_ref, gcol_ref, h2_hbm, pre_scr, mix_scr, h2_scr, h2_sem, *, tiles_per_seq):
    tm, d = x_ref.shape
    aw = vg_ref.shape[1]
    bw = cb_ref.shape[1]
    hd = aw // A_HEADS
    nch = tm // CHUNK
    i = pl.program_id(0)
    groups = tm // V7X_SUBLANES

    slot = i % 2

    def h2_copies(tile, sl):
        return [pltpu.make_async_copy(h2_scr.at[sl, :, s], h2_hbm.at[pl.ds(tile * groups, groups), :, s, :],
                                      h2_sem.at[sl])
                for s in range(V7X_SUBLANES)]

    @pl.when(i == 0)
    def _():
        mix_scr[...] = jnp.zeros(mix_scr.shape, F32)

    @pl.when(i > 2)
    def _():
        for cp in h2_copies(0, slot):
            cp.wait()

    x = x_ref[...]
    mod = mod_ref[0]
    shift1, scale1 = mod[:, 0:d], mod[:, d:2 * d]
    h = _rms(x, g1_ref[...]) * (1.0 + scale1) + shift1
    hb = h.astype(BF16)

    uv = jnp.dot(hb, win_ref[:, 0:2 * aw], preferred_element_type=F32)

    modp = modp_ref[0]
    gate1p, shift2p, scale2p = modp[:, 2 * d:3 * d], modp[:, 3 * d:4 * d], modp[:, 4 * d:5 * d]
    gate1p = gate1p + _zero_after(uv[0:1, 0:V7X_LANES], d)
    x1 = xp_ref[...] + gate1p * mix_scr[...]
    x1_ref[...] = x1
    h2 = _rms(x1, g2_ref[...]) * (1.0 + scale2p) + shift2p
    words = _pack_bf16_pairs(h2)
    for s in range(V7X_SUBLANES):
        h2_scr[slot, :, s] = words[:, s * V7X_LANES:(s + 1) * V7X_LANES].reshape(
            groups, V7X_SUBLANES, V7X_LANES)
    lt = lax.dot_general(wrt_ref[...], h2.astype(BF16), (((1,), (1,)), ((), ())),
                         preferred_element_type=F32) + br_ref[...]
    slab = _route(lt, tm)
    eid_ref[0] = slab[2:4].astype(I32)
    wide = jnp.concatenate([slab, jnp.zeros((V7X_LANES - V7X_SUBLANES, tm), F32)], axis=0)
    gcol_ref[...] = wide.T

    uv = jax.nn.gelu(uv)
    u, v = uv[:, :aw], uv[:, aw:]
    tq_r = lax.broadcasted_iota(I32, (CHUNK, CHUNK), 0)
    tq_c = lax.broadcasted_iota(I32, (CHUNK, CHUNK), 1)
    causal = tq_c <= tq_r
    ya = []
    for hh in range(A_HEADS):
        sl = slice(hh * hd, (hh + 1) * hd)
        vh = _rms(v[:, sl], vg_ref[:, sl]).astype(BF16)
        rhs = jnp.concatenate([vh[c * CHUNK:(c + 1) * CHUNK] for c in range(nch)], axis=1)
        w = jnp.where(causal, ws_ref[hh], 0.0).astype(BF16)
        zs = jnp.dot(w, rhs, preferred_element_type=F32) + bst_ref[:, hh:hh + 1]
        zs = jnp.concatenate([zs[:, c * hd:(c + 1) * hd] for c in range(nch)], axis=0)
        ya.append(u[:, sl] * zs)

    bcx = jnp.dot(hb, win_ref[:, 2 * aw:], preferred_element_type=F32)
    bg, cg, xin = bcx[:, :bw], bcx[:, bw:2 * bw], bcx[:, 2 * bw:]
    pre = cg * xin

    @pl.when(i % tiles_per_seq == 0)
    def _():
        pre_scr[0:V7X_SUBLANES, :] = jnp.zeros((V7X_SUBLANES, bw), F32)

    pre_scr[V7X_SUBLANES:V7X_SUBLANES + tm, :] = pre
    p1 = pre_scr[V7X_SUBLANES - 1:V7X_SUBLANES - 1 + tm, :]
    p2 = pre_scr[V7X_SUBLANES - 2:V7X_SUBLANES - 2 + tm, :]
    conv = cw_ref[0:1, :] * p2 + cw_ref[1:2, :] * p1 + cw_ref[2:3, :] * pre + cb_ref[...]
    yb = bg * conv
    pre_scr[0:V7X_SUBLANES, :] = pre_scr[tm:tm + V7X_SUBLANES, :]

    y = jnp.concatenate(ya + [yb], axis=1).astype(BF16)
    mix_scr[...] = jnp.dot(y, wout_ref[...], preferred_element_type=F32)

    @pl.when(i > 0)
    def _():
        for cp in h2_copies(i - 1, slot):
            cp.start()

    @pl.when(i == pl.num_programs(0) - 1)
    def _():
        for sl in range(2):
            for cp in h2_copies(0, sl):
                cp.wait()


def _mix(xt, mod3, g1, g2, win, wout, ws, bst, vg, cw, cb, wrt, br, seq):
    t, d = xt.shape
    tm = TOKEN_TILE
    nt = t // tm
    per_seq = seq // tm
    in_cols = win.shape[1]
    bw = cb.shape[1]
    words_per_row = d // 2
    assert words_per_row == V7X_SUBLANES * V7X_LANES
    const2 = lambda i: (0, 0)
    single = dict(pipeline_mode=pl.Buffered(1))
    body = functools.partial(_mix_body, tiles_per_seq=per_seq)
    cur = lambda i: jnp.minimum(i, nt - 1)
    prev = lambda i: jnp.maximum(i - 1, 0)
    vmem = (2 * d * (in_cols + d)
            + 3 * 2 * tm * d * 4
            + 2 * tm * d * 2
            + (tm + V7X_SUBLANES) * bw * 4 + tm * d * 4
            + tm * (in_cols + 4 * d) * 4
            + 8 * 1024 * 1024)
    return pl.pallas_call(
        body,
        grid=(nt + 1,),
        in_specs=[
            pl.BlockSpec((tm, d), lambda i: (cur(i), 0)),
            pl.BlockSpec((tm, d), lambda i: (prev(i), 0)),
            pl.BlockSpec((1, 1, mod3.shape[2]), lambda i: (cur(i) // per_seq, 0, 0)),
            pl.BlockSpec((1, 1, mod3.shape[2]), lambda i: (prev(i) // per_seq, 0, 0)),
            pl.BlockSpec((1, d), const2),
            pl.BlockSpec((1, d), const2),
            pl.BlockSpec(win.shape, const2, **single),
            pl.BlockSpec(wout.shape, const2, **single),
            pl.BlockSpec(ws.shape, lambda i: (0, 0, 0)),
            pl.BlockSpec(bst.shape, const2),
            pl.BlockSpec(vg.shape, const2),
            pl.BlockSpec(cw.shape, const2),
            pl.BlockSpec(cb.shape, const2),
            pl.BlockSpec(wrt.shape, const2),
            pl.BlockSpec(br.shape, const2),
        ],
        out_specs=[
            pl.BlockSpec((tm, d), lambda i: (prev(i), 0)),
            pl.BlockSpec((1, TOP_K, tm), lambda i: (prev(i), 0, 0)),
            pl.BlockSpec((tm, V7X_LANES), lambda i: (prev(i), 0)),
            pl.BlockSpec(memory_space=pl.ANY),
        ],
        out_shape=[
            jax.ShapeDtypeStruct((t, d), F32),
            jax.ShapeDtypeStruct((nt, TOP_K, tm), I32),
            jax.ShapeDtypeStruct((t, V7X_LANES), F32),
            jax.ShapeDtypeStruct((t // V7X_SUBLANES, V7X_SUBLANES, words_per_row // V7X_LANES, V7X_LANES), U32),
        ],
        scratch_shapes=[pltpu.VMEM((tm + V7X_SUBLANES, bw), F32), pltpu.VMEM((tm, d), F32),
                        pltpu.VMEM((2, tm // V7X_SUBLANES, words_per_row // V7X_LANES, V7X_SUBLANES, V7X_LANES),
                                   U32),
                        pltpu.SemaphoreType.DMA((2,))],
        compiler_params=_params(min(vmem, V7X_VMEM_BYTES - 6 * 1024 * 1024), ("arbitrary",)),
        name="mix",
    )(xt, xt, mod3, mod3, g1, g2, win, wout, ws, bst, vg, cw, cb, wrt, br)


def _dispatch_body(eid_ref, texp_ref, tab_ref, rank_scr, dest_v, dest_s, init_v, sem,
                   *, n_e, n_tiles, n_tok):
    nch, _, c = eid_ref.shape
    te = EXPERT_TILE
    r = lax.broadcasted_iota(I32, (c, c), 0)
    q = lax.broadcasted_iota(I32, (c, c), 1)
    before = (r < q).astype(BF16)
    e_iota = lax.broadcasted_iota(I32, (n_e, c), 0)

    init_v[...] = (lax.broadcasted_iota(I32, init_v.shape, 0) & (DUMP_TILES * te - 1)) + TOP_K * n_tok
    cp_i = pltpu.make_async_copy(init_v, tab_ref, sem.at[0])
    cp_i.start()

    def count_body(ch, carry):
        e2 = eid_ref[ch]
        ranks = []
        for k in range(TOP_K):
            oh = e_iota == e2[k:k + 1]
            ohf = oh.astype(F32)
            pref = jnp.dot(ohf.astype(BF16), before, preferred_element_type=F32)
            ranks.append(jnp.sum(jnp.where(oh, pref + carry, 0.0), axis=0, keepdims=True))
            carry = carry + jnp.sum(ohf, axis=1, keepdims=True)
        rank_scr[ch] = jnp.concatenate(ranks, axis=0)
        return carry

    counts = lax.fori_loop(0, nch, count_body, jnp.zeros((n_e, 1), F32))

    padded = jnp.floor((counts + float(te - 1)) / float(te)) * float(te)
    sub = lax.broadcasted_iota(I32, (n_e, V7X_LANES), 0)
    lane = lax.broadcasted_iota(I32, (n_e, V7X_LANES), 1)
    pstart_row = jnp.sum(jnp.where(sub < lane, padded, 0.0), axis=0, keepdims=True)
    pstart = jnp.sum(jnp.where(sub == lane, pstart_row, 0.0), axis=1, keepdims=True)
    pend = pstart + padded
    total = jnp.max(pend, axis=0, keepdims=True)
    last_e = jnp.max(jnp.where(counts > 0.0, sub[:, 0:1].astype(F32), -1.0), axis=0, keepdims=True)

    group = V7X_SUBLANES // TOP_K

    def dest_body(g, carry):
        rows = []
        for j in range(group):
            e2 = eid_ref[g * group + j]
            rk = rank_scr[g * group + j]
            for k in range(TOP_K):
                oh = e_iota == e2[k:k + 1]
                off = jnp.sum(jnp.where(oh, pstart, 0.0), axis=0, keepdims=True)
                rows.append(off + rk[k:k + 1])
        row0 = pl.multiple_of(g * V7X_SUBLANES, V7X_SUBLANES)
        dest_v[pl.ds(row0, V7X_SUBLANES), :] = jnp.concatenate(rows, axis=0).astype(I32)
        return carry

    lax.fori_loop(0, nch // group, dest_body, 0)
    cp_d = pltpu.make_async_copy(dest_v, dest_s, sem.at[1])
    cp_d.start()

    w = texp_ref.shape[1]
    tj = lax.broadcasted_iota(I32, (n_e, w), 1).astype(F32) * float(te)
    texp = jnp.sum((pend <= tj).astype(F32), axis=0, keepdims=True)
    tj1 = tj[0:1]
    texp = jnp.where(tj1 < total, jnp.minimum(texp, float(n_e - 1)), last_e)
    lane_w = lax.broadcasted_iota(I32, (1, w), 1)
    texp = jnp.where(lane_w == n_tiles, total / float(te), texp)
    texp_ref[...] = texp.astype(I32)

    cp_i.wait()
    cp_d.wait()

    def row_body(rw, carry):
        a0 = (rw % TOP_K) * n_tok + (rw // TOP_K) * c
        for col in range(c):
            tab_ref[dest_s[rw, col]] = a0 + col
        return carry

    lax.fori_loop(0, nch * TOP_K, row_body, 0)


def _dispatch(eid, n_e, n_tiles, n_tok):
    nch, k, c = eid.shape
    assert c == EXPERT_TILE and EXPERT_TILE & (EXPERT_TILE - 1) == 0
    body = functools.partial(_dispatch_body, n_e=n_e, n_tiles=n_tiles, n_tok=n_tok)
    w = 2 * V7X_LANES
    assert n_tiles < w
    rows = n_tiles + V7X_SUBLANES
    rows += (-rows) % 4
    smem = pl.BlockSpec(memory_space=pltpu.SMEM)
    return pl.pallas_call(
        body,
        out_specs=[pl.BlockSpec(memory_space=pltpu.VMEM), smem],
        out_shape=[
            jax.ShapeDtypeStruct((1, w), I32),
            jax.ShapeDtypeStruct((rows * EXPERT_TILE,), I32),
        ],
        scratch_shapes=[
            pltpu.VMEM((nch, k, c), F32),
            pltpu.VMEM((nch * k, c), I32),
            pltpu.SMEM((nch * k, c), I32),
            pltpu.VMEM((rows * EXPERT_TILE,), I32),
            pltpu.SemaphoreType.DMA((2,)),
        ],
        compiler_params=_params(16 * 1024 * 1024),
        name="dispatch",
    )(eid)


def _expert_body(tab_ref, texp_ref, h2_hbm, wgu_hbm, wd_hbm, y2_hbm,
                 x0, x1, x2, y0, y1, y2, wgu_f, wd_f, wgu_s, wd_s, gsem, ssem, wsem, zsem, *, n_tiles):
    te = EXPERT_TILE
    ring = EXPERT_RING
    de = wd_s.shape[0]
    n_tok = h2_hbm.shape[0]
    xs, ys = (x0, x1, x2), (y0, y1, y2)
    p = pl.program_id(0)
    n_used = texp_ref[n_tiles]
    t0 = ring * p

    def gather(tile, buf, sem):
        for r in range(te):
            tok = tab_ref[tile * te + r] & (n_tok - 1)
            pltpu.make_async_copy(h2_hbm.at[tok], buf.at[r // V7X_SUBLANES, :, r % V7X_SUBLANES, :], sem).start()

    def scatter(tile, buf, sem):
        for r in range(te):
            row = tab_ref[tile * te + r]
            pltpu.make_async_copy(buf.at[pl.ds(r, 1), :], y2_hbm.at[pl.ds(row, 1), :], sem).start()

    def wait_rows(buf, sem):
        pltpu.make_async_copy(buf, buf, sem).wait()

    def weight_copies(e):
        return (pltpu.make_async_copy(wgu_hbm.at[e], wgu_f, wsem.at[0]),
                pltpu.make_async_copy(wd_hbm.at[e], wd_f, wsem.at[1]))

    def dump_fill(h):
        rows = pl.ds(TOP_K * n_tok + h * te, te)
        return pltpu.make_async_copy(ys[ring - 1], y2_hbm.at[rows, :], zsem)

    def switch_weights(tile):
        e = texp_ref[tile]
        first = jnp.logical_or(tile == 0, texp_ref[jnp.maximum(tile - 1, 0)] != e)

        @pl.when(jnp.logical_and(first, tile < n_used))
        def _():
            @pl.when(tile == 0)
            def _():
                for cp in weight_copies(e):
                    cp.start(priority=1)

            for cp in weight_copies(e):
                cp.wait()
            wgu_s[...] = wgu_f[...].astype(BF16)
            wd_s[...] = wd_f[...].astype(BF16)
            nxt = lax.while_loop(
                lambda k: jnp.logical_and(k < n_used, texp_ref[jnp.minimum(k, n_tiles - 1)] == e),
                lambda k: k + 1, tile + 1)

            @pl.when(nxt < n_used)
            def _():
                for cp in weight_copies(texp_ref[jnp.minimum(nxt, n_tiles - 1)]):
                    cp.start(priority=1)

    def compute(xbuf, ybuf):
        words = jnp.concatenate([xbuf[:, s].reshape(te, V7X_LANES) for s in range(V7X_SUBLANES)], axis=1)
        xb16 = _unpack_bf16_pairs(words).astype(BF16)
        gu = jnp.dot(xb16, wgu_s[...], preferred_element_type=F32)
        a = (jax.nn.silu(gu[:, :de]) * gu[:, de:]).astype(BF16)
        ybuf[...] = _pack_bf16_pairs(jnp.dot(a, wd_s[...], preferred_element_type=F32))

    @pl.when(t0 < n_used)
    def _():
        @pl.when(p == 0)
        def _():
            ys[ring - 1][...] = jnp.zeros(ys[ring - 1].shape, U32)
            for h in range(DUMP_TILES):
                dump_fill(h).start()
            for h in range(DUMP_TILES):
                dump_fill(h).wait()
            for m in range(ring - 1):
                gather(m, xs[m], gsem.at[m])

        for m in range(ring):
            tile = t0 + m
            nm = (m + ring - 1) % ring
            switch_weights(tile)
            wait_rows(xs[m], gsem.at[m])
            if m == ring - 1:
                wait_rows(ys[m], ssem.at[m])
            else:
                @pl.when(p > 0)
                def _(m=m):
                    wait_rows(ys[m], ssem.at[m])
            gather(tile + ring - 1, xs[nm], gsem.at[nm])
            scatter(jnp.where(tile == 0, n_tiles, tile - 1), ys[nm], ssem.at[nm])
            compute(xs[m], ys[m])

        @pl.when(t0 + ring >= n_used)
        def _():
            for m in range(ring - 1):
                wait_rows(ys[m], ssem.at[m])
            scatter(t0 + ring - 1, ys[ring - 1], ssem.at[ring - 1])
            wait_rows(ys[ring - 1], ssem.at[ring - 1])
            for m in range(ring - 1):
                wait_rows(xs[m], gsem.at[m])


def _experts(tab, texp, h2p, wgu, wd, n_tiles):
    t = h2p.shape[0]
    dp = h2p.shape[1] * h2p.shape[2]
    te = EXPERT_TILE
    _, d, n_gu = wgu.shape
    de = wd.shape[1]
    assert n_tiles % EXPERT_RING == 0 and t & (t - 1) == 0
    body = functools.partial(_expert_body, n_tiles=n_tiles)
    vmem = ((d * n_gu + de * d) * (4 + 2) + 2 * EXPERT_RING * te * dp * 4 + te * (n_gu + 3 * d) * 4
            + 4 * 1024 * 1024)
    hbm = pl.BlockSpec(memory_space=pl.ANY)
    row_buf = pltpu.VMEM((te, dp), U32)
    tile_buf = pltpu.VMEM((te // V7X_SUBLANES,) + h2p.shape[1:2] + (V7X_SUBLANES, V7X_LANES), U32)
    return pl.pallas_call(
        body,
        grid_spec=pltpu.PrefetchScalarGridSpec(
            num_scalar_prefetch=2,
            grid=(n_tiles // EXPERT_RING,),
            in_specs=[hbm, hbm, hbm],
            out_specs=hbm,
            scratch_shapes=[tile_buf] * EXPERT_RING + [row_buf] * EXPERT_RING + [
                pltpu.VMEM((d, n_gu), F32), pltpu.VMEM((de, d), F32),
                pltpu.VMEM((d, n_gu), BF16), pltpu.VMEM((de, d), BF16),
                pltpu.SemaphoreType.DMA((EXPERT_RING,)), pltpu.SemaphoreType.DMA((EXPERT_RING,)),
                pltpu.SemaphoreType.DMA((2,)), pltpu.SemaphoreType.DMA,
            ],
        ),
        out_shape=jax.ShapeDtypeStruct((TOP_K * t + DUMP_TILES * te, dp), U32),
        compiler_params=_params(vmem, ("arbitrary",)),
        name="experts",
    )(tab, texp, h2p, wgu, wd)


def _combine_body(x1_ref, y0_ref, y1_ref, gcol_ref, mod_ref, modf_ref, gf_ref, o_ref):
    d = x1_ref.shape[1]
    gate2 = mod_ref[0][:, 5 * d:6 * d]
    modf = modf_ref[0]
    shift_f, scale_f = modf[:, 0:d], modf[:, d:2 * d]
    y0 = _unpack_bf16_pairs(y0_ref[...])
    y1 = _unpack_bf16_pairs(y1_ref[...])
    ffn = y0 * gcol_ref[:, 0:1] + y1 * gcol_ref[:, 1:2]
    x2 = x1_ref[...] + gate2 * ffn
    o_ref[...] = _rms(x2, gf_ref[...]) * (1.0 + scale_f) + shift_f


def _combine(x1, y2, gcol, mod3, modf3, gf, seq):
    t, d = x1.shape
    tm = COMBINE_TILE
    per_seq = seq // tm
    return pl.pallas_call(
        _combine_body,
        grid=(t // tm,),
        in_specs=[
            pl.BlockSpec((tm, d), lambda i: (i, 0)),
            pl.BlockSpec((tm, d // 2), lambda i: (i, 0)),
            pl.BlockSpec((tm, d // 2), lambda i: (t // tm + i, 0)),
            pl.BlockSpec((tm, V7X_LANES), lambda i: (i, 0)),
            pl.BlockSpec((1, 1, mod3.shape[2]), lambda i: (i // per_seq, 0, 0)),
            pl.BlockSpec((1, 1, modf3.shape[2]), lambda i: (i // per_seq, 0, 0)),
            pl.BlockSpec((1, d), lambda i: (0, 0)),
        ],
        out_specs=pl.BlockSpec((tm, d), lambda i: (i, 0)),
        out_shape=jax.ShapeDtypeStruct((t, d), F32),
        compiler_params=_params(2 * tm * (3 * d + V7X_LANES) * 4 + 4 * tm * d * 4, ("arbitrary",)),
        name="combine",
    )(x1, y2, y2, gcol, mod3, modf3, gf)


def kernel(x, c, w_ada, b_ada, norm1_g, w_in, w_out, gmlp_w_s, gmlp_b_s, gmlp_v_gain, conv_w, conv_b,
           norm2_g, w_router_group, b_router_group, w_router_expert, b_router_expert, w_gate_up, w_down,
           w_ada_final, b_ada_final, norm_f_g):
    bsz, seq, d = x.shape
    depth = w_ada.shape[0]
    n_e = w_router_expert.shape[2]
    t = bsz * seq
    assert seq % TOKEN_TILE == 0 and TOKEN_TILE % CHUNK == 0 and seq % COMBINE_TILE == 0
    assert w_router_group.shape[2] == N_GROUPS and n_e + N_GROUPS <= ROUTER_ROWS
    n_tiles = (t * TOP_K + n_e * (EXPERT_TILE - 1)) // EXPERT_TILE
    n_tiles = -(-n_tiles // EXPERT_RING) * EXPERT_RING

    modf3 = _ada(c, w_ada_final, b_ada_final).reshape(bsz, 1, 2 * d)
    xt = x.reshape(t, d)
    for l in range(depth):
        mod3 = _ada(c, w_ada[l], b_ada[l]).reshape(bsz, 1, -1)
        wr = jnp.concatenate([w_router_expert[l], w_router_group[l]], axis=1)
        wrt = jnp.pad(wr, ((0, 0), (0, ROUTER_ROWS - wr.shape[1]))).T.astype(BF16)
        br = jnp.concatenate([b_router_expert[l], b_router_group[l]])
        br = jnp.pad(br, (0, ROUTER_ROWS - br.shape[0])).reshape(ROUTER_ROWS, 1)
        x1, eid, gcol, h2 = _mix(
            xt, mod3, norm1_g[l].reshape(1, d), norm2_g[l].reshape(1, d),
            w_in[l].astype(BF16), w_out[l].astype(BF16), gmlp_w_s[l], gmlp_b_s[l].T,
            gmlp_v_gain[l].reshape(1, -1), conv_w[l], conv_b[l].reshape(1, -1), wrt, br, seq)
        texp, tab = _dispatch(eid, n_e, n_tiles, t)
        h2 = h2.reshape((t,) + h2.shape[2:])
        y2 = _experts(tab, texp.reshape(-1), h2, w_gate_up[l], w_down[l], n_tiles)
        assert depth == 1
        xt = _combine(x1, y2, gcol, mod3, modf3, norm_f_g.reshape(1, d), seq)
    return xt.reshape(bsz, seq, d)
```

```python
import functools

import jax
import jax.numpy as jnp
from jax import lax
from jax.experimental import pallas as pl
from jax.experimental.pallas import tpu as pltpu

F32 = jnp.float32
BF16 = jnp.bfloat16
I32 = jnp.int32
U32 = jnp.uint32

A_HEADS = 8
CHUNK = 128
N_GROUPS = 4
TOP_K = 2
EPS = 1e-6

V7X_LANES = 128
V7X_SUBLANES = 8
V7X_VMEM_BYTES = 64 * 1024 * 1024

TOKEN_TILE = 256
EXPERT_TILE = 256
COMBINE_TILE = 512
ROUTER_ROWS = 40
EXPERT_RING = 3
DUMP_TILES = 2


def _rms(x, g):
    y = x * lax.rsqrt(jnp.mean(x * x, axis=-1, keepdims=True) + EPS)
    return y * g


def _pack_bf16_pairs(x):
    n = x.shape[1] // 2
    lo = lax.bitcast_convert_type(x[:, :n].astype(BF16).astype(F32), U32)
    hi = lax.bitcast_convert_type(x[:, n:].astype(BF16).astype(F32), U32)
    return hi | (lo >> 16)


def _unpack_bf16_pairs(w):
    lo = lax.bitcast_convert_type(w << 16, F32)
    hi = lax.bitcast_convert_type(w & jnp.uint32(0xFFFF0000), F32)
    return jnp.concatenate([lo, hi], axis=1)


def _zero_after(anchor, n):
    bits = lax.bitcast_convert_type(anchor, U32)
    zero = lax.bitcast_convert_type((bits >> 16) >> 16, F32)
    return jnp.concatenate([zero] * (n // anchor.shape[1]), axis=1)


def _params(vmem_bytes, semantics=None):
    kw = dict(vmem_limit_bytes=int(vmem_bytes))
    if semantics is not None:
        kw["dimension_semantics"] = semantics
    return pltpu.CompilerParams(**kw)


def _ada_body(c_ref, w_ref, b_ref, o_ref):
    ca = jax.nn.silu(c_ref[...]).astype(BF16)
    o_ref[...] = jnp.dot(ca, w_ref[...].astype(BF16), preferred_element_type=F32) + b_ref[...]


def _ada(c, w, b, tn=1024):
    bsz, d = c.shape
    n = w.shape[1]
    return pl.pallas_call(
        _ada_body,
        grid=(n // tn,),
        in_specs=[
            pl.BlockSpec((bsz, d), lambda j: (0, 0)),
            pl.BlockSpec((d, tn), lambda j: (0, j)),
            pl.BlockSpec((1, tn), lambda j: (0, j)),
        ],
        out_specs=pl.BlockSpec((bsz, tn), lambda j: (0, j)),
        out_shape=jax.ShapeDtypeStruct((bsz, n), F32),
        compiler_params=_params(4 * d * tn * 4, ("arbitrary",)),
        name="ada",
    )(c, w, b.reshape(1, n))


def _route(lt, tm):
    n_e = 32
    epg = n_e // N_GROUPS
    row = lax.broadcasted_iota(I32, (V7X_SUBLANES, tm), 0).astype(F32)
    gl = lt[n_e:n_e + V7X_SUBLANES]
    gvalid = row < float(N_GROUPS)
    glm = jnp.where(gvalid, gl, -jnp.inf)
    gmax = jnp.max(glm, axis=0, keepdims=True)
    garg = jnp.min(jnp.where(glm == gmax, row, float(V7X_SUBLANES)), axis=0, keepdims=True)
    gsum = jnp.sum(jnp.where(gvalid, jnp.exp(gl - gmax), 0.0), axis=0, keepdims=True)
    p_grp = 1.0 / gsum
    es = lt[0:epg]
    for g in range(1, N_GROUPS):
        es = jnp.where(garg == float(g), lt[g * epg:(g + 1) * epg], es)
    m1 = jnp.max(es, axis=0, keepdims=True)
    i1 = jnp.min(jnp.where(es == m1, row, float(epg)), axis=0, keepdims=True)
    es2 = jnp.where(row == i1, -jnp.inf, es)
    m2 = jnp.max(es2, axis=0, keepdims=True)
    i2 = jnp.min(jnp.where(es2 == m2, row, float(epg)), axis=0, keepdims=True)
    z = jnp.exp(m2 - m1)
    den = 1.0 + z
    g0 = p_grp / den
    g1 = p_grp * z / den
    e0 = garg * float(epg) + i1
    e1 = garg * float(epg) + i2
    zero = jnp.zeros_like(g0)
    return jnp.concatenate([g0, g1, e0, e1, zero, zero, zero, zero], axis=0)


def _mix_body(x_ref, xp_ref, mod_ref, modp_ref, g1_ref, g2_ref, win_ref, wout_ref, ws_ref, bst_ref, vg_ref,
              cw_ref, cb_ref, wrt_ref, br_ref,
              x1_ref, eid_ref, gcol_ref, h2_hbm, pre_scr, mix_scr, h2_scr, h2_sem, *, tiles_per_seq):
    tm, d = x_ref.shape
    aw = vg_ref.shape[1]
    bw = cb_ref.shape[1]
    hd = aw // A_HEADS
    nch = tm // CHUNK
    i = pl.program_id(0)
    groups = tm // V7X_SUBLANES

    slot = i % 2

    def h2_copies(tile, sl):
        return [pltpu.make_async_copy(h2_scr.at[sl, :, s], h2_hbm.at[pl.ds(tile * groups, groups), :, s, :],
                                      h2_sem.at[sl])
                for s in range(V7X_SUBLANES)]

    @pl.when(i == 0)
    def _():
        mix_scr[...] = jnp.zeros(mix_scr.shape, F32)

    @pl.when(i > 2)
    def _():
        for cp in h2_copies(0, slot):
            cp.wait()

    x = x_ref[...]
    mod = mod_ref[0]
    shift1, scale1 = mod[:, 0:d], mod[:, d:2 * d]
    h = _rms(x, g1_ref[...]) * (1.0 + scale1) + shift1
    hb = h.astype(BF16)

    uv = jnp.dot(hb, win_ref[:, 0:2 * aw], preferred_element_type=F32)

    modp = modp_ref[0]
    gate1p, shift2p, scale2p = modp[:, 2 * d:3 * d], modp[:, 3 * d:4 * d], modp[:, 4 * d:5 * d]
    gate1p = gate1p + _zero_after(uv[0:1, 0:V7X_LANES], d)
    x1 = xp_ref[...] + gate1p * mix_scr[...]
    x1_ref[...] = x1
    h2 = _rms(x1, g2_ref[...]) * (1.0 + scale2p) + shift2p
    words = _pack_bf16_pairs(h2)
    for s in range(V7X_SUBLANES):
        h2_scr[slot, :, s] = words[:, s * V7X_LANES:(s + 1) * V7X_LANES].reshape(
            groups, V7X_SUBLANES, V7X_LANES)
    lt = lax.dot_general(wrt_ref[...], h2.astype(BF16), (((1,), (1,)), ((), ())),
                         preferred_element_type=F32) + br_ref[...]
    slab = _route(lt, tm)
    eid_ref[0] = slab[2:4].astype(I32)
    wide = jnp.concatenate([slab, jnp.zeros((V7X_LANES - V7X_SUBLANES, tm), F32)], axis=0)
    gcol_ref[...] = wide.T

    uv = jax.nn.gelu(uv)
    u, v = uv[:, :aw], uv[:, aw:]
    tq_r = lax.broadcasted_iota(I32, (CHUNK, CHUNK), 0)
    tq_c = lax.broadcasted_iota(I32, (CHUNK, CHUNK), 1)
    causal = tq_c <= tq_r
    ya = []
    for hh in range(A_HEADS):
        sl = slice(hh * hd, (hh + 1) * hd)
        vh = _rms(v[:, sl], vg_ref[:, sl]).astype(BF16)
        rhs = jnp.concatenate([vh[c * CHUNK:(c + 1) * CHUNK] for c in range(nch)], axis=1)
        w = jnp.where(causal, ws_ref[hh], 0.0).astype(BF16)
        zs = jnp.dot(w, rhs, preferred_element_type=F32) + bst_ref[:, hh:hh + 1]
        zs = jnp.concatenate([zs[:, c * hd:(c + 1) * hd] for c in range(nch)], axis=0)
        ya.append(u[:, sl] * zs)

    bcx = jnp.dot(hb, win_ref[:, 2 * aw:], preferred_element_type=F32)
    bg, cg, xin = bcx[:, :bw], bcx[:, bw:2 * bw], bcx[:, 2 * bw:]
    pre = cg * xin

    @pl.when(i % tiles_per_seq == 0)
    def _():
        pre_scr[0:V7X_SUBLANES, :] = jnp.zeros((V7X_SUBLANES, bw), F32)

    pre_scr[V7X_SUBLANES:V7X_SUBLANES + tm, :] = pre
    p1 = pre_scr[V7X_SUBLANES - 1:V7X_SUBLANES - 1 + tm, :]
    p2 = pre_scr[V7X_SUBLANES - 2:V7X_SUBLANES - 2 + tm, :]
    conv = cw_ref[0:1, :] * p2 + cw_ref[1:2, :] * p1 + cw_ref[2:3, :] * pre + cb_ref[...]
    yb = bg * conv
    pre_scr[0:V7X_SUBLANES, :] = pre_scr[tm:tm + V7X_SUBLANES, :]

    y = jnp.concatenate(ya + [yb], axis=1).astype(BF16)
    mix_scr[...] = jnp.dot(y, wout_ref[...], preferred_element_type=F32)

    @pl.when(i > 0)
    def _():
        for cp in h2_copies(i - 1, slot):
            cp.start()

    @pl.when(i == pl.num_programs(0) - 1)
    def _():
        for sl in range(2):
            for cp in h2_copies(0, sl):
                cp.wait()


def _mix(xt, mod3, g1, g2, win, wout, ws, bst, vg, cw, cb, wrt, br, seq):
    t, d = xt.shape
    tm = TOKEN_TILE
    nt = t // tm
    per_seq = seq // tm
    in_cols = win.shape[1]
    bw = cb.shape[1]
    words_per_row = d // 2
    assert words_per_row == V7X_SUBLANES * V7X_LANES
    const2 = lambda i: (0, 0)
    single = dict(pipeline_mode=pl.Buffered(1))
    body = functools.partial(_mix_body, tiles_per_seq=per_seq)
    cur = lambda i: jnp.minimum(i, nt - 1)
    prev = lambda i: jnp.maximum(i - 1, 0)
    vmem = (2 * d * (in_cols + d)
            + 3 * 2 * tm * d * 4
            + 2 * tm * d * 2
            + (tm + V7X_SUBLANES) * bw * 4 + tm * d * 4
            + tm * (in_cols + 4 * d) * 4
            + 8 * 1024 * 1024)
    return pl.pallas_call(
        body,
        grid=(nt + 1,),
        in_specs=[
            pl.BlockSpec((tm, d), lambda i: (cur(i), 0)),
            pl.BlockSpec((tm, d), lambda i: (prev(i), 0)),
            pl.BlockSpec((1, 1, mod3.shape[2]), lambda i: (cur(i) // per_seq, 0, 0)),
            pl.BlockSpec((1, 1, mod3.shape[2]), lambda i: (prev(i) // per_seq, 0, 0)),
            pl.BlockSpec((1, d), const2),
            pl.BlockSpec((1, d), const2),
            pl.BlockSpec(win.shape, const2, **single),
            pl.BlockSpec(wout.shape, const2, **single),
            pl.BlockSpec(ws.shape, lambda i: (0, 0, 0)),
            pl.BlockSpec(bst.shape, const2),
            pl.BlockSpec(vg.shape, const2),
            pl.BlockSpec(cw.shape, const2),
            pl.BlockSpec(cb.shape, const2),
            pl.BlockSpec(wrt.shape, const2),
            pl.BlockSpec(br.shape, const2),
        ],
        out_specs=[
            pl.BlockSpec((tm, d), lambda i: (prev(i), 0)),
            pl.BlockSpec((1, TOP_K, tm), lambda i: (prev(i), 0, 0)),
            pl.BlockSpec((tm, V7X_LANES), lambda i: (prev(i), 0)),
            pl.BlockSpec(memory_space=pl.ANY),
        ],
        out_shape=[
            jax.ShapeDtypeStruct((t, d), F32),
            jax.ShapeDtypeStruct((nt, TOP_K, tm), I32),
            jax.ShapeDtypeStruct((t, V7X_LANES), F32),
            jax.ShapeDtypeStruct((t // V7X_SUBLANES, V7X_SUBLANES, words_per_row // V7X_LANES, V7X_LANES), U32),
        ],
        scratch_shapes=[pltpu.VMEM((tm + V7X_SUBLANES, bw), F32), pltpu.VMEM((tm, d), F32),
                        pltpu.VMEM((2, tm // V7X_SUBLANES, words_per_row // V7X_LANES, V7X_SUBLANES, V7X_LANES),
                                   U32),
                        pltpu.SemaphoreType.DMA((2,))],
        compiler_params=_params(min(vmem, V7X_VMEM_BYTES - 6 * 1024 * 1024), ("arbitrary",)),
        name="mix",
    )(xt, xt, mod3, mod3, g1, g2, win, wout, ws, bst, vg, cw, cb, wrt, br)


def _dispatch_body(eid_ref, texp_ref, tab_ref, rank_scr, dest_v, dest_s, init_v, sem,
                   *, n_e, n_tiles, n_tok):
    nch, _, c = eid_ref.shape
    te = EXPERT_TILE
    r = lax.broadcasted_iota(I32, (c, c), 0)
    q = lax.broadcasted_iota(I32, (c, c), 1)
    before = (r < q).astype(BF16)
    e_iota = lax.broadcasted_iota(I32, (n_e, c), 0)

    init_v[...] = (lax.broadcasted_iota(I32, init_v.shape, 0) & (DUMP_TILES * te - 1)) + TOP_K * n_tok
    cp_i = pltpu.make_async_copy(init_v, tab_ref, sem.at[0])
    cp_i.start()

    def count_body(ch, carry):
        e2 = eid_ref[ch]
        ranks = []
        for k in range(TOP_K):
            oh = e_iota == e2[k:k + 1]
            ohf = oh.astype(F32)
            pref = jnp.dot(ohf.astype(BF16), before, preferred_element_type=F32)
            ranks.append(jnp.sum(jnp.where(oh, pref + carry, 0.0), axis=0, keepdims=True))
            carry = carry + jnp.sum(ohf, axis=1, keepdims=True)
        rank_scr[ch] = jnp.concatenate(ranks, axis=0)
        return carry

    counts = lax.fori_loop(0, nch, count_body, jnp.zeros((n_e, 1), F32))

    padded = jnp.floor((counts + float(te - 1)) / float(te)) * float(te)
    sub = lax.broadcasted_iota(I32, (n_e, V7X_LANES), 0)
    lane = lax.broadcasted_iota(I32, (n_e, V7X_LANES), 1)
    pstart_row = jnp.sum(jnp.where(sub < lane, padded, 0.0), axis=0, keepdims=True)
    pstart = jnp.sum(jnp.where(sub == lane, pstart_row, 0.0), axis=1, keepdims=True)
    pend = pstart + padded
    total = jnp.max(pend, axis=0, keepdims=True)
    last_e = jnp.max(jnp.where(counts > 0.0, sub[:, 0:1].astype(F32), -1.0), axis=0, keepdims=True)

    rows_k = n_tok // V7X_LANES
    group = V7X_SUBLANES * V7X_LANES // c

    def dest_body(g, carry):
        rows = [[] for _ in range(TOP_K)]
        for j in range(group):
            e2 = eid_ref[g * group + j]
            rk = rank_scr[g * group + j]
            for k in range(TOP_K):
                oh = e_iota == e2[k:k + 1]
                off = jnp.sum(jnp.where(oh, pstart, 0.0), axis=0, keepdims=True)
                dest = (off + rk[k:k + 1]).astype(I32)
                rows[k] += [dest[:, h * V7X_LANES:(h + 1) * V7X_LANES] for h in range(c // V7X_LANES)]
        for k in range(TOP_K):
            word0 = pl.multiple_of((k * rows_k + g * V7X_SUBLANES) * V7X_LANES, V7X_SUBLANES * V7X_LANES)
            dest_v[pl.ds(word0, V7X_SUBLANES * V7X_LANES)] = jnp.concatenate(rows[k], axis=0).reshape(-1)
        return carry

    lax.fori_loop(0, nch // group, dest_body, 0)
    cp_d = pltpu.make_async_copy(dest_v, dest_s, sem.at[1])
    cp_d.start()

    w = texp_ref.shape[1]
    tj = lax.broadcasted_iota(I32, (n_e, w), 1).astype(F32) * float(te)
    texp = jnp.sum((pend <= tj).astype(F32), axis=0, keepdims=True)
    tj1 = tj[0:1]
    texp = jnp.where(tj1 < total, jnp.minimum(texp, float(n_e - 1)), last_e)
    lane_w = lax.broadcasted_iota(I32, (1, w), 1)
    texp = jnp.where(lane_w == n_tiles, total / float(te), texp)
    texp_ref[...] = texp.astype(I32)

    cp_i.wait()
    cp_d.wait()

    def lane_body(lane, carry):
        for row in range(TOP_K * rows_k):
            a = row * V7X_LANES + lane
            tab_ref[dest_s[a]] = a
        return carry

    lax.fori_loop(0, V7X_LANES, lane_body, 0)


def _dispatch(eid, n_e, n_tiles, n_tok):
    nch, k, c = eid.shape
    assert c == EXPERT_TILE and c % V7X_LANES == 0 and nch * c == n_tok and n_tok % (V7X_SUBLANES * V7X_LANES) == 0
    body = functools.partial(_dispatch_body, n_e=n_e, n_tiles=n_tiles, n_tok=n_tok)
    w = 2 * V7X_LANES
    assert n_tiles < w
    rows = n_tiles + V7X_SUBLANES
    rows += (-rows) % 4
    smem = pl.BlockSpec(memory_space=pltpu.SMEM)
    return pl.pallas_call(
        body,
        out_specs=[pl.BlockSpec(memory_space=pltpu.VMEM), smem],
        out_shape=[
            jax.ShapeDtypeStruct((1, w), I32),
            jax.ShapeDtypeStruct((rows * EXPERT_TILE,), I32),
        ],
        scratch_shapes=[
            pltpu.VMEM((nch, k, c), F32),
            pltpu.VMEM((k * n_tok,), I32),
            pltpu.SMEM((k * n_tok,), I32),
            pltpu.VMEM((rows * EXPERT_TILE,), I32),
            pltpu.SemaphoreType.DMA((2,)),
        ],
        compiler_params=_params(16 * 1024 * 1024),
        name="dispatch",
    )(eid)


def _expert_body(tab_ref, texp_ref, h2_hbm, wgu_hbm, wd_hbm, y2_hbm,
                 x0, x1, x2, y0, y1, y2, wgu_f, wd_f, wgu_s, wd_s, gsem, ssem, wsem, zsem, *, n_tiles):
    te = EXPERT_TILE
    ring = EXPERT_RING
    de = wd_s.shape[0]
    n_tok = h2_hbm.shape[0]
    xs, ys = (x0, x1, x2), (y0, y1, y2)
    p = pl.program_id(0)
    n_used = texp_ref[n_tiles]
    t0 = ring * p

    def gather(tile, buf, sem):
        for r in range(te):
            tok = tab_ref[tile * te + r] & (n_tok - 1)
            pltpu.make_async_copy(h2_hbm.at[tok], buf.at[r // V7X_SUBLANES, :, r % V7X_SUBLANES, :], sem).start()

    def scatter(tile, buf, sem):
        for r in range(te):
            row = tab_ref[tile * te + r]
            pltpu.make_async_copy(buf.at[pl.ds(r, 1), :], y2_hbm.at[pl.ds(row, 1), :], sem).start()

    def wait_rows(buf, sem):
        pltpu.make_async_copy(buf, buf, sem).wait()

    def weight_copies(e):
        return (pltpu.make_async_copy(wgu_hbm.at[e], wgu_f, wsem.at[0]),
                pltpu.make_async_copy(wd_hbm.at[e], wd_f, wsem.at[1]))

    def dump_fill(h):
        rows = pl.ds(TOP_K * n_tok + h * te, te)
        return pltpu.make_async_copy(ys[ring - 1], y2_hbm.at[rows, :], zsem)

    def switch_weights(tile):
        e = texp_ref[tile]
        first = jnp.logical_or(tile == 0, texp_ref[jnp.maximum(tile - 1, 0)] != e)

        @pl.when(jnp.logical_and(first, tile < n_used))
        def _():
            @pl.when(tile == 0)
            def _():
                for cp in weight_copies(e):
                    cp.start(priority=1)

            for cp in weight_copies(e):
                cp.wait()
            wgu_s[...] = wgu_f[...].astype(BF16)
            wd_s[...] = wd_f[...].astype(BF16)
            nxt = lax.while_loop(
                lambda k: jnp.logical_and(k < n_used, texp_ref[jnp.minimum(k, n_tiles - 1)] == e),
                lambda k: k + 1, tile + 1)

            @pl.when(nxt < n_used)
            def _():
                for cp in weight_copies(texp_ref[jnp.minimum(nxt, n_tiles - 1)]):
                    cp.start(priority=1)

    def compute(xbuf, ybuf):
        words = jnp.concatenate([xbuf[:, s].reshape(te, V7X_LANES) for s in range(V7X_SUBLANES)], axis=1)
        xb16 = _unpack_bf16_pairs(words).astype(BF16)
        gu = jnp.dot(xb16, wgu_s[...], preferred_element_type=F32)
        a = (jax.nn.silu(gu[:, :de]) * gu[:, de:]).astype(BF16)
        ybuf[...] = _pack_bf16_pairs(jnp.dot(a, wd_s[...], preferred_element_type=F32))

    @pl.when(t0 < n_used)
    def _():
        @pl.when(p == 0)
        def _():
            ys[ring - 1][...] = jnp.zeros(ys[ring - 1].shape, U32)
            for h in range(DUMP_TILES):
                dump_fill(h).start()
            for h in range(DUMP_TILES):
                dump_fill(h).wait()
            for m in range(ring - 1):
                gather(m, xs[m], gsem.at[m])

        for m in range(ring):
            tile = t0 + m
            nm = (m + ring - 1) % ring
            switch_weights(tile)
            wait_rows(xs[m], gsem.at[m])
            if m == ring - 1:
                wait_rows(ys[m], ssem.at[m])
            else:
                @pl.when(p > 0)
                def _(m=m):
                    wait_rows(ys[m], ssem.at[m])
            gather(tile + ring - 1, xs[nm], gsem.at[nm])
            scatter(jnp.where(tile == 0, n_tiles, tile - 1), ys[nm], ssem.at[nm])
            compute(xs[m], ys[m])

        @pl.when(t0 + ring >= n_used)
        def _():
            for m in range(ring - 1):
                wait_rows(ys[m], ssem.at[m])
            scatter(t0 + ring - 1, ys[ring - 1], ssem.at[ring - 1])
            wait_rows(ys[ring - 1], ssem.at[ring - 1])
            for m in range(ring - 1):
                wait_rows(xs[m], gsem.at[m])


def _experts(tab, texp, h2p, wgu, wd, n_tiles):
    t = h2p.shape[0]
    dp = h2p.shape[1] * h2p.shape[2]
    te = EXPERT_TILE
    _, d, n_gu = wgu.shape
    de = wd.shape[1]
    assert n_tiles % EXPERT_RING == 0 and t & (t - 1) == 0
    body = functools.partial(_expert_body, n_tiles=n_tiles)
    vmem = ((d * n_gu + de * d) * (4 + 2) + 2 * EXPERT_RING * te * dp * 4 + te * (n_gu + 3 * d) * 4
            + 4 * 1024 * 1024)
    hbm = pl.BlockSpec(memory_space=pl.ANY)
    row_buf = pltpu.VMEM((te, dp), U32)
    tile_buf = pltpu.VMEM((te // V7X_SUBLANES,) + h2p.shape[1:2] + (V7X_SUBLANES, V7X_LANES), U32)
    return pl.pallas_call(
        body,
        grid_spec=pltpu.PrefetchScalarGridSpec(
            num_scalar_prefetch=2,
            grid=(n_tiles // EXPERT_RING,),
            in_specs=[hbm, hbm, hbm],
            out_specs=hbm,
            scratch_shapes=[tile_buf] * EXPERT_RING + [row_buf] * EXPERT_RING + [
                pltpu.VMEM((d, n_gu), F32), pltpu.VMEM((de, d), F32),
                pltpu.VMEM((d, n_gu), BF16), pltpu.VMEM((de, d), BF16),
                pltpu.SemaphoreType.DMA((EXPERT_RING,)), pltpu.SemaphoreType.DMA((EXPERT_RING,)),
                pltpu.SemaphoreType.DMA((2,)), pltpu.SemaphoreType.DMA,
            ],
        ),
        out_shape=jax.ShapeDtypeStruct((TOP_K * t + DUMP_TILES * te, dp), U32),
        compiler_params=_params(vmem, ("arbitrary",)),
        name="experts",
    )(tab, texp, h2p, wgu, wd)


def _combine_body(x1_ref, y0_ref, y1_ref, gcol_ref, mod_ref, modf_ref, gf_ref, o_ref):
    d = x1_ref.shape[1]
    gate2 = mod_ref[0][:, 5 * d:6 * d]
    modf = modf_ref[0]
    shift_f, scale_f = modf[:, 0:d], modf[:, d:2 * d]
    y0 = _unpack_bf16_pairs(y0_ref[...])
    y1 = _unpack_bf16_pairs(y1_ref[...])
    ffn = y0 * gcol_ref[:, 0:1] + y1 * gcol_ref[:, 1:2]
    x2 = x1_ref[...] + gate2 * ffn
    o_ref[...] = _rms(x2, gf_ref[...]) * (1.0 + scale_f) + shift_f


def _combine(x1, y2, gcol, mod3, modf3, gf, seq):
    t, d = x1.shape
    tm = COMBINE_TILE
    per_seq = seq // tm
    return pl.pallas_call(
        _combine_body,
        grid=(t // tm,),
        in_specs=[
            pl.BlockSpec((tm, d), lambda i: (i, 0)),
            pl.BlockSpec((tm, d // 2), lambda i: (i, 0)),
            pl.BlockSpec((tm, d // 2), lambda i: (t // tm + i, 0)),
            pl.BlockSpec((tm, V7X_LANES), lambda i: (i, 0)),
            pl.BlockSpec((1, 1, mod3.shape[2]), lambda i: (i // per_seq, 0, 0)),
            pl.BlockSpec((1, 1, modf3.shape[2]), lambda i: (i // per_seq, 0, 0)),
            pl.BlockSpec((1, d), lambda i: (0, 0)),
        ],
        out_specs=pl.BlockSpec((tm, d), lambda i: (i, 0)),
        out_shape=jax.ShapeDtypeStruct((t, d), F32),
        compiler_params=_params(2 * tm * (3 * d + V7X_LANES) * 4 + 4 * tm * d * 4, ("arbitrary",)),
        name="combine",
    )(x1, y2, y2, gcol, mod3, modf3, gf)


def kernel(x, c, w_ada, b_ada, norm1_g, w_in, w_out, gmlp_w_s, gmlp_b_s, gmlp_v_gain, conv_w, conv_b,
           norm2_g, w_router_group, b_router_group, w_router_expert, b_router_expert, w_gate_up, w_down,
           w_ada_final, b_ada_final, norm_f_g):
    bsz, seq, d = x.shape
    depth = w_ada.shape[0]
    n_e = w_router_expert.shape[2]
    t = bsz * seq
    assert seq % TOKEN_TILE == 0 and TOKEN_TILE % CHUNK == 0 and seq % COMBINE_TILE == 0
    assert w_router_group.shape[2] == N_GROUPS and n_e + N_GROUPS <= ROUTER_ROWS
    n_tiles = (t * TOP_K + n_e * (EXPERT_TILE - 1)) // EXPERT_TILE
    n_tiles = -(-n_tiles // EXPERT_RING) * EXPERT_RING

    modf3 = _ada(c, w_ada_final, b_ada_final).reshape(bsz, 1, 2 * d)
    xt = x.reshape(t, d)
    for l in range(depth):
        mod3 = _ada(c, w_ada[l], b_ada[l]).reshape(bsz, 1, -1)
        wr = jnp.concatenate([w_router_expert[l], w_router_group[l]], axis=1)
        wrt = jnp.pad(wr, ((0, 0), (0, ROUTER_ROWS - wr.shape[1]))).T.astype(BF16)
        br = jnp.concatenate([b_router_expert[l], b_router_group[l]])
        br = jnp.pad(br, (0, ROUTER_ROWS - br.shape[0])).reshape(ROUTER_ROWS, 1)
        x1, eid, gcol, h2 = _mix(
            xt, mod3, norm1_g[l].reshape(1, d), norm2_g[l].reshape(1, d),
            w_in[l].astype(BF16), w_out[l].astype(BF16), gmlp_w_s[l], gmlp_b_s[l].T,
            gmlp_v_gain[l].reshape(1, -1), conv_w[l], conv_b[l].reshape(1, -1), wrt, br, seq)
        texp, tab = _dispatch(eid, n_e, n_tiles, t)
        h2 = h2.reshape((t,) + h2.shape[2:])
        y2 = _experts(tab, texp.reshape(-1), h2, w_gate_up[l], w_down[l], n_tiles)
        assert depth == 1
        xt = _combine(x1, y2, gcol, mod3, modf3, norm_f_g.reshape(1, d), seq)
    return xt.reshape(bsz, seq, d)
```

```python
import functools

import jax
import jax.numpy as jnp
from jax import lax
from jax.experimental import pallas as pl
from jax.experimental.pallas import tpu as pltpu

F32 = jnp.float32
BF16 = jnp.bfloat16
I32 = jnp.int32
U32 = jnp.uint32

A_HEADS = 8
CHUNK = 128
N_GROUPS = 4
TOP_K = 2
EPS = 1e-6

V7X_LANES = 128
V7X_SUBLANES = 8
V7X_VMEM_BYTES = 64 * 1024 * 1024

TOKEN_TILE = 256
EXPERT_TILE = 256
COMBINE_TILE = 512
MIX_STAGE_ROWS = 64
ROUTER_ROWS = 40
EXPERT_RING = 3
DUMP_TILES = 2


def _rms(x, g):
    y = x * lax.rsqrt(jnp.mean(x * x, axis=-1, keepdims=True) + EPS)
    return y * g


def _pack_bf16_pairs(x):
    n = x.shape[1] // 2
    lo = lax.bitcast_convert_type(x[:, :n].astype(BF16).astype(F32), U32)
    hi = lax.bitcast_convert_type(x[:, n:].astype(BF16).astype(F32), U32)
    return hi | (lo >> 16)


def _unpack_bf16_pairs(w):
    lo = lax.bitcast_convert_type(w << 16, F32)
    hi = lax.bitcast_convert_type(w & jnp.uint32(0xFFFF0000), F32)
    return jnp.concatenate([lo, hi], axis=1)


def _zero_after(anchor, n):
    bits = lax.bitcast_convert_type(anchor, U32)
    zero = lax.bitcast_convert_type((bits >> 16) >> 16, F32)
    return jnp.concatenate([zero] * (n // anchor.shape[1]), axis=1)


def _params(vmem_bytes, semantics=None):
    kw = dict(vmem_limit_bytes=int(vmem_bytes))
    if semantics is not None:
        kw["dimension_semantics"] = semantics
    return pltpu.CompilerParams(**kw)


def _ada_body(c_ref, w_ref, b_ref, o_ref):
    ca = jax.nn.silu(c_ref[...]).astype(BF16)
    o_ref[...] = jnp.dot(ca, w_ref[...].astype(BF16), preferred_element_type=F32) + b_ref[...]


def _ada(c, w, b, tn=1024):
    bsz, d = c.shape
    n = w.shape[1]
    return pl.pallas_call(
        _ada_body,
        grid=(n // tn,),
        in_specs=[
            pl.BlockSpec((bsz, d), lambda j: (0, 0)),
            pl.BlockSpec((d, tn), lambda j: (0, j)),
            pl.BlockSpec((1, tn), lambda j: (0, j)),
        ],
        out_specs=pl.BlockSpec((bsz, tn), lambda j: (0, j)),
        out_shape=jax.ShapeDtypeStruct((bsz, n), F32),
        compiler_params=_params(4 * d * tn * 4, ("arbitrary",)),
        name="ada",
    )(c, w, b.reshape(1, n))


def _route(lt, tm):
    n_e = 32
    epg = n_e // N_GROUPS
    row = lax.broadcasted_iota(I32, (V7X_SUBLANES, tm), 0).astype(F32)
    gl = lt[n_e:n_e + V7X_SUBLANES]
    gvalid = row < float(N_GROUPS)
    glm = jnp.where(gvalid, gl, -jnp.inf)
    gmax = jnp.max(glm, axis=0, keepdims=True)
    garg = jnp.min(jnp.where(glm == gmax, row, float(V7X_SUBLANES)), axis=0, keepdims=True)
    gsum = jnp.sum(jnp.where(gvalid, jnp.exp(gl - gmax), 0.0), axis=0, keepdims=True)
    p_grp = 1.0 / gsum
    es = lt[0:epg]
    for g in range(1, N_GROUPS):
        es = jnp.where(garg == float(g), lt[g * epg:(g + 1) * epg], es)
    m1 = jnp.max(es, axis=0, keepdims=True)
    i1 = jnp.min(jnp.where(es == m1, row, float(epg)), axis=0, keepdims=True)
    es2 = jnp.where(row == i1, -jnp.inf, es)
    m2 = jnp.max(es2, axis=0, keepdims=True)
    i2 = jnp.min(jnp.where(es2 == m2, row, float(epg)), axis=0, keepdims=True)
    z = jnp.exp(m2 - m1)
    den = 1.0 + z
    g0 = p_grp / den
    g1 = p_grp * z / den
    e0 = garg * float(epg) + i1
    e1 = garg * float(epg) + i2
    zero = jnp.zeros_like(g0)
    return jnp.concatenate([g0, g1, e0, e1, zero, zero, zero, zero], axis=0)


def _load_cast(src_hbm, dst, stage, sem):
    rows = stage.shape[1]
    cols = src_hbm.shape[1]
    n = src_hbm.shape[0] // rows

    def copy(c, slot):
        return pltpu.make_async_copy(src_hbm.at[pl.ds(c * rows, rows), :],
                                     stage.at[slot, :, pl.ds(0, cols)], sem.at[slot])

    copy(0, 0).start()

    def body(c, carry):
        slot = c % 2

        @pl.when(c + 1 < n)
        def _():
            copy(c + 1, 1 - slot).start()

        copy(c, slot).wait()
        r0 = pl.multiple_of(c * rows, rows)
        dst[pl.ds(r0, rows), :] = stage[slot, :, 0:cols].astype(BF16)
        return carry

    lax.fori_loop(0, n, body, 0)


def _mix_body(x_ref, xp_ref, mod_ref, modp_ref, g1_ref, g2_ref, win_hbm, wout_hbm, ws_ref, bst_ref, vg_ref,
              cw_ref, cb_ref, wrt_ref, br_ref,
              x1_ref, eid_ref, gcol_ref, h2_hbm,
              pre_scr, mix_scr, h2_scr, win_s, wout_s, stage, h2_sem, w_sem, *, tiles_per_seq):
    tm, d = x_ref.shape
    aw = vg_ref.shape[1]
    bw = cb_ref.shape[1]
    hd = aw // A_HEADS
    nch = tm // CHUNK
    i = pl.program_id(0)
    last = pl.num_programs(0) - 1
    groups = tm // V7X_SUBLANES
    slot = i % 2

    def h2_copies(tile, sl):
        return [pltpu.make_async_copy(h2_scr.at[sl, :, s], h2_hbm.at[pl.ds(tile * groups, groups), :, s, :],
                                      h2_sem.at[sl])
                for s in range(V7X_SUBLANES)]

    @pl.when(i == 0)
    def _():
        mix_scr[...] = jnp.zeros(mix_scr.shape, F32)
        _load_cast(win_hbm, win_s, stage, w_sem)
        _load_cast(wout_hbm, wout_s, stage, w_sem)

    @pl.when(i > 2)
    def _():
        for cp in h2_copies(0, slot):
            cp.wait()

    def last_stage(anchor):
        modp = modp_ref[0]
        gate1p, shift2p, scale2p = modp[:, 2 * d:3 * d], modp[:, 3 * d:4 * d], modp[:, 4 * d:5 * d]
        if anchor is not None:
            gate1p = gate1p + _zero_after(anchor, d)
        x1 = xp_ref[...] + gate1p * mix_scr[...]
        x1_ref[...] = x1
        h2 = _rms(x1, g2_ref[...]) * (1.0 + scale2p) + shift2p
        words = _pack_bf16_pairs(h2)
        for s in range(V7X_SUBLANES):
            h2_scr[slot, :, s] = words[:, s * V7X_LANES:(s + 1) * V7X_LANES].reshape(
                groups, V7X_SUBLANES, V7X_LANES)
        lt = lax.dot_general(wrt_ref[...], h2.astype(BF16), (((1,), (1,)), ((), ())),
                             preferred_element_type=F32) + br_ref[...]
        slab = _route(lt, tm)
        eid_ref[0] = slab[2:4].astype(I32)
        wide = jnp.concatenate([slab, jnp.zeros((V7X_LANES - V7X_SUBLANES, tm), F32)], axis=0)
        gcol_ref[...] = wide.T

    @pl.when(i < last)
    def _():
        x = x_ref[...]
        mod = mod_ref[0]
        shift1, scale1 = mod[:, 0:d], mod[:, d:2 * d]
        h = _rms(x, g1_ref[...]) * (1.0 + scale1) + shift1
        hb = h.astype(BF16)

        uv = jnp.dot(hb, win_s[:, 0:2 * aw], preferred_element_type=F32)
        last_stage(uv[0:1, 0:V7X_LANES])
        uv = jax.nn.gelu(uv)
        u, v = uv[:, :aw], uv[:, aw:]
        tq_r = lax.broadcasted_iota(I32, (CHUNK, CHUNK), 0)
        tq_c = lax.broadcasted_iota(I32, (CHUNK, CHUNK), 1)
        causal = tq_c <= tq_r
        ya = []
        for hh in range(A_HEADS):
            sl = slice(hh * hd, (hh + 1) * hd)
            vh = _rms(v[:, sl], vg_ref[:, sl]).astype(BF16)
            rhs = jnp.concatenate([vh[c * CHUNK:(c + 1) * CHUNK] for c in range(nch)], axis=1)
            w = jnp.where(causal, ws_ref[hh], 0.0).astype(BF16)
            zs = jnp.dot(w, rhs, preferred_element_type=F32) + bst_ref[:, hh:hh + 1]
            zs = jnp.concatenate([zs[:, c * hd:(c + 1) * hd] for c in range(nch)], axis=0)
            ya.append(u[:, sl] * zs)

        bcx = jnp.dot(hb, win_s[:, 2 * aw:], preferred_element_type=F32)
        bg, cg, xin = bcx[:, :bw], bcx[:, bw:2 * bw], bcx[:, 2 * bw:]
        pre = cg * xin

        @pl.when(i % tiles_per_seq == 0)
        def _():
            pre_scr[0:V7X_SUBLANES, :] = jnp.zeros((V7X_SUBLANES, bw), F32)

        pre_scr[V7X_SUBLANES:V7X_SUBLANES + tm, :] = pre
        p1 = pre_scr[V7X_SUBLANES - 1:V7X_SUBLANES - 1 + tm, :]
        p2 = pre_scr[V7X_SUBLANES - 2:V7X_SUBLANES - 2 + tm, :]
        conv = cw_ref[0:1, :] * p2 + cw_ref[1:2, :] * p1 + cw_ref[2:3, :] * pre + cb_ref[...]
        yb = bg * conv
        pre_scr[0:V7X_SUBLANES, :] = pre_scr[tm:tm + V7X_SUBLANES, :]

        y = jnp.concatenate(ya + [yb], axis=1).astype(BF16)
        mix_scr[...] = jnp.dot(y, wout_s[...], preferred_element_type=F32)

    @pl.when(i == last)
    def _():
        last_stage(None)

    @pl.when(i > 0)
    def _():
        for cp in h2_copies(i - 1, slot):
            cp.start()

    @pl.when(i == last)
    def _():
        for sl in range(2):
            for cp in h2_copies(0, sl):
                cp.wait()


def _mix(xt, mod3, g1, g2, win, wout, ws, bst, vg, cw, cb, wrt, br, seq):
    t, d = xt.shape
    tm = TOKEN_TILE
    nt = t // tm
    per_seq = seq // tm
    in_cols = win.shape[1]
    bw = cb.shape[1]
    words_per_row = d // 2
    assert words_per_row == V7X_SUBLANES * V7X_LANES
    assert d % MIX_STAGE_ROWS == 0 and wout.shape[1] <= in_cols
    const2 = lambda i: (0, 0)
    body = functools.partial(_mix_body, tiles_per_seq=per_seq)
    cur = lambda i: jnp.minimum(i, nt - 1)
    prev = lambda i: jnp.maximum(i - 1, 0)
    hbm = pl.BlockSpec(memory_space=pl.ANY)
    vmem = (2 * d * (in_cols + d)
            + 2 * MIX_STAGE_ROWS * in_cols * 4
            + 3 * 2 * tm * d * 4
            + 2 * tm * d * 2
            + (tm + V7X_SUBLANES) * bw * 4 + tm * d * 4
            + tm * (in_cols + 4 * d) * 4
            + 8 * 1024 * 1024)
    return pl.pallas_call(
        body,
        grid=(nt + 1,),
        in_specs=[
            pl.BlockSpec((tm, d), lambda i: (cur(i), 0)),
            pl.BlockSpec((tm, d), lambda i: (prev(i), 0)),
            pl.BlockSpec((1, 1, mod3.shape[2]), lambda i: (cur(i) // per_seq, 0, 0)),
            pl.BlockSpec((1, 1, mod3.shape[2]), lambda i: (prev(i) // per_seq, 0, 0)),
            pl.BlockSpec((1, d), const2),
            pl.BlockSpec((1, d), const2),
            hbm,
            hbm,
            pl.BlockSpec(ws.shape, lambda i: (0, 0, 0)),
            pl.BlockSpec(bst.shape, const2),
            pl.BlockSpec(vg.shape, const2),
            pl.BlockSpec(cw.shape, const2),
            pl.BlockSpec(cb.shape, const2),
            pl.BlockSpec(wrt.shape, const2),
            pl.BlockSpec(br.shape, const2),
        ],
        out_specs=[
            pl.BlockSpec((tm, d), lambda i: (prev(i), 0)),
            pl.BlockSpec((1, TOP_K, tm), lambda i: (prev(i), 0, 0)),
            pl.BlockSpec((tm, V7X_LANES), lambda i: (prev(i), 0)),
            hbm,
        ],
        out_shape=[
            jax.ShapeDtypeStruct((t, d), F32),
            jax.ShapeDtypeStruct((nt, TOP_K, tm), I32),
            jax.ShapeDtypeStruct((t, V7X_LANES), F32),
            jax.ShapeDtypeStruct((t // V7X_SUBLANES, V7X_SUBLANES, words_per_row // V7X_LANES, V7X_LANES), U32),
        ],
        scratch_shapes=[
            pltpu.VMEM((tm + V7X_SUBLANES, bw), F32),
            pltpu.VMEM((tm, d), F32),
            pltpu.VMEM((2, tm // V7X_SUBLANES, words_per_row // V7X_LANES, V7X_SUBLANES, V7X_LANES), U32),
            pltpu.VMEM(win.shape, BF16),
            pltpu.VMEM(wout.shape, BF16),
            pltpu.VMEM((2, MIX_STAGE_ROWS, in_cols), F32),
            pltpu.SemaphoreType.DMA((2,)),
            pltpu.SemaphoreType.DMA((2,)),
        ],
        compiler_params=_params(min(vmem, V7X_VMEM_BYTES - 6 * 1024 * 1024), ("arbitrary",)),
        name="mix",
    )(xt, xt, mod3, mod3, g1, g2, win, wout, ws, bst, vg, cw, cb, wrt, br)


def _dispatch_body(eid_ref, texp_ref, tab_ref, rank_scr, dest_v, dest_s, init_v, sem,
                   *, n_e, n_tiles, n_tok):
    nch, _, c = eid_ref.shape
    te = EXPERT_TILE
    r = lax.broadcasted_iota(I32, (c, c), 0)
    q = lax.broadcasted_iota(I32, (c, c), 1)
    before = (r < q).astype(BF16)
    e_iota = lax.broadcasted_iota(I32, (n_e, c), 0)

    init_v[...] = (lax.broadcasted_iota(I32, init_v.shape, 0) & (DUMP_TILES * te - 1)) + TOP_K * n_tok
    cp_i = pltpu.make_async_copy(init_v, tab_ref, sem.at[0])
    cp_i.start()

    def count_body(ch, carry):
        e2 = eid_ref[ch]
        ranks = []
        for k in range(TOP_K):
            oh = e_iota == e2[k:k + 1]
            ohf = oh.astype(F32)
            pref = jnp.dot(ohf.astype(BF16), before, preferred_element_type=F32)
            ranks.append(jnp.sum(jnp.where(oh, pref + carry, 0.0), axis=0, keepdims=True))
            carry = carry + jnp.sum(ohf, axis=1, keepdims=True)
        rank_scr[ch] = jnp.concatenate(ranks, axis=0)
        return carry

    counts = lax.fori_loop(0, nch, count_body, jnp.zeros((n_e, 1), F32))

    padded = jnp.floor((counts + float(te - 1)) / float(te)) * float(te)
    sub = lax.broadcasted_iota(I32, (n_e, V7X_LANES), 0)
    lane = lax.broadcasted_iota(I32, (n_e, V7X_LANES), 1)
    pstart_row = jnp.sum(jnp.where(sub < lane, padded, 0.0), axis=0, keepdims=True)
    pstart = jnp.sum(jnp.where(sub == lane, pstart_row, 0.0), axis=1, keepdims=True)
    pend = pstart + padded
    total = jnp.max(pend, axis=0, keepdims=True)
    last_e = jnp.max(jnp.where(counts > 0.0, sub[:, 0:1].astype(F32), -1.0), axis=0, keepdims=True)

    rows_k = n_tok // V7X_LANES
    group = V7X_SUBLANES * V7X_LANES // c

    def dest_body(g, carry):
        rows = [[] for _ in range(TOP_K)]
        for j in range(group):
            e2 = eid_ref[g * group + j]
            rk = rank_scr[g * group + j]
            for k in range(TOP_K):
                oh = e_iota == e2[k:k + 1]
                off = jnp.sum(jnp.where(oh, pstart, 0.0), axis=0, keepdims=True)
                dest = (off + rk[k:k + 1]).astype(I32)
                rows[k] += [dest[:, h * V7X_LANES:(h + 1) * V7X_LANES] for h in range(c // V7X_LANES)]
        for k in range(TOP_K):
            word0 = pl.multiple_of((k * rows_k + g * V7X_SUBLANES) * V7X_LANES, V7X_SUBLANES * V7X_LANES)
            dest_v[pl.ds(word0, V7X_SUBLANES * V7X_LANES)] = jnp.concatenate(rows[k], axis=0).reshape(-1)
        return carry

    lax.fori_loop(0, nch // group, dest_body, 0)
    cp_d = pltpu.make_async_copy(dest_v, dest_s, sem.at[1])
    cp_d.start()

    w = texp_ref.shape[1]
    tj = lax.broadcasted_iota(I32, (n_e, w), 1).astype(F32) * float(te)
    texp = jnp.sum((pend <= tj).astype(F32), axis=0, keepdims=True)
    tj1 = tj[0:1]
    texp = jnp.where(tj1 < total, jnp.minimum(texp, float(n_e - 1)), last_e)
    lane_w = lax.broadcasted_iota(I32, (1, w), 1)
    texp = jnp.where(lane_w == n_tiles, total / float(te), texp)
    texp_ref[...] = texp.astype(I32)

    cp_i.wait()
    cp_d.wait()

    def lane_body(lane, carry):
        for row in range(TOP_K * rows_k):
            a = row * V7X_LANES + lane
            tab_ref[dest_s[a]] = a
        return carry

    lax.fori_loop(0, V7X_LANES, lane_body, 0)


def _dispatch(eid, n_e, n_tiles, n_tok):
    nch, k, c = eid.shape
    assert c == EXPERT_TILE and c % V7X_LANES == 0 and nch * c == n_tok and n_tok % (V7X_SUBLANES * V7X_LANES) == 0
    body = functools.partial(_dispatch_body, n_e=n_e, n_tiles=n_tiles, n_tok=n_tok)
    w = 2 * V7X_LANES
    assert n_tiles < w
    rows = n_tiles + V7X_SUBLANES
    rows += (-rows) % 4
    smem = pl.BlockSpec(memory_space=pltpu.SMEM)
    return pl.pallas_call(
        body,
        out_specs=[pl.BlockSpec(memory_space=pltpu.VMEM), smem],
        out_shape=[
            jax.ShapeDtypeStruct((1, w), I32),
            jax.ShapeDtypeStruct((rows * EXPERT_TILE,), I32),
        ],
        scratch_shapes=[
            pltpu.VMEM((nch, k, c), F32),
            pltpu.VMEM((k * n_tok,), I32),
            pltpu.SMEM((k * n_tok,), I32),
            pltpu.VMEM((rows * EXPERT_TILE,), I32),
            pltpu.SemaphoreType.DMA((2,)),
        ],
        compiler_params=_params(16 * 1024 * 1024),
        name="dispatch",
    )(eid)


def _expert_body(tab_ref, texp_ref, h2_hbm, wgu_hbm, wd_hbm, y2_hbm,
                 x0, x1, x2, y0, y1, y2, wgu_f, wd_f, wgu_s, wd_s, gsem, ssem, wsem, zsem, *, n_tiles):
    te = EXPERT_TILE
    ring = EXPERT_RING
    de = wd_s.shape[0]
    n_tok = h2_hbm.shape[0]
    xs, ys = (x0, x1, x2), (y0, y1, y2)
    p = pl.program_id(0)
    n_used = texp_ref[n_tiles]
    t0 = ring * p

    def gather(tile, buf, sem):
        for r in range(te):
            tok = tab_ref[tile * te + r] & (n_tok - 1)
            pltpu.make_async_copy(h2_hbm.at[tok], buf.at[r // V7X_SUBLANES, :, r % V7X_SUBLANES, :], sem).start()

    def scatter(tile, buf, sem):
        for r in range(te):
            row = tab_ref[tile * te + r]
            pltpu.make_async_copy(buf.at[pl.ds(r, 1), :], y2_hbm.at[pl.ds(row, 1), :], sem).start()

    def wait_rows(buf, sem):
        pltpu.make_async_copy(buf, buf, sem).wait()

    def weight_copies(e):
        return (pltpu.make_async_copy(wgu_hbm.at[e], wgu_f, wsem.at[0]),
                pltpu.make_async_copy(wd_hbm.at[e], wd_f, wsem.at[1]))

    def dump_fill(h):
        rows = pl.ds(TOP_K * n_tok + h * te, te)
        return pltpu.make_async_copy(ys[ring - 1], y2_hbm.at[rows, :], zsem)

    def switch_weights(tile):
        e = texp_ref[tile]
        first = jnp.logical_or(tile == 0, texp_ref[jnp.maximum(tile - 1, 0)] != e)

        @pl.when(jnp.logical_and(first, tile < n_used))
        def _():
            @pl.when(tile == 0)
            def _():
                for cp in weight_copies(e):
                    cp.start(priority=1)

            for cp in weight_copies(e):
                cp.wait()
            wgu_s[...] = wgu_f[...].astype(BF16)
            wd_s[...] = wd_f[...].astype(BF16)
            nxt = lax.while_loop(
                lambda k: jnp.logical_and(k < n_used, texp_ref[jnp.minimum(k, n_tiles - 1)] == e),
                lambda k: k + 1, tile + 1)

            @pl.when(nxt < n_used)
            def _():
                for cp in weight_copies(texp_ref[jnp.minimum(nxt, n_tiles - 1)]):
                    cp.start(priority=1)

    def compute(xbuf, ybuf):
        words = jnp.concatenate([xbuf[:, s].reshape(te, V7X_LANES) for s in range(V7X_SUBLANES)], axis=1)
        xb16 = _unpack_bf16_pairs(words).astype(BF16)
        gu = jnp.dot(xb16, wgu_s[...], preferred_element_type=F32)
        a = (jax.nn.silu(gu[:, :de]) * gu[:, de:]).astype(BF16)
        ybuf[...] = _pack_bf16_pairs(jnp.dot(a, wd_s[...], preferred_element_type=F32))

    @pl.when(t0 < n_used)
    def _():
        @pl.when(p == 0)
        def _():
            ys[ring - 1][...] = jnp.zeros(ys[ring - 1].shape, U32)
            for h in range(DUMP_TILES):
                dump_fill(h).start()
            for h in range(DUMP_TILES):
                dump_fill(h).wait()
            for m in range(ring - 1):
                gather(m, xs[m], gsem.at[m])

        for m in range(ring):
            tile = t0 + m
            nm = (m + ring - 1) % ring
            switch_weights(tile)
            wait_rows(xs[m], gsem.at[m])
            if m == ring - 1:
                wait_rows(ys[m], ssem.at[m])
            else:
                @pl.when(p > 0)
                def _(m=m):
                    wait_rows(ys[m], ssem.at[m])
            gather(tile + ring - 1, xs[nm], gsem.at[nm])
            scatter(jnp.where(tile == 0, n_tiles, tile - 1), ys[nm], ssem.at[nm])
            compute(xs[m], ys[m])

        @pl.when(t0 + ring >= n_used)
        def _():
            for m in range(ring - 1):
                wait_rows(ys[m], ssem.at[m])
            scatter(t0 + ring - 1, ys[ring - 1], ssem.at[ring - 1])
            wait_rows(ys[ring - 1], ssem.at[ring - 1])
            for m in range(ring - 1):
                wait_rows(xs[m], gsem.at[m])


def _experts(tab, texp, h2p, wgu, wd, n_tiles):
    t = h2p.shape[0]
    dp = h2p.shape[1] * h2p.shape[2]
    te = EXPERT_TILE
    _, d, n_gu = wgu.shape
    de = wd.shape[1]
    assert n_tiles % EXPERT_RING == 0 and t & (t - 1) == 0
    body = functools.partial(_expert_body, n_tiles=n_tiles)
    vmem = ((d * n_gu + de * d) * (4 + 2) + 2 * EXPERT_RING * te * dp * 4 + te * (n_gu + 3 * d) * 4
            + 4 * 1024 * 1024)
    hbm = pl.BlockSpec(memory_space=pl.ANY)
    row_buf = pltpu.VMEM((te, dp), U32)
    tile_buf = pltpu.VMEM((te // V7X_SUBLANES,) + h2p.shape[1:2] + (V7X_SUBLANES, V7X_LANES), U32)
    return pl.pallas_call(
        body,
        grid_spec=pltpu.PrefetchScalarGridSpec(
            num_scalar_prefetch=2,
            grid=(n_tiles // EXPERT_RING,),
            in_specs=[hbm, hbm, hbm],
            out_specs=hbm,
            scratch_shapes=[tile_buf] * EXPERT_RING + [row_buf] * EXPERT_RING + [
                pltpu.VMEM((d, n_gu), F32), pltpu.VMEM((de, d), F32),
                pltpu.VMEM((d, n_gu), BF16), pltpu.VMEM((de, d), BF16),
                pltpu.SemaphoreType.DMA((EXPERT_RING,)), pltpu.SemaphoreType.DMA((EXPERT_RING,)),
                pltpu.SemaphoreType.DMA((2,)), pltpu.SemaphoreType.DMA,
            ],
        ),
        out_shape=jax.ShapeDtypeStruct((TOP_K * t + DUMP_TILES * te, dp), U32),
        compiler_params=_params(vmem, ("arbitrary",)),
        name="experts",
    )(tab, texp, h2p, wgu, wd)


def _combine_body(x1_ref, y0_ref, y1_ref, gcol_ref, mod_ref, modf_ref, gf_ref, o_ref):
    d = x1_ref.shape[1]
    gate2 = mod_ref[0][:, 5 * d:6 * d]
    modf = modf_ref[0]
    shift_f, scale_f = modf[:, 0:d], modf[:, d:2 * d]
    y0 = _unpack_bf16_pairs(y0_ref[...])
    y1 = _unpack_bf16_pairs(y1_ref[...])
    ffn = y0 * gcol_ref[:, 0:1] + y1 * gcol_ref[:, 1:2]
    x2 = x1_ref[...] + gate2 * ffn
    o_ref[...] = _rms(x2, gf_ref[...]) * (1.0 + scale_f) + shift_f


def _combine(x1, y2, gcol, mod3, modf3, gf, seq):
    t, d = x1.shape
    tm = COMBINE_TILE
    per_seq = seq // tm
    return pl.pallas_call(
        _combine_body,
        grid=(t // tm,),
        in_specs=[
            pl.BlockSpec((tm, d), lambda i: (i, 0)),
            pl.BlockSpec((tm, d // 2), lambda i: (i, 0)),
            pl.BlockSpec((tm, d // 2), lambda i: (t // tm + i, 0)),
            pl.BlockSpec((tm, V7X_LANES), lambda i: (i, 0)),
            pl.BlockSpec((1, 1, mod3.shape[2]), lambda i: (i // per_seq, 0, 0)),
            pl.BlockSpec((1, 1, modf3.shape[2]), lambda i: (i // per_seq, 0, 0)),
            pl.BlockSpec((1, d), lambda i: (0, 0)),
        ],
        out_specs=pl.BlockSpec((tm, d), lambda i: (i, 0)),
        out_shape=jax.ShapeDtypeStruct((t, d), F32),
        compiler_params=_params(2 * tm * (3 * d + V7X_LANES) * 4 + 4 * tm * d * 4, ("arbitrary",)),
        name="combine",
    )(x1, y2, y2, gcol, mod3, modf3, gf)


def kernel(x, c, w_ada, b_ada, norm1_g, w_in, w_out, gmlp_w_s, gmlp_b_s, gmlp_v_gain, conv_w, conv_b,
           norm2_g, w_router_group, b_router_group, w_router_expert, b_router_expert, w_gate_up, w_down,
           w_ada_final, b_ada_final, norm_f_g):
    bsz, seq, d = x.shape
    depth = w_ada.shape[0]
    n_e = w_router_expert.shape[2]
    t = bsz * seq
    assert seq % TOKEN_TILE == 0 and TOKEN_TILE % CHUNK == 0 and seq % COMBINE_TILE == 0
    assert w_router_group.shape[2] == N_GROUPS and n_e + N_GROUPS <= ROUTER_ROWS
    n_tiles = (t * TOP_K + n_e * (EXPERT_TILE - 1)) // EXPERT_TILE
    n_tiles = -(-n_tiles // EXPERT_RING) * EXPERT_RING

    modf3 = _ada(c, w_ada_final, b_ada_final).reshape(bsz, 1, 2 * d)
    xt = x.reshape(t, d)
    for l in range(depth):
        mod3 = _ada(c, w_ada[l], b_ada[l]).reshape(bsz, 1, -1)
        wr = jnp.concatenate([w_router_expert[l], w_router_group[l]], axis=1)
        wrt = jnp.pad(wr, ((0, 0), (0, ROUTER_ROWS - wr.shape[1]))).T.astype(BF16)
        br = jnp.concatenate([b_router_expert[l], b_router_group[l]])
        br = jnp.pad(br, (0, ROUTER_ROWS - br.shape[0])).reshape(ROUTER_ROWS, 1)
        x1, eid, gcol, h2 = _mix(
            xt, mod3, norm1_g[l].reshape(1, d), norm2_g[l].reshape(1, d),
            w_in[l], w_out[l], gmlp_w_s[l], gmlp_b_s[l].T,
            gmlp_v_gain[l].reshape(1, -1), conv_w[l], conv_b[l].reshape(1, -1), wrt, br, seq)
        texp, tab = _dispatch(eid, n_e, n_tiles, t)
        h2 = h2.reshape((t,) + h2.shape[2:])
        y2 = _experts(tab, texp.reshape(-1), h2, w_gate_up[l], w_down[l], n_tiles)
        assert depth == 1
        xt = _combine(x1, y2, gcol, mod3, modf3, norm_f_g.reshape(1, d), seq)
    return xt.reshape(bsz, seq, d)
```

```python
import functools

import jax
import jax.numpy as jnp
from jax import lax
from jax.experimental import pallas as pl
from jax.experimental.pallas import tpu as pltpu

F32 = jnp.float32
BF16 = jnp.bfloat16
I32 = jnp.int32
U32 = jnp.uint32

A_HEADS = 8
CHUNK = 128
N_GROUPS = 4
TOP_K = 2
EPS = 1e-6

V7X_LANES = 128
V7X_SUBLANES = 8
V7X_VMEM_BYTES = 64 * 1024 * 1024

TOKEN_TILE = 256
EXPERT_TILE = 256
COMBINE_TILE = 512
MIX_STAGE_ROWS = 64
MIX_STAGE_SLOTS = 4
ROUTER_ROWS = 40
EXPERT_RING = 3
DUMP_TILES = 2


def _rms(x, g):
    y = x * lax.rsqrt(jnp.mean(x * x, axis=-1, keepdims=True) + EPS)
    return y * g


def _pack_bf16_pairs(x):
    n = x.shape[1] // 2
    lo = lax.bitcast_convert_type(x[:, :n].astype(BF16).astype(F32), U32)
    hi = lax.bitcast_convert_type(x[:, n:].astype(BF16).astype(F32), U32)
    return hi | (lo >> 16)


def _unpack_bf16_pairs(w):
    lo = lax.bitcast_convert_type(w << 16, F32)
    hi = lax.bitcast_convert_type(w & jnp.uint32(0xFFFF0000), F32)
    return jnp.concatenate([lo, hi], axis=1)


def _zero_after(anchor, n):
    bits = lax.bitcast_convert_type(anchor, U32)
    zero = lax.bitcast_convert_type((bits >> 16) >> 16, F32)
    return jnp.concatenate([zero] * (n // anchor.shape[1]), axis=1)


def _params(vmem_bytes, semantics=None):
    kw = dict(vmem_limit_bytes=int(vmem_bytes))
    if semantics is not None:
        kw["dimension_semantics"] = semantics
    return pltpu.CompilerParams(**kw)


def _ada_body(c_ref, w_ref, b_ref, o_ref):
    ca = jax.nn.silu(c_ref[...]).astype(BF16)
    o_ref[...] = jnp.dot(ca, w_ref[...].astype(BF16), preferred_element_type=F32) + b_ref[...]


def _ada(c, w, b, tn=1024):
    bsz, d = c.shape
    n = w.shape[1]
    return pl.pallas_call(
        _ada_body,
        grid=(n // tn,),
        in_specs=[
            pl.BlockSpec((bsz, d), lambda j: (0, 0)),
            pl.BlockSpec((d, tn), lambda j: (0, j)),
            pl.BlockSpec((1, tn), lambda j: (0, j)),
        ],
        out_specs=pl.BlockSpec((bsz, tn), lambda j: (0, j)),
        out_shape=jax.ShapeDtypeStruct((bsz, n), F32),
        compiler_params=_params(4 * d * tn * 4, ("arbitrary",)),
        name="ada",
    )(c, w, b.reshape(1, n))


def _route(lt, tm):
    n_e = 32
    epg = n_e // N_GROUPS
    row = lax.broadcasted_iota(I32, (V7X_SUBLANES, tm), 0).astype(F32)
    gl = lt[n_e:n_e + V7X_SUBLANES]
    gvalid = row < float(N_GROUPS)
    glm = jnp.where(gvalid, gl, -jnp.inf)
    gmax = jnp.max(glm, axis=0, keepdims=True)
    garg = jnp.min(jnp.where(glm == gmax, row, float(V7X_SUBLANES)), axis=0, keepdims=True)
    gsum = jnp.sum(jnp.where(gvalid, jnp.exp(gl - gmax), 0.0), axis=0, keepdims=True)
    p_grp = 1.0 / gsum
    es = lt[0:epg]
    for g in range(1, N_GROUPS):
        es = jnp.where(garg == float(g), lt[g * epg:(g + 1) * epg], es)
    m1 = jnp.max(es, axis=0, keepdims=True)
    i1 = jnp.min(jnp.where(es == m1, row, float(epg)), axis=0, keepdims=True)
    es2 = jnp.where(row == i1, -jnp.inf, es)
    m2 = jnp.max(es2, axis=0, keepdims=True)
    i2 = jnp.min(jnp.where(es2 == m2, row, float(epg)), axis=0, keepdims=True)
    z = jnp.exp(m2 - m1)
    den = 1.0 + z
    g0 = p_grp / den
    g1 = p_grp * z / den
    e0 = garg * float(epg) + i1
    e1 = garg * float(epg) + i2
    zero = jnp.zeros_like(g0)
    return jnp.concatenate([g0, g1, e0, e1, zero, zero, zero, zero], axis=0)


def _load_cast(src_hbm, dst, stage, sem):
    slots, rows = stage.shape[0], stage.shape[1]
    cols = src_hbm.shape[1]
    n = src_hbm.shape[0] // rows
    ahead = slots - 1

    def copy(c, slot):
        return pltpu.make_async_copy(src_hbm.at[pl.ds(c * rows, rows), :],
                                     stage.at[slot, :, pl.ds(0, cols)], sem.at[slot])

    for c in range(ahead):
        copy(c, c).start()

    def body(c, carry):
        slot = c % slots

        @pl.when(c + ahead < n)
        def _():
            copy(c + ahead, (c + ahead) % slots).start()

        copy(c, slot).wait()
        r0 = pl.multiple_of(c * rows, rows)
        dst[pl.ds(r0, rows), :] = stage[slot, :, 0:cols].astype(BF16)
        return carry

    lax.fori_loop(0, n, body, 0)


def _mix_body(x_ref, xp_ref, mod_ref, modp_ref, g1_ref, g2_ref, win_hbm, wout_hbm, ws_ref, bst_ref, vg_ref,
              cw_ref, cb_ref, wrt_ref, br_ref,
              x1_ref, eid_ref, gcol_ref, h2_hbm,
              pre_scr, mix_scr, h2_scr, win_s, wout_s, stage, h2_sem, w_sem, *, tiles_per_seq):
    tm, d = x_ref.shape
    aw = vg_ref.shape[1]
    bw = cb_ref.shape[1]
    hd = aw // A_HEADS
    nch = tm // CHUNK
    i = pl.program_id(0)
    last = pl.num_programs(0) - 1
    groups = tm // V7X_SUBLANES
    slot = i % 2

    def h2_copies(tile, sl):
        return [pltpu.make_async_copy(h2_scr.at[sl, :, s], h2_hbm.at[pl.ds(tile * groups, groups), :, s, :],
                                      h2_sem.at[sl])
                for s in range(V7X_SUBLANES)]

    @pl.when(i == 0)
    def _():
        mix_scr[...] = jnp.zeros(mix_scr.shape, F32)
        _load_cast(win_hbm, win_s, stage, w_sem)
        _load_cast(wout_hbm, wout_s, stage, w_sem)

    @pl.when(i > 2)
    def _():
        for cp in h2_copies(0, slot):
            cp.wait()

    def last_stage(anchor):
        modp = modp_ref[0]
        gate1p, shift2p, scale2p = modp[:, 2 * d:3 * d], modp[:, 3 * d:4 * d], modp[:, 4 * d:5 * d]
        if anchor is not None:
            gate1p = gate1p + _zero_after(anchor, d)
        x1 = xp_ref[...] + gate1p * mix_scr[...]
        x1_ref[...] = x1
        h2 = _rms(x1, g2_ref[...]) * (1.0 + scale2p) + shift2p
        words = _pack_bf16_pairs(h2)
        for s in range(V7X_SUBLANES):
            h2_scr[slot, :, s] = words[:, s * V7X_LANES:(s + 1) * V7X_LANES].reshape(
                groups, V7X_SUBLANES, V7X_LANES)
        lt = lax.dot_general(wrt_ref[...], h2.astype(BF16), (((1,), (1,)), ((), ())),
                             preferred_element_type=F32) + br_ref[...]
        slab = _route(lt, tm)
        eid_ref[0] = slab[2:4].astype(I32)
        wide = jnp.concatenate([slab, jnp.zeros((V7X_LANES - V7X_SUBLANES, tm), F32)], axis=0)
        gcol_ref[...] = wide.T

    @pl.when(i < last)
    def _():
        x = x_ref[...]
        mod = mod_ref[0]
        shift1, scale1 = mod[:, 0:d], mod[:, d:2 * d]
        h = _rms(x, g1_ref[...]) * (1.0 + scale1) + shift1
        hb = h.astype(BF16)

        uv = jnp.dot(hb, win_s[:, 0:2 * aw], preferred_element_type=F32)
        last_stage(uv[0:1, 0:V7X_LANES])
        uv = jax.nn.gelu(uv)
        u, v = uv[:, :aw], uv[:, aw:]
        tq_r = lax.broadcasted_iota(I32, (CHUNK, CHUNK), 0)
        tq_c = lax.broadcasted_iota(I32, (CHUNK, CHUNK), 1)
        causal = tq_c <= tq_r
        ya = []
        for hh in range(A_HEADS):
            sl = slice(hh * hd, (hh + 1) * hd)
            vh = _rms(v[:, sl], vg_ref[:, sl]).astype(BF16)
            rhs = jnp.concatenate([vh[c * CHUNK:(c + 1) * CHUNK] for c in range(nch)], axis=1)
            w = jnp.where(causal, ws_ref[hh], 0.0).astype(BF16)
            zs = jnp.dot(w, rhs, preferred_element_type=F32) + bst_ref[:, hh:hh + 1]
            zs = jnp.concatenate([zs[:, c * hd:(c + 1) * hd] for c in range(nch)], axis=0)
            ya.append(u[:, sl] * zs)

        bcx = jnp.dot(hb, win_s[:, 2 * aw:], preferred_element_type=F32)
        bg, cg, xin = bcx[:, :bw], bcx[:, bw:2 * bw], bcx[:, 2 * bw:]
        pre = cg * xin

        @pl.when(i % tiles_per_seq == 0)
        def _():
            pre_scr[0:V7X_SUBLANES, :] = jnp.zeros((V7X_SUBLANES, bw), F32)

        pre_scr[V7X_SUBLANES:V7X_SUBLANES + tm, :] = pre
        p1 = pre_scr[V7X_SUBLANES - 1:V7X_SUBLANES - 1 + tm, :]
        p2 = pre_scr[V7X_SUBLANES - 2:V7X_SUBLANES - 2 + tm, :]
        conv = cw_ref[0:1, :] * p2 + cw_ref[1:2, :] * p1 + cw_ref[2:3, :] * pre + cb_ref[...]
        yb = bg * conv
        pre_scr[0:V7X_SUBLANES, :] = pre_scr[tm:tm + V7X_SUBLANES, :]

        y = jnp.concatenate(ya + [yb], axis=1).astype(BF16)
        mix_scr[...] = jnp.dot(y, wout_s[...], preferred_element_type=F32)

    @pl.when(i == last)
    def _():
        last_stage(None)

    @pl.when(i > 0)
    def _():
        for cp in h2_copies(i - 1, slot):
            cp.start()

    @pl.when(i == last)
    def _():
        for sl in range(2):
            for cp in h2_copies(0, sl):
                cp.wait()


def _mix(xt, mod3, g1, g2, win, wout, ws, bst, vg, cw, cb, wrt, br, seq):
    t, d = xt.shape
    tm = TOKEN_TILE
    nt = t // tm
    per_seq = seq // tm
    in_cols = win.shape[1]
    bw = cb.shape[1]
    words_per_row = d // 2
    assert words_per_row == V7X_SUBLANES * V7X_LANES
    assert d % MIX_STAGE_ROWS == 0 and d // MIX_STAGE_ROWS >= MIX_STAGE_SLOTS and wout.shape[1] <= in_cols
    const2 = lambda i: (0, 0)
    body = functools.partial(_mix_body, tiles_per_seq=per_seq)
    cur = lambda i: jnp.minimum(i, nt - 1)
    prev = lambda i: jnp.maximum(i - 1, 0)
    hbm = pl.BlockSpec(memory_space=pl.ANY)
    vmem = (2 * d * (in_cols + d)
            + MIX_STAGE_SLOTS * MIX_STAGE_ROWS * in_cols * 4
            + 3 * 2 * tm * d * 4
            + 2 * tm * d * 2
            + (tm + V7X_SUBLANES) * bw * 4 + tm * d * 4
            + tm * (in_cols + 4 * d) * 4
            + 8 * 1024 * 1024)
    return pl.pallas_call(
        body,
        grid=(nt + 1,),
        in_specs=[
            pl.BlockSpec((tm, d), lambda i: (cur(i), 0)),
            pl.BlockSpec((tm, d), lambda i: (prev(i), 0)),
            pl.BlockSpec((1, 1, mod3.shape[2]), lambda i: (cur(i) // per_seq, 0, 0)),
            pl.BlockSpec((1, 1, mod3.shape[2]), lambda i: (prev(i) // per_seq, 0, 0)),
            pl.BlockSpec((1, d), const2),
            pl.BlockSpec((1, d), const2),
            hbm,
            hbm,
            pl.BlockSpec(ws.shape, lambda i: (0, 0, 0)),
            pl.BlockSpec(bst.shape, const2),
            pl.BlockSpec(vg.shape, const2),
            pl.BlockSpec(cw.shape, const2),
            pl.BlockSpec(cb.shape, const2),
            pl.BlockSpec(wrt.shape, const2),
            pl.BlockSpec(br.shape, const2),
        ],
        out_specs=[
            pl.BlockSpec((tm, d), lambda i: (prev(i), 0)),
            pl.BlockSpec((1, TOP_K, tm), lambda i: (prev(i), 0, 0)),
            pl.BlockSpec((tm, V7X_LANES), lambda i: (prev(i), 0)),
            hbm,
        ],
        out_shape=[
            jax.ShapeDtypeStruct((t, d), F32),
            jax.ShapeDtypeStruct((nt, TOP_K, tm), I32),
            jax.ShapeDtypeStruct((t, V7X_LANES), F32),
            jax.ShapeDtypeStruct((t // V7X_SUBLANES, V7X_SUBLANES, words_per_row // V7X_LANES, V7X_LANES), U32),
        ],
        scratch_shapes=[
            pltpu.VMEM((tm + V7X_SUBLANES, bw), F32),
            pltpu.VMEM((tm, d), F32),
            pltpu.VMEM((2, tm // V7X_SUBLANES, words_per_row // V7X_LANES, V7X_SUBLANES, V7X_LANES), U32),
            pltpu.VMEM(win.shape, BF16),
            pltpu.VMEM(wout.shape, BF16),
            pltpu.VMEM((MIX_STAGE_SLOTS, MIX_STAGE_ROWS, in_cols), F32),
            pltpu.SemaphoreType.DMA((2,)),
            pltpu.SemaphoreType.DMA((MIX_STAGE_SLOTS,)),
        ],
        compiler_params=_params(min(vmem, V7X_VMEM_BYTES - 6 * 1024 * 1024), ("arbitrary",)),
        name="mix",
    )(xt, xt, mod3, mod3, g1, g2, win, wout, ws, bst, vg, cw, cb, wrt, br)


def _dispatch_body(eid_ref, texp_ref, tab_ref, rank_scr, dest_v, dest_s, init_v, sem,
                   *, n_e, n_tiles, n_tok):
    nch, _, c = eid_ref.shape
    te = EXPERT_TILE
    r = lax.broadcasted_iota(I32, (c, c), 0)
    q = lax.broadcasted_iota(I32, (c, c), 1)
    before = (r < q).astype(BF16)
    e_iota = lax.broadcasted_iota(I32, (n_e, c), 0)

    init_v[...] = (lax.broadcasted_iota(I32, init_v.shape, 0) & (DUMP_TILES * te - 1)) + TOP_K * n_tok
    cp_i = pltpu.make_async_copy(init_v, tab_ref, sem.at[0])
    cp_i.start()

    def count_body(ch, carry):
        e2 = eid_ref[ch]
        ranks = []
        for k in range(TOP_K):
            oh = e_iota == e2[k:k + 1]
            ohf = oh.astype(F32)
            pref = jnp.dot(ohf.astype(BF16), before, preferred_element_type=F32)
            ranks.append(jnp.sum(jnp.where(oh, pref + carry, 0.0), axis=0, keepdims=True))
            carry = carry + jnp.sum(ohf, axis=1, keepdims=True)
        rank_scr[ch] = jnp.concatenate(ranks, axis=0)
        return carry

    counts = lax.fori_loop(0, nch, count_body, jnp.zeros((n_e, 1), F32))

    padded = jnp.floor((counts + float(te - 1)) / float(te)) * float(te)
    sub = lax.broadcasted_iota(I32, (n_e, V7X_LANES), 0)
    lane = lax.broadcasted_iota(I32, (n_e, V7X_LANES), 1)
    pstart_row = jnp.sum(jnp.where(sub < lane, padded, 0.0), axis=0, keepdims=True)
    pstart = jnp.sum(jnp.where(sub == lane, pstart_row, 0.0), axis=1, keepdims=True)
    pend = pstart + padded
    total = jnp.max(pend, axis=0, keepdims=True)
    last_e = jnp.max(jnp.where(counts > 0.0, sub[:, 0:1].astype(F32), -1.0), axis=0, keepdims=True)

    rows_k = n_tok // V7X_LANES
    group = V7X_SUBLANES * V7X_LANES // c

    def dest_body(g, carry):
        rows = [[] for _ in range(TOP_K)]
        for j in range(group):
            e2 = eid_ref[g * group + j]
            rk = rank_scr[g * group + j]
            for k in range(TOP_K):
                oh = e_iota == e2[k:k + 1]
                off = jnp.sum(jnp.where(oh, pstart, 0.0), axis=0, keepdims=True)
                dest = (off + rk[k:k + 1]).astype(I32)
                rows[k] += [dest[:, h * V7X_LANES:(h + 1) * V7X_LANES] for h in range(c // V7X_LANES)]
        for k in range(TOP_K):
            word0 = pl.multiple_of((k * rows_k + g * V7X_SUBLANES) * V7X_LANES, V7X_SUBLANES * V7X_LANES)
            dest_v[pl.ds(word0, V7X_SUBLANES * V7X_LANES)] = jnp.concatenate(rows[k], axis=0).reshape(-1)
        return carry

    lax.fori_loop(0, nch // group, dest_body, 0)
    cp_d = pltpu.make_async_copy(dest_v, dest_s, sem.at[1])
    cp_d.start()

    w = texp_ref.shape[1]
    tj = lax.broadcasted_iota(I32, (n_e, w), 1).astype(F32) * float(te)
    texp = jnp.sum((pend <= tj).astype(F32), axis=0, keepdims=True)
    tj1 = tj[0:1]
    texp = jnp.where(tj1 < total, jnp.minimum(texp, float(n_e - 1)), last_e)
    lane_w = lax.broadcasted_iota(I32, (1, w), 1)
    texp = jnp.where(lane_w == n_tiles, total / float(te), texp)
    texp_ref[...] = texp.astype(I32)

    cp_i.wait()
    cp_d.wait()

    def lane_body(lane, carry):
        for row in range(TOP_K * rows_k):
            a = row * V7X_LANES + lane
            tab_ref[dest_s[a]] = a
        return carry

    lax.fori_loop(0, V7X_LANES, lane_body, 0)


def _dispatch(eid, n_e, n_tiles, n_tok):
    nch, k, c = eid.shape
    assert c == EXPERT_TILE and c % V7X_LANES == 0 and nch * c == n_tok and n_tok % (V7X_SUBLANES * V7X_LANES) == 0
    body = functools.partial(_dispatch_body, n_e=n_e, n_tiles=n_tiles, n_tok=n_tok)
    w = 2 * V7X_LANES
    assert n_tiles < w
    rows = n_tiles + V7X_SUBLANES
    rows += (-rows) % 4
    smem = pl.BlockSpec(memory_space=pltpu.SMEM)
    return pl.pallas_call(
        body,
        out_specs=[pl.BlockSpec(memory_space=pltpu.VMEM), smem],
        out_shape=[
            jax.ShapeDtypeStruct((1, w), I32),
            jax.ShapeDtypeStruct((rows * EXPERT_TILE,), I32),
        ],
        scratch_shapes=[
            pltpu.VMEM((nch, k, c), F32),
            pltpu.VMEM((k * n_tok,), I32),
            pltpu.SMEM((k * n_tok,), I32),
            pltpu.VMEM((rows * EXPERT_TILE,), I32),
            pltpu.SemaphoreType.DMA((2,)),
        ],
        compiler_params=_params(16 * 1024 * 1024),
        name="dispatch",
    )(eid)


def _expert_body(tab_ref, texp_ref, h2_hbm, wgu_hbm, wd_hbm, y2_hbm,
                 x0, x1, x2, y0, y1, y2, wgu_f, wd_f, wgu_s, wd_s, gsem, ssem, wsem, zsem, *, n_tiles):
    te = EXPERT_TILE
    ring = EXPERT_RING
    de = wd_s.shape[0]
    n_tok = h2_hbm.shape[0]
    xs, ys = (x0, x1, x2), (y0, y1, y2)
    p = pl.program_id(0)
    n_used = texp_ref[n_tiles]
    t0 = ring * p

    def gather(tile, buf, sem):
        for r in range(te):
            tok = tab_ref[tile * te + r] & (n_tok - 1)
            pltpu.make_async_copy(h2_hbm.at[tok], buf.at[r // V7X_SUBLANES, :, r % V7X_SUBLANES, :], sem).start()

    def scatter(tile, buf, sem):
        for r in range(te):
            row = tab_ref[tile * te + r]
            pltpu.make_async_copy(buf.at[pl.ds(r, 1), :], y2_hbm.at[pl.ds(row, 1), :], sem).start()

    def wait_rows(buf, sem):
        pltpu.make_async_copy(buf, buf, sem).wait()

    def weight_copies(e):
        return (pltpu.make_async_copy(wgu_hbm.at[e], wgu_f, wsem.at[0]),
                pltpu.make_async_copy(wd_hbm.at[e], wd_f, wsem.at[1]))

    def dump_fill(h):
        rows = pl.ds(TOP_K * n_tok + h * te, te)
        return pltpu.make_async_copy(ys[ring - 1], y2_hbm.at[rows, :], zsem)

    def switch_weights(tile):
        e = texp_ref[tile]
        first = jnp.logical_or(tile == 0, texp_ref[jnp.maximum(tile - 1, 0)] != e)

        @pl.when(jnp.logical_and(first, tile < n_used))
        def _():
            @pl.when(tile == 0)
            def _():
                for cp in weight_copies(e):
                    cp.start(priority=1)

            for cp in weight_copies(e):
                cp.wait()
            wgu_s[...] = wgu_f[...].astype(BF16)
            wd_s[...] = wd_f[...].astype(BF16)
            nxt = lax.while_loop(
                lambda k: jnp.logical_and(k < n_used, texp_ref[jnp.minimum(k, n_tiles - 1)] == e),
                lambda k: k + 1, tile + 1)

            @pl.when(nxt < n_used)
            def _():
                for cp in weight_copies(texp_ref[jnp.minimum(nxt, n_tiles - 1)]):
                    cp.start(priority=1)

    def compute(xbuf, ybuf):
        words = jnp.concatenate([xbuf[:, s].reshape(te, V7X_LANES) for s in range(V7X_SUBLANES)], axis=1)
        xb16 = _unpack_bf16_pairs(words).astype(BF16)
        gu = jnp.dot(xb16, wgu_s[...], preferred_element_type=F32)
        a = (jax.nn.silu(gu[:, :de]) * gu[:, de:]).astype(BF16)
        ybuf[...] = _pack_bf16_pairs(jnp.dot(a, wd_s[...], preferred_element_type=F32))

    @pl.when(t0 < n_used)
    def _():
        @pl.when(p == 0)
        def _():
            ys[ring - 1][...] = jnp.zeros(ys[ring - 1].shape, U32)
            for h in range(DUMP_TILES):
                dump_fill(h).start()
            for h in range(DUMP_TILES):
                dump_fill(h).wait()
            for m in range(ring - 1):
                gather(m, xs[m], gsem.at[m])

        for m in range(ring):
            tile = t0 + m
            nm = (m + ring - 1) % ring
            switch_weights(tile)
            wait_rows(xs[m], gsem.at[m])
            if m == ring - 1:
                wait_rows(ys[m], ssem.at[m])
            else:
                @pl.when(p > 0)
                def _(m=m):
                    wait_rows(ys[m], ssem.at[m])
            gather(tile + ring - 1, xs[nm], gsem.at[nm])
            scatter(jnp.where(tile == 0, n_tiles, tile - 1), ys[nm], ssem.at[nm])
            compute(xs[m], ys[m])

        @pl.when(t0 + ring >= n_used)
        def _():
            for m in range(ring - 1):
                wait_rows(ys[m], ssem.at[m])
            scatter(t0 + ring - 1, ys[ring - 1], ssem.at[ring - 1])
            wait_rows(ys[ring - 1], ssem.at[ring - 1])
            for m in range(ring - 1):
                wait_rows(xs[m], gsem.at[m])


def _experts(tab, texp, h2p, wgu, wd, n_tiles):
    t = h2p.shape[0]
    dp = h2p.shape[1] * h2p.shape[2]
    te = EXPERT_TILE
    _, d, n_gu = wgu.shape
    de = wd.shape[1]
    assert n_tiles % EXPERT_RING == 0 and t & (t - 1) == 0
    body = functools.partial(_expert_body, n_tiles=n_tiles)
    vmem = ((d * n_gu + de * d) * (4 + 2) + 2 * EXPERT_RING * te * dp * 4 + te * (n_gu + 3 * d) * 4
            + 4 * 1024 * 1024)
    hbm = pl.BlockSpec(memory_space=pl.ANY)
    row_buf = pltpu.VMEM((te, dp), U32)
    tile_buf = pltpu.VMEM((te // V7X_SUBLANES,) + h2p.shape[1:2] + (V7X_SUBLANES, V7X_LANES), U32)
    return pl.pallas_call(
        body,
        grid_spec=pltpu.PrefetchScalarGridSpec(
            num_scalar_prefetch=2,
            grid=(n_tiles // EXPERT_RING,),
            in_specs=[hbm, hbm, hbm],
            out_specs=hbm,
            scratch_shapes=[tile_buf] * EXPERT_RING + [row_buf] * EXPERT_RING + [
                pltpu.VMEM((d, n_gu), F32), pltpu.VMEM((de, d), F32),
                pltpu.VMEM((d, n_gu), BF16), pltpu.VMEM((de, d), BF16),
                pltpu.SemaphoreType.DMA((EXPERT_RING,)), pltpu.SemaphoreType.DMA((EXPERT_RING,)),
                pltpu.SemaphoreType.DMA((2,)), pltpu.SemaphoreType.DMA,
            ],
        ),
        out_shape=jax.ShapeDtypeStruct((TOP_K * t + DUMP_TILES * te, dp), U32),
        compiler_params=_params(vmem, ("arbitrary",)),
        name="experts",
    )(tab, texp, h2p, wgu, wd)


def _combine_body(x1_ref, y0_ref, y1_ref, gcol_ref, mod_ref, modf_ref, gf_ref, o_ref):
    d = x1_ref.shape[1]
    gate2 = mod_ref[0][:, 5 * d:6 * d]
    modf = modf_ref[0]
    shift_f, scale_f = modf[:, 0:d], modf[:, d:2 * d]
    y0 = _unpack_bf16_pairs(y0_ref[...])
    y1 = _unpack_bf16_pairs(y1_ref[...])
    ffn = y0 * gcol_ref[:, 0:1] + y1 * gcol_ref[:, 1:2]
    x2 = x1_ref[...] + gate2 * ffn
    o_ref[...] = _rms(x2, gf_ref[...]) * (1.0 + scale_f) + shift_f


def _combine(x1, y2, gcol, mod3, modf3, gf, seq):
    t, d = x1.shape
    tm = COMBINE_TILE
    per_seq = seq // tm
    return pl.pallas_call(
        _combine_body,
        grid=(t // tm,),
        in_specs=[
            pl.BlockSpec((tm, d), lambda i: (i, 0)),
            pl.BlockSpec((tm, d // 2), lambda i: (i, 0)),
            pl.BlockSpec((tm, d // 2), lambda i: (t // tm + i, 0)),
            pl.BlockSpec((tm, V7X_LANES), lambda i: (i, 0)),
            pl.BlockSpec((1, 1, mod3.shape[2]), lambda i: (i // per_seq, 0, 0)),
            pl.BlockSpec((1, 1, modf3.shape[2]), lambda i: (i // per_seq, 0, 0)),
            pl.BlockSpec((1, d), lambda i: (0, 0)),
        ],
        out_specs=pl.BlockSpec((tm, d), lambda i: (i, 0)),
        out_shape=jax.ShapeDtypeStruct((t, d), F32),
        compiler_params=_params(2 * tm * (3 * d + V7X_LANES) * 4 + 4 * tm * d * 4, ("arbitrary",)),
        name="combine",
    )(x1, y2, y2, gcol, mod3, modf3, gf)


def kernel(x, c, w_ada, b_ada, norm1_g, w_in, w_out, gmlp_w_s, gmlp_b_s, gmlp_v_gain, conv_w, conv_b,
           norm2_g, w_router_group, b_router_group, w_router_expert, b_router_expert, w_gate_up, w_down,
           w_ada_final, b_ada_final, norm_f_g):
    bsz, seq, d = x.shape
    depth = w_ada.shape[0]
    n_e = w_router_expert.shape[2]
    t = bsz * seq
    assert seq % TOKEN_TILE == 0 and TOKEN_TILE % CHUNK == 0 and seq % COMBINE_TILE == 0
    assert w_router_group.shape[2] == N_GROUPS and n_e + N_GROUPS <= ROUTER_ROWS
    n_tiles = (t * TOP_K + n_e * (EXPERT_TILE - 1)) // EXPERT_TILE
    n_tiles = -(-n_tiles // EXPERT_RING) * EXPERT_RING

    modf3 = _ada(c, w_ada_final, b_ada_final).reshape(bsz, 1, 2 * d)
    xt = x.reshape(t, d)
    for l in range(depth):
        mod3 = _ada(c, w_ada[l], b_ada[l]).reshape(bsz, 1, -1)
        wr = jnp.concatenate([w_router_expert[l], w_router_group[l]], axis=1)
        wrt = jnp.pad(wr, ((0, 0), (0, ROUTER_ROWS - wr.shape[1]))).T.astype(BF16)
        br = jnp.concatenate([b_router_expert[l], b_router_group[l]])
        br = jnp.pad(br, (0, ROUTER_ROWS - br.shape[0])).reshape(ROUTER_ROWS, 1)
        x1, eid, gcol, h2 = _mix(
            xt, mod3, norm1_g[l].reshape(1, d), norm2_g[l].reshape(1, d),
            w_in[l], w_out[l], gmlp_w_s[l], gmlp_b_s[l].T,
            gmlp_v_gain[l].reshape(1, -1), conv_w[l], conv_b[l].reshape(1, -1), wrt, br, seq)
        texp, tab = _dispatch(eid, n_e, n_tiles, t)
        h2 = h2.reshape((t,) + h2.shape[2:])
        y2 = _experts(tab, texp.reshape(-1), h2, w_gate_up[l], w_down[l], n_tiles)
        assert depth == 1
        xt = _combine(x1, y2, gcol, mod3, modf3, norm_f_g.reshape(1, d), seq)
    return xt.reshape(bsz, seq, d)
```

```python
import functools

import jax
import jax.numpy as jnp
from jax import lax
from jax.experimental import pallas as pl
from jax.experimental.pallas import tpu as pltpu

F32 = jnp.float32
BF16 = jnp.bfloat16
I32 = jnp.int32
U32 = jnp.uint32

A_HEADS = 8
CHUNK = 128
N_GROUPS = 4
TOP_K = 2
EPS = 1e-6

V7X_LANES = 128
V7X_SUBLANES = 8
V7X_VMEM_BYTES = 64 * 1024 * 1024
V7X_VMEM_RESERVE = 6 * 1024 * 1024
VMEM_TEMP_ALLOWANCE = 8 * 1024 * 1024

TOKEN_TILE = 256
EXPERT_TILE = 256
COMBINE_TILE = 512
ADA_TILE = 1024
MIX_STAGE_ROWS = 64
MIX_STAGE_SLOTS = 4
ROUTER_ROWS = 40
EXPERT_RING = 3
EXPERT_STEP_TILES = 2 * EXPERT_RING
DUMP_TILES = 2


def _rms(x, g):
    y = x * lax.rsqrt(jnp.mean(x * x, axis=-1, keepdims=True) + EPS)
    return y * g


def _pack_bf16_pairs(x):
    n = x.shape[1] // 2
    lo = lax.bitcast_convert_type(x[:, :n].astype(BF16).astype(F32), U32)
    hi = lax.bitcast_convert_type(x[:, n:].astype(BF16).astype(F32), U32)
    return hi | (lo >> 16)


def _unpack_bf16_pairs(w):
    lo = lax.bitcast_convert_type(w << 16, F32)
    hi = lax.bitcast_convert_type(w & jnp.uint32(0xFFFF0000), F32)
    return jnp.concatenate([lo, hi], axis=1)


def _zero_after(anchor, n):
    bits = lax.bitcast_convert_type(anchor, U32)
    zero = lax.bitcast_convert_type((bits >> 16) >> 16, F32)
    return jnp.concatenate([zero] * (n // anchor.shape[1]), axis=1)


def _params(buffer_bytes, semantics=None):
    vmem_bytes = min(buffer_bytes + VMEM_TEMP_ALLOWANCE, V7X_VMEM_BYTES - V7X_VMEM_RESERVE)
    kw = dict(vmem_limit_bytes=int(vmem_bytes))
    if semantics is not None:
        kw["dimension_semantics"] = semantics
    return pltpu.CompilerParams(**kw)


def _ada_body(c_ref, w_ref, b_ref, o_ref):
    ca = jax.nn.silu(c_ref[...]).astype(BF16)
    o_ref[...] = jnp.dot(ca, w_ref[...].astype(BF16), preferred_element_type=F32) + b_ref[...]


def _ada(c, w, b):
    bsz, d = c.shape
    n = w.shape[1]
    tn = ADA_TILE
    return pl.pallas_call(
        _ada_body,
        grid=(n // tn,),
        in_specs=[
            pl.BlockSpec((bsz, d), lambda j: (0, 0)),
            pl.BlockSpec((d, tn), lambda j: (0, j)),
            pl.BlockSpec((1, tn), lambda j: (0, j)),
        ],
        out_specs=pl.BlockSpec((bsz, tn), lambda j: (0, j)),
        out_shape=jax.ShapeDtypeStruct((bsz, n), F32),
        compiler_params=_params(2 * d * tn * 4 + d * tn * 2, ("arbitrary",)),
        name="ada",
    )(c, w, b.reshape(1, n))


def _route(lt, tm):
    n_e = 32
    epg = n_e // N_GROUPS
    row = lax.broadcasted_iota(I32, (V7X_SUBLANES, tm), 0).astype(F32)
    gl = lt[n_e:n_e + V7X_SUBLANES]
    gvalid = row < float(N_GROUPS)
    glm = jnp.where(gvalid, gl, -jnp.inf)
    gmax = jnp.max(glm, axis=0, keepdims=True)
    garg = jnp.min(jnp.where(glm == gmax, row, float(V7X_SUBLANES)), axis=0, keepdims=True)
    gsum = jnp.sum(jnp.where(gvalid, jnp.exp(gl - gmax), 0.0), axis=0, keepdims=True)
    p_grp = 1.0 / gsum
    es = lt[0:epg]
    for g in range(1, N_GROUPS):
        es = jnp.where(garg == float(g), lt[g * epg:(g + 1) * epg], es)
    m1 = jnp.max(es, axis=0, keepdims=True)
    i1 = jnp.min(jnp.where(es == m1, row, float(epg)), axis=0, keepdims=True)
    es2 = jnp.where(row == i1, -jnp.inf, es)
    m2 = jnp.max(es2, axis=0, keepdims=True)
    i2 = jnp.min(jnp.where(es2 == m2, row, float(epg)), axis=0, keepdims=True)
    z = jnp.exp(m2 - m1)
    den = 1.0 + z
    g0 = p_grp / den
    g1 = p_grp * z / den
    e0 = garg * float(epg) + i1
    e1 = garg * float(epg) + i2
    zero = jnp.zeros_like(g0)
    return jnp.concatenate([g0, g1, e0, e1, zero, zero, zero, zero], axis=0)


def _load_cast(src_hbm, dst, stage, sem):
    slots, rows = stage.shape[0], stage.shape[1]
    cols = src_hbm.shape[1]
    n = src_hbm.shape[0] // rows
    ahead = slots - 1

    def copy(c, slot):
        return pltpu.make_async_copy(src_hbm.at[pl.ds(c * rows, rows), :],
                                     stage.at[slot, :, pl.ds(0, cols)], sem.at[slot])

    for c in range(ahead):
        copy(c, c).start()

    def body(c, carry):
        slot = c % slots

        @pl.when(c + ahead < n)
        def _():
            copy(c + ahead, (c + ahead) % slots).start()

        copy(c, slot).wait()
        r0 = pl.multiple_of(c * rows, rows)
        dst[pl.ds(r0, rows), :] = stage[slot, :, 0:cols].astype(BF16)
        return carry

    lax.fori_loop(0, n, body, 0)


def _mix_body(x_ref, xp_ref, mod_ref, modp_ref, g1_ref, g2_ref, win_hbm, wout_hbm, ws_ref, bst_ref, vg_ref,
              cw_ref, cb_ref, wrt_ref, br_ref,
              x1_ref, eid_ref, gcol_ref, h2_hbm,
              pre_scr, mix_scr, h2_scr, win_s, wout_s, stage, h2_sem, w_sem, *, tiles_per_seq):
    tm, d = x_ref.shape
    aw = vg_ref.shape[1]
    bw = cb_ref.shape[1]
    hd = aw // A_HEADS
    nch = tm // CHUNK
    i = pl.program_id(0)
    last = pl.num_programs(0) - 1
    groups = tm // V7X_SUBLANES
    slot = i % 2

    def h2_copies(tile, sl):
        return [pltpu.make_async_copy(h2_scr.at[sl, :, s], h2_hbm.at[pl.ds(tile * groups, groups), :, s, :],
                                      h2_sem.at[sl])
                for s in range(V7X_SUBLANES)]

    @pl.when(i == 0)
    def _():
        mix_scr[...] = jnp.zeros(mix_scr.shape, F32)
        _load_cast(win_hbm, win_s, stage, w_sem)
        _load_cast(wout_hbm, wout_s, stage, w_sem)

    @pl.when(i > 2)
    def _():
        for cp in h2_copies(0, slot):
            cp.wait()

    def last_stage(anchor):
        modp = modp_ref[0]
        gate1p, shift2p, scale2p = modp[:, 2 * d:3 * d], modp[:, 3 * d:4 * d], modp[:, 4 * d:5 * d]
        if anchor is not None:
            gate1p = gate1p + _zero_after(anchor, d)
        x1 = xp_ref[...] + gate1p * mix_scr[...]
        x1_ref[...] = x1
        h2 = _rms(x1, g2_ref[...]) * (1.0 + scale2p) + shift2p
        words = _pack_bf16_pairs(h2)
        for s in range(V7X_SUBLANES):
            h2_scr[slot, :, s] = words[:, s * V7X_LANES:(s + 1) * V7X_LANES].reshape(
                groups, V7X_SUBLANES, V7X_LANES)
        lt = lax.dot_general(wrt_ref[...], h2.astype(BF16), (((1,), (1,)), ((), ())),
                             preferred_element_type=F32) + br_ref[...]
        slab = _route(lt, tm)
        eid_ref[0] = slab[2:4].astype(I32)
        wide = jnp.concatenate([slab, jnp.zeros((V7X_LANES - V7X_SUBLANES, tm), F32)], axis=0)
        gcol_ref[...] = wide.T

    @pl.when(i < last)
    def _():
        x = x_ref[...]
        mod = mod_ref[0]
        shift1, scale1 = mod[:, 0:d], mod[:, d:2 * d]
        h = _rms(x, g1_ref[...]) * (1.0 + scale1) + shift1
        hb = h.astype(BF16)

        uv = jnp.dot(hb, win_s[:, 0:2 * aw], preferred_element_type=F32)
        last_stage(uv[0:1, 0:V7X_LANES])
        uv = jax.nn.gelu(uv)
        u, v = uv[:, :aw], uv[:, aw:]
        tq_r = lax.broadcasted_iota(I32, (CHUNK, CHUNK), 0)
        tq_c = lax.broadcasted_iota(I32, (CHUNK, CHUNK), 1)
        causal = tq_c <= tq_r
        ya = []
        for hh in range(A_HEADS):
            sl = slice(hh * hd, (hh + 1) * hd)
            vh = _rms(v[:, sl], vg_ref[:, sl]).astype(BF16)
            rhs = jnp.concatenate([vh[c * CHUNK:(c + 1) * CHUNK] for c in range(nch)], axis=1)
            w = jnp.where(causal, ws_ref[hh], 0.0).astype(BF16)
            zs = jnp.dot(w, rhs, preferred_element_type=F32) + bst_ref[:, hh:hh + 1]
            zs = jnp.concatenate([zs[:, c * hd:(c + 1) * hd] for c in range(nch)], axis=0)
            ya.append(u[:, sl] * zs)

        bcx = jnp.dot(hb, win_s[:, 2 * aw:], preferred_element_type=F32)
        bg, cg, xin = bcx[:, :bw], bcx[:, bw:2 * bw], bcx[:, 2 * bw:]
        pre = cg * xin

        @pl.when(i % tiles_per_seq == 0)
        def _():
            pre_scr[0:V7X_SUBLANES, :] = jnp.zeros((V7X_SUBLANES, bw), F32)

        pre_scr[V7X_SUBLANES:V7X_SUBLANES + tm, :] = pre
        p1 = pre_scr[V7X_SUBLANES - 1:V7X_SUBLANES - 1 + tm, :]
        p2 = pre_scr[V7X_SUBLANES - 2:V7X_SUBLANES - 2 + tm, :]
        conv = cw_ref[0:1, :] * p2 + cw_ref[1:2, :] * p1 + cw_ref[2:3, :] * pre + cb_ref[...]
        yb = bg * conv
        pre_scr[0:V7X_SUBLANES, :] = pre_scr[tm:tm + V7X_SUBLANES, :]

        y = jnp.concatenate(ya + [yb], axis=1).astype(BF16)
        mix_scr[...] = jnp.dot(y, wout_s[...], preferred_element_type=F32)

    @pl.when(i == last)
    def _():
        last_stage(None)

    @pl.when(i > 0)
    def _():
        for cp in h2_copies(i - 1, slot):
            cp.start()

    @pl.when(i == last)
    def _():
        for sl in range(2):
            for cp in h2_copies(0, sl):
                cp.wait()


def _mix(xt, mod3, g1, g2, win, wout, ws, bst, vg, cw, cb, wrt, br, seq):
    t, d = xt.shape
    tm = TOKEN_TILE
    nt = t // tm
    per_seq = seq // tm
    in_cols = win.shape[1]
    bw = cb.shape[1]
    words_per_row = d // 2
    assert words_per_row == V7X_SUBLANES * V7X_LANES
    assert d % MIX_STAGE_ROWS == 0 and d // MIX_STAGE_ROWS >= MIX_STAGE_SLOTS and wout.shape[1] <= in_cols
    const2 = lambda i: (0, 0)
    body = functools.partial(_mix_body, tiles_per_seq=per_seq)
    cur = lambda i: jnp.minimum(i, nt - 1)
    prev = lambda i: jnp.maximum(i - 1, 0)
    hbm = pl.BlockSpec(memory_space=pl.ANY)
    vmem = (2 * d * (in_cols + d)
            + MIX_STAGE_SLOTS * MIX_STAGE_ROWS * in_cols * 4
            + 3 * 2 * tm * d * 4
            + 2 * tm * d * 2
            + (tm + V7X_SUBLANES) * bw * 4 + tm * d * 4
            + tm * (in_cols + 4 * d) * 4)
    return pl.pallas_call(
        body,
        grid=(nt + 1,),
        in_specs=[
            pl.BlockSpec((tm, d), lambda i: (cur(i), 0)),
            pl.BlockSpec((tm, d), lambda i: (prev(i), 0)),
            pl.BlockSpec((1, 1, mod3.shape[2]), lambda i: (cur(i) // per_seq, 0, 0)),
            pl.BlockSpec((1, 1, mod3.shape[2]), lambda i: (prev(i) // per_seq, 0, 0)),
            pl.BlockSpec((1, d), const2),
            pl.BlockSpec((1, d), const2),
            hbm,
            hbm,
            pl.BlockSpec(ws.shape, lambda i: (0, 0, 0)),
            pl.BlockSpec(bst.shape, const2),
            pl.BlockSpec(vg.shape, const2),
            pl.BlockSpec(cw.shape, const2),
            pl.BlockSpec(cb.shape, const2),
            pl.BlockSpec(wrt.shape, const2),
            pl.BlockSpec(br.shape, const2),
        ],
        out_specs=[
            pl.BlockSpec((tm, d), lambda i: (prev(i), 0)),
            pl.BlockSpec((1, TOP_K, tm), lambda i: (prev(i), 0, 0)),
            pl.BlockSpec((tm, V7X_LANES), lambda i: (prev(i), 0)),
            hbm,
        ],
        out_shape=[
            jax.ShapeDtypeStruct((t, d), F32),
            jax.ShapeDtypeStruct((nt, TOP_K, tm), I32),
            jax.ShapeDtypeStruct((t, V7X_LANES), F32),
            jax.ShapeDtypeStruct((t // V7X_SUBLANES, V7X_SUBLANES, words_per_row // V7X_LANES, V7X_LANES), U32),
        ],
        scratch_shapes=[
            pltpu.VMEM((tm + V7X_SUBLANES, bw), F32),
            pltpu.VMEM((tm, d), F32),
            pltpu.VMEM((2, tm // V7X_SUBLANES, words_per_row // V7X_LANES, V7X_SUBLANES, V7X_LANES), U32),
            pltpu.VMEM(win.shape, BF16),
            pltpu.VMEM(wout.shape, BF16),
            pltpu.VMEM((MIX_STAGE_SLOTS, MIX_STAGE_ROWS, in_cols), F32),
            pltpu.SemaphoreType.DMA((2,)),
            pltpu.SemaphoreType.DMA((MIX_STAGE_SLOTS,)),
        ],
        compiler_params=_params(vmem, ("arbitrary",)),
        name="mix",
    )(xt, xt, mod3, mod3, g1, g2, win, wout, ws, bst, vg, cw, cb, wrt, br)


def _dispatch_body(eid_ref, texp_ref, tab_ref, rank_scr, dest_v, dest_s, init_v, sem,
                   *, n_e, n_tiles, n_tok):
    nch, _, c = eid_ref.shape
    te = EXPERT_TILE
    r = lax.broadcasted_iota(I32, (c, c), 0)
    q = lax.broadcasted_iota(I32, (c, c), 1)
    before = (r < q).astype(BF16)
    e_iota = lax.broadcasted_iota(I32, (n_e, c), 0)

    init_v[...] = (lax.broadcasted_iota(I32, init_v.shape, 0) & (DUMP_TILES * te - 1)) + TOP_K * n_tok
    cp_i = pltpu.make_async_copy(init_v, tab_ref, sem.at[0])
    cp_i.start()

    def count_body(ch, carry):
        e2 = eid_ref[ch]
        ranks = []
        for k in range(TOP_K):
            oh = e_iota == e2[k:k + 1]
            ohf = oh.astype(F32)
            pref = jnp.dot(ohf.astype(BF16), before, preferred_element_type=F32)
            ranks.append(jnp.sum(jnp.where(oh, pref + carry, 0.0), axis=0, keepdims=True))
            carry = carry + jnp.sum(ohf, axis=1, keepdims=True)
        rank_scr[ch] = jnp.concatenate(ranks, axis=0)
        return carry

    counts = lax.fori_loop(0, nch, count_body, jnp.zeros((n_e, 1), F32))

    padded = jnp.floor((counts + float(te - 1)) / float(te)) * float(te)
    sub = lax.broadcasted_iota(I32, (n_e, V7X_LANES), 0)
    lane = lax.broadcasted_iota(I32, (n_e, V7X_LANES), 1)
    pstart_row = jnp.sum(jnp.where(sub < lane, padded, 0.0), axis=0, keepdims=True)
    pstart = jnp.sum(jnp.where(sub == lane, pstart_row, 0.0), axis=1, keepdims=True)
    pend = pstart + padded
    total = jnp.max(pend, axis=0, keepdims=True)
    last_e = jnp.max(jnp.where(counts > 0.0, sub[:, 0:1].astype(F32), -1.0), axis=0, keepdims=True)

    rows_k = n_tok // V7X_LANES
    group = V7X_SUBLANES * V7X_LANES // c

    def dest_body(g, carry):
        rows = [[] for _ in range(TOP_K)]
        for j in range(group):
            e2 = eid_ref[g * group + j]
            rk = rank_scr[g * group + j]
            for k in range(TOP_K):
                oh = e_iota == e2[k:k + 1]
                off = jnp.sum(jnp.where(oh, pstart, 0.0), axis=0, keepdims=True)
                dest = (off + rk[k:k + 1]).astype(I32)
                rows[k] += [dest[:, h * V7X_LANES:(h + 1) * V7X_LANES] for h in range(c // V7X_LANES)]
        for k in range(TOP_K):
            word0 = pl.multiple_of((k * rows_k + g * V7X_SUBLANES) * V7X_LANES, V7X_SUBLANES * V7X_LANES)
            dest_v[pl.ds(word0, V7X_SUBLANES * V7X_LANES)] = jnp.concatenate(rows[k], axis=0).reshape(-1)
        return carry

    lax.fori_loop(0, nch // group, dest_body, 0)
    cp_d = pltpu.make_async_copy(dest_v, dest_s, sem.at[1])
    cp_d.start()

    w = texp_ref.shape[1]
    tj = lax.broadcasted_iota(I32, (n_e, w), 1).astype(F32) * float(te)
    texp = jnp.sum((pend <= tj).astype(F32), axis=0, keepdims=True)
    tj1 = tj[0:1]
    texp = jnp.where(tj1 < total, jnp.minimum(texp, float(n_e - 1)), last_e)
    lane_w = lax.broadcasted_iota(I32, (1, w), 1)
    texp = jnp.where(lane_w == n_tiles, total / float(te), texp)
    texp_ref[...] = texp.astype(I32)

    cp_i.wait()
    cp_d.wait()

    def lane_body(lane, carry):
        for row in range(TOP_K * rows_k):
            a = row * V7X_LANES + lane
            tab_ref[dest_s[a]] = a
        return carry

    lax.fori_loop(0, V7X_LANES, lane_body, 0)


def _dispatch(eid, n_e, n_tiles, n_tok):
    nch, k, c = eid.shape
    assert c == EXPERT_TILE and c % V7X_LANES == 0 and nch * c == n_tok and n_tok % (V7X_SUBLANES * V7X_LANES) == 0
    body = functools.partial(_dispatch_body, n_e=n_e, n_tiles=n_tiles, n_tok=n_tok)
    w = 2 * V7X_LANES
    assert n_tiles < w
    rows = n_tiles + V7X_SUBLANES
    rows += (-rows) % 4
    smem = pl.BlockSpec(memory_space=pltpu.SMEM)
    return pl.pallas_call(
        body,
        out_specs=[pl.BlockSpec(memory_space=pltpu.VMEM), smem],
        out_shape=[
            jax.ShapeDtypeStruct((1, w), I32),
            jax.ShapeDtypeStruct((rows * EXPERT_TILE,), I32),
        ],
        scratch_shapes=[
            pltpu.VMEM((nch, k, c), F32),
            pltpu.VMEM((k * n_tok,), I32),
            pltpu.SMEM((k * n_tok,), I32),
            pltpu.VMEM((rows * EXPERT_TILE,), I32),
            pltpu.SemaphoreType.DMA((2,)),
        ],
        compiler_params=_params(nch * V7X_SUBLANES * c * 4 + (k * n_tok + rows * EXPERT_TILE) * 4 + 4 * c * c * 4),
        name="dispatch",
    )(eid)


def _expert_body(tab_ref, texp_ref, h2_hbm, wgu_hbm, wd_hbm, y2_hbm,
                 x0, x1, x2, y0, y1, y2, wgu_f, wd_f, wgu_s, wd_s, gsem, ssem, wsem, zsem, *, n_tiles):
    te = EXPERT_TILE
    ring = EXPERT_RING
    de = wd_s.shape[0]
    n_tok = h2_hbm.shape[0]
    xs, ys = (x0, x1, x2), (y0, y1, y2)
    p = pl.program_id(0)
    n_used = texp_ref[n_tiles]
    step_tiles = EXPERT_STEP_TILES
    t0 = step_tiles * p

    def gather(tile, buf, sem):
        for r in range(te):
            tok = tab_ref[tile * te + r] & (n_tok - 1)
            pltpu.make_async_copy(h2_hbm.at[tok], buf.at[r // V7X_SUBLANES, :, r % V7X_SUBLANES, :], sem).start()

    def scatter(tile, buf, sem):
        for r in range(te):
            row = tab_ref[tile * te + r]
            pltpu.make_async_copy(buf.at[pl.ds(r, 1), :], y2_hbm.at[pl.ds(row, 1), :], sem).start()

    def wait_rows(buf, sem):
        pltpu.make_async_copy(buf, buf, sem).wait()

    def weight_copies(e):
        return (pltpu.make_async_copy(wgu_hbm.at[e], wgu_f, wsem.at[0]),
                pltpu.make_async_copy(wd_hbm.at[e], wd_f, wsem.at[1]))

    def dump_fill(h):
        rows = pl.ds(TOP_K * n_tok + h * te, te)
        return pltpu.make_async_copy(ys[ring - 1], y2_hbm.at[rows, :], zsem)

    def switch_weights(tile):
        e = texp_ref[tile]
        first = jnp.logical_or(tile == 0, texp_ref[jnp.maximum(tile - 1, 0)] != e)

        @pl.when(jnp.logical_and(first, tile < n_used))
        def _():
            @pl.when(tile == 0)
            def _():
                for cp in weight_copies(e):
                    cp.start(priority=1)

            for cp in weight_copies(e):
                cp.wait()
            wgu_s[...] = wgu_f[...].astype(BF16)
            wd_s[...] = wd_f[...].astype(BF16)
            nxt = lax.while_loop(
                lambda k: jnp.logical_and(k < n_used, texp_ref[jnp.minimum(k, n_tiles - 1)] == e),
                lambda k: k + 1, tile + 1)

            @pl.when(nxt < n_used)
            def _():
                for cp in weight_copies(texp_ref[jnp.minimum(nxt, n_tiles - 1)]):
                    cp.start(priority=1)

    def compute(xbuf, ybuf):
        words = jnp.concatenate([xbuf[:, s].reshape(te, V7X_LANES) for s in range(V7X_SUBLANES)], axis=1)
        xb16 = _unpack_bf16_pairs(words).astype(BF16)
        gu = jnp.dot(xb16, wgu_s[...], preferred_element_type=F32)
        a = (jax.nn.silu(gu[:, :de]) * gu[:, de:]).astype(BF16)
        ybuf[...] = _pack_bf16_pairs(jnp.dot(a, wd_s[...], preferred_element_type=F32))

    @pl.when(t0 < n_used)
    def _():
        @pl.when(p == 0)
        def _():
            ys[ring - 1][...] = jnp.zeros(ys[ring - 1].shape, U32)
            for h in range(DUMP_TILES):
                dump_fill(h).start()
            for h in range(DUMP_TILES):
                dump_fill(h).wait()
            for m in range(ring - 1):
                gather(m, xs[m], gsem.at[m])

        for j in range(step_tiles):
            tile = t0 + j
            m = j % ring
            nm = (m + ring - 1) % ring
            switch_weights(tile)
            wait_rows(xs[m], gsem.at[m])
            if j >= ring - 1:
                wait_rows(ys[m], ssem.at[m])
            else:
                @pl.when(p > 0)
                def _(m=m):
                    wait_rows(ys[m], ssem.at[m])
            gather(tile + ring - 1, xs[nm], gsem.at[nm])
            scatter(jnp.where(tile == 0, n_tiles, tile - 1), ys[nm], ssem.at[nm])
            compute(xs[m], ys[m])

        @pl.when(t0 + step_tiles >= n_used)
        def _():
            for m in range(ring - 1):
                wait_rows(ys[m], ssem.at[m])
            scatter(t0 + step_tiles - 1, ys[ring - 1], ssem.at[ring - 1])
            wait_rows(ys[ring - 1], ssem.at[ring - 1])
            for m in range(ring - 1):
                wait_rows(xs[m], gsem.at[m])


def _experts(tab, texp, h2p, wgu, wd, n_tiles):
    t = h2p.shape[0]
    dp = h2p.shape[1] * h2p.shape[2]
    te = EXPERT_TILE
    _, d, n_gu = wgu.shape
    de = wd.shape[1]
    assert n_tiles % EXPERT_STEP_TILES == 0 and EXPERT_STEP_TILES % EXPERT_RING == 0 and t & (t - 1) == 0
    body = functools.partial(_expert_body, n_tiles=n_tiles)
    vmem = (d * n_gu + de * d) * (4 + 2) + 2 * EXPERT_RING * te * dp * 4 + te * (n_gu + 3 * d) * 4
    hbm = pl.BlockSpec(memory_space=pl.ANY)
    row_buf = pltpu.VMEM((te, dp), U32)
    tile_buf = pltpu.VMEM((te // V7X_SUBLANES,) + h2p.shape[1:2] + (V7X_SUBLANES, V7X_LANES), U32)
    return pl.pallas_call(
        body,
        grid_spec=pltpu.PrefetchScalarGridSpec(
            num_scalar_prefetch=2,
            grid=(n_tiles // EXPERT_STEP_TILES,),
            in_specs=[hbm, hbm, hbm],
            out_specs=hbm,
            scratch_shapes=[tile_buf] * EXPERT_RING + [row_buf] * EXPERT_RING + [
                pltpu.VMEM((d, n_gu), F32), pltpu.VMEM((de, d), F32),
                pltpu.VMEM((d, n_gu), BF16), pltpu.VMEM((de, d), BF16),
                pltpu.SemaphoreType.DMA((EXPERT_RING,)), pltpu.SemaphoreType.DMA((EXPERT_RING,)),
                pltpu.SemaphoreType.DMA((2,)), pltpu.SemaphoreType.DMA,
            ],
        ),
        out_shape=jax.ShapeDtypeStruct((TOP_K * t + DUMP_TILES * te, dp), U32),
        compiler_params=_params(vmem, ("arbitrary",)),
        name="experts",
    )(tab, texp, h2p, wgu, wd)


def _combine_body(x1_ref, y0_ref, y1_ref, gcol_ref, mod_ref, modf_ref, gf_ref, o_ref):
    d = x1_ref.shape[1]
    gate2 = mod_ref[0][:, 5 * d:6 * d]
    modf = modf_ref[0]
    shift_f, scale_f = modf[:, 0:d], modf[:, d:2 * d]
    y0 = _unpack_bf16_pairs(y0_ref[...])
    y1 = _unpack_bf16_pairs(y1_ref[...])
    ffn = y0 * gcol_ref[:, 0:1] + y1 * gcol_ref[:, 1:2]
    x2 = x1_ref[...] + gate2 * ffn
    o_ref[...] = _rms(x2, gf_ref[...]) * (1.0 + scale_f) + shift_f


def _combine(x1, y2, gcol, mod3, modf3, gf, seq):
    t, d = x1.shape
    tm = COMBINE_TILE
    per_seq = seq // tm
    return pl.pallas_call(
        _combine_body,
        grid=(t // tm,),
        in_specs=[
            pl.BlockSpec((tm, d), lambda i: (i, 0)),
            pl.BlockSpec((tm, d // 2), lambda i: (i, 0)),
            pl.BlockSpec((tm, d // 2), lambda i: (t // tm + i, 0)),
            pl.BlockSpec((tm, V7X_LANES), lambda i: (i, 0)),
            pl.BlockSpec((1, 1, mod3.shape[2]), lambda i: (i // per_seq, 0, 0)),
            pl.BlockSpec((1, 1, modf3.shape[2]), lambda i: (i // per_seq, 0, 0)),
            pl.BlockSpec((1, d), lambda i: (0, 0)),
        ],
        out_specs=pl.BlockSpec((tm, d), lambda i: (i, 0)),
        out_shape=jax.ShapeDtypeStruct((t, d), F32),
        compiler_params=_params(2 * tm * (3 * d + V7X_LANES) * 4, ("arbitrary",)),
        name="combine",
    )(x1, y2, y2, gcol, mod3, modf3, gf)


def kernel(x, c, w_ada, b_ada, norm1_g, w_in, w_out, gmlp_w_s, gmlp_b_s, gmlp_v_gain, conv_w, conv_b,
           norm2_g, w_router_group, b_router_group, w_router_expert, b_router_expert, w_gate_up, w_down,
           w_ada_final, b_ada_final, norm_f_g):
    bsz, seq, d = x.shape
    depth = w_ada.shape[0]
    n_e = w_router_expert.shape[2]
    t = bsz * seq
    assert seq % TOKEN_TILE == 0 and TOKEN_TILE % CHUNK == 0 and seq % COMBINE_TILE == 0
    assert w_router_group.shape[2] == N_GROUPS and n_e + N_GROUPS <= ROUTER_ROWS
    n_tiles = (t * TOP_K + n_e * (EXPERT_TILE - 1)) // EXPERT_TILE
    n_tiles = -(-n_tiles // EXPERT_STEP_TILES) * EXPERT_STEP_TILES

    modf3 = _ada(c, w_ada_final, b_ada_final).reshape(bsz, 1, 2 * d)
    xt = x.reshape(t, d)
    for l in range(depth):
        mod3 = _ada(c, w_ada[l], b_ada[l]).reshape(bsz, 1, -1)
        wr = jnp.concatenate([w_router_expert[l], w_router_group[l]], axis=1)
        wrt = jnp.pad(wr, ((0, 0), (0, ROUTER_ROWS - wr.shape[1]))).T.astype(BF16)
        br = jnp.concatenate([b_router_expert[l], b_router_group[l]])
        br = jnp.pad(br, (0, ROUTER_ROWS - br.shape[0])).reshape(ROUTER_ROWS, 1)
        x1, eid, gcol, h2 = _mix(
            xt, mod3, norm1_g[l].reshape(1, d), norm2_g[l].reshape(1, d),
            w_in[l], w_out[l], gmlp_w_s[l], gmlp_b_s[l].T,
            gmlp_v_gain[l].reshape(1, -1), conv_w[l], conv_b[l].reshape(1, -1), wrt, br, seq)
        texp, tab = _dispatch(eid, n_e, n_tiles, t)
        h2 = h2.reshape((t,) + h2.shape[2:])
        y2 = _experts(tab, texp.reshape(-1), h2, w_gate_up[l], w_down[l], n_tiles)
        assert depth == 1
        xt = _combine(x1, y2, gcol, mod3, modf3, norm_f_g.reshape(1, d), seq)
    return xt.reshape(bsz, seq, d)
```

```python
import functools

import jax
import jax.numpy as jnp
from jax import lax
from jax.experimental import pallas as pl
from jax.experimental.pallas import tpu as pltpu

F32 = jnp.float32
BF16 = jnp.bfloat16
I32 = jnp.int32
U32 = jnp.uint32

A_HEADS = 8
CHUNK = 128
N_GROUPS = 4
TOP_K = 2
EPS = 1e-6

V7X_LANES = 128
V7X_SUBLANES = 8
V7X_VMEM_BYTES = 64 * 1024 * 1024
V7X_VMEM_RESERVE = 6 * 1024 * 1024
VMEM_TEMP_ALLOWANCE = 8 * 1024 * 1024

TOKEN_TILE = 256
EXPERT_TILE = 256
COMBINE_TILE = 512
ADA_TILE = 1024
MIX_STAGE_ROWS = 64
MIX_STAGE_SLOTS = 4
ROUTER_ROWS = 40
EXPERT_RING = 3
DUMP_TILES = 2


def _rms(x, g):
    y = x * lax.rsqrt(jnp.mean(x * x, axis=-1, keepdims=True) + EPS)
    return y * g


def _pack_bf16_pairs(x):
    n = x.shape[1] // 2
    lo = lax.bitcast_convert_type(x[:, :n].astype(BF16).astype(F32), U32)
    hi = lax.bitcast_convert_type(x[:, n:].astype(BF16).astype(F32), U32)
    return hi | (lo >> 16)


def _unpack_bf16_pairs(w):
    lo = lax.bitcast_convert_type(w << 16, F32)
    hi = lax.bitcast_convert_type(w & jnp.uint32(0xFFFF0000), F32)
    return jnp.concatenate([lo, hi], axis=1)


def _zero_after(anchor, n):
    bits = lax.bitcast_convert_type(anchor, U32)
    zero = lax.bitcast_convert_type((bits >> 16) >> 16, F32)
    return jnp.concatenate([zero] * (n // anchor.shape[1]), axis=1)


def _params(buffer_bytes, semantics=None):
    vmem_bytes = min(buffer_bytes + VMEM_TEMP_ALLOWANCE, V7X_VMEM_BYTES - V7X_VMEM_RESERVE)
    kw = dict(vmem_limit_bytes=int(vmem_bytes))
    if semantics is not None:
        kw["dimension_semantics"] = semantics
    return pltpu.CompilerParams(**kw)


def _ada_body(c_ref, w_ref, b_ref, o_ref):
    ca = jax.nn.silu(c_ref[...]).astype(BF16)
    o_ref[...] = jnp.dot(ca, w_ref[...].astype(BF16), preferred_element_type=F32) + b_ref[...]


def _ada(c, w, b):
    bsz, d = c.shape
    n = w.shape[1]
    tn = ADA_TILE
    return pl.pallas_call(
        _ada_body,
        grid=(n // tn,),
        in_specs=[
            pl.BlockSpec((bsz, d), lambda j: (0, 0)),
            pl.BlockSpec((d, tn), lambda j: (0, j)),
            pl.BlockSpec((1, tn), lambda j: (0, j)),
        ],
        out_specs=pl.BlockSpec((bsz, tn), lambda j: (0, j)),
        out_shape=jax.ShapeDtypeStruct((bsz, n), F32),
        compiler_params=_params(2 * d * tn * 4 + d * tn * 2, ("arbitrary",)),
        name="ada",
    )(c, w, b.reshape(1, n))


def _route(lt, tm):
    n_e = 32
    epg = n_e // N_GROUPS
    row = lax.broadcasted_iota(I32, (V7X_SUBLANES, tm), 0).astype(F32)
    gl = lt[n_e:n_e + V7X_SUBLANES]
    gvalid = row < float(N_GROUPS)
    glm = jnp.where(gvalid, gl, -jnp.inf)
    gmax = jnp.max(glm, axis=0, keepdims=True)
    garg = jnp.min(jnp.where(glm == gmax, row, float(V7X_SUBLANES)), axis=0, keepdims=True)
    gsum = jnp.sum(jnp.where(gvalid, jnp.exp(gl - gmax), 0.0), axis=0, keepdims=True)
    p_grp = 1.0 / gsum
    es = lt[0:epg]
    for g in range(1, N_GROUPS):
        es = jnp.where(garg == float(g), lt[g * epg:(g + 1) * epg], es)
    m1 = jnp.max(es, axis=0, keepdims=True)
    i1 = jnp.min(jnp.where(es == m1, row, float(epg)), axis=0, keepdims=True)
    es2 = jnp.where(row == i1, -jnp.inf, es)
    m2 = jnp.max(es2, axis=0, keepdims=True)
    i2 = jnp.min(jnp.where(es2 == m2, row, float(epg)), axis=0, keepdims=True)
    z = jnp.exp(m2 - m1)
    den = 1.0 + z
    g0 = p_grp / den
    g1 = p_grp * z / den
    e0 = garg * float(epg) + i1
    e1 = garg * float(epg) + i2
    zero = jnp.zeros_like(g0)
    return jnp.concatenate([g0, g1, e0, e1, zero, zero, zero, zero], axis=0)


def _load_cast(src_hbm, dst, stage, sem):
    slots, rows = stage.shape[0], stage.shape[1]
    cols = src_hbm.shape[1]
    n = src_hbm.shape[0] // rows
    ahead = slots - 1

    def copy(c, slot):
        return pltpu.make_async_copy(src_hbm.at[pl.ds(c * rows, rows), :],
                                     stage.at[slot, :, pl.ds(0, cols)], sem.at[slot])

    for c in range(ahead):
        copy(c, c).start()

    def body(c, carry):
        slot = c % slots

        @pl.when(c + ahead < n)
        def _():
            copy(c + ahead, (c + ahead) % slots).start()

        copy(c, slot).wait()
        r0 = pl.multiple_of(c * rows, rows)
        dst[pl.ds(r0, rows), :] = stage[slot, :, 0:cols].astype(BF16)
        return carry

    lax.fori_loop(0, n, body, 0)


def _mix_body(x_ref, xp_ref, mod_ref, modp_ref, g1_ref, g2_ref, win_hbm, wout_hbm, ws_ref, bst_ref, vg_ref,
              cw_ref, cb_ref, wrt_ref, br_ref,
              x1_ref, eid_ref, gcol_ref, h2_hbm,
              pre_scr, mix_scr, h2_scr, win_s, wout_s, stage, h2_sem, w_sem, *, tiles_per_seq):
    tm, d = x_ref.shape
    aw = vg_ref.shape[1]
    bw = cb_ref.shape[1]
    hd = aw // A_HEADS
    nch = tm // CHUNK
    i = pl.program_id(0)
    last = pl.num_programs(0) - 1
    groups = tm // V7X_SUBLANES
    slot = i % 2

    def h2_copies(tile, sl):
        return [pltpu.make_async_copy(h2_scr.at[sl, :, s], h2_hbm.at[pl.ds(tile * groups, groups), :, s, :],
                                      h2_sem.at[sl])
                for s in range(V7X_SUBLANES)]

    @pl.when(i == 0)
    def _():
        mix_scr[...] = jnp.zeros(mix_scr.shape, F32)
        _load_cast(win_hbm, win_s, stage, w_sem)
        _load_cast(wout_hbm, wout_s, stage, w_sem)

    @pl.when(i > 2)
    def _():
        for cp in h2_copies(0, slot):
            cp.wait()

    def last_stage(anchor):
        modp = modp_ref[0]
        gate1p, shift2p, scale2p = modp[:, 2 * d:3 * d], modp[:, 3 * d:4 * d], modp[:, 4 * d:5 * d]
        if anchor is not None:
            gate1p = gate1p + _zero_after(anchor, d)
        x1 = xp_ref[...] + gate1p * mix_scr[...]
        x1_ref[...] = x1
        h2 = _rms(x1, g2_ref[...]) * (1.0 + scale2p) + shift2p
        words = _pack_bf16_pairs(h2)
        for s in range(V7X_SUBLANES):
            h2_scr[slot, :, s] = words[:, s * V7X_LANES:(s + 1) * V7X_LANES].reshape(
                groups, V7X_SUBLANES, V7X_LANES)
        lt = lax.dot_general(wrt_ref[...], h2.astype(BF16), (((1,), (1,)), ((), ())),
                             preferred_element_type=F32) + br_ref[...]
        slab = _route(lt, tm)
        eid_ref[0] = slab[2:4].astype(I32)
        wide = jnp.concatenate([slab, jnp.zeros((V7X_LANES - V7X_SUBLANES, tm), F32)], axis=0)
        gcol_ref[...] = wide.T

    @pl.when(i < last)
    def _():
        x = x_ref[...]
        mod = mod_ref[0]
        shift1, scale1 = mod[:, 0:d], mod[:, d:2 * d]
        h = _rms(x, g1_ref[...]) * (1.0 + scale1) + shift1
        hb = h.astype(BF16)

        uv = jnp.dot(hb, win_s[:, 0:2 * aw], preferred_element_type=F32)
        last_stage(uv[0:1, 0:V7X_LANES])
        uv = jax.nn.gelu(uv)
        u, v = uv[:, :aw], uv[:, aw:]
        tq_r = lax.broadcasted_iota(I32, (CHUNK, CHUNK), 0)
        tq_c = lax.broadcasted_iota(I32, (CHUNK, CHUNK), 1)
        causal = tq_c <= tq_r
        ya = []
        for hh in range(A_HEADS):
            sl = slice(hh * hd, (hh + 1) * hd)
            vh = _rms(v[:, sl], vg_ref[:, sl]).astype(BF16)
            rhs = jnp.concatenate([vh[c * CHUNK:(c + 1) * CHUNK] for c in range(nch)], axis=1)
            w = jnp.where(causal, ws_ref[hh], 0.0).astype(BF16)
            zs = jnp.dot(w, rhs, preferred_element_type=F32) + bst_ref[:, hh:hh + 1]
            zs = jnp.concatenate([zs[:, c * hd:(c + 1) * hd] for c in range(nch)], axis=0)
            ya.append(u[:, sl] * zs)

        bcx = jnp.dot(hb, win_s[:, 2 * aw:], preferred_element_type=F32)
        bg, cg, xin = bcx[:, :bw], bcx[:, bw:2 * bw], bcx[:, 2 * bw:]
        pre = cg * xin

        @pl.when(i % tiles_per_seq == 0)
        def _():
            pre_scr[0:V7X_SUBLANES, :] = jnp.zeros((V7X_SUBLANES, bw), F32)

        pre_scr[V7X_SUBLANES:V7X_SUBLANES + tm, :] = pre
        p1 = pre_scr[V7X_SUBLANES - 1:V7X_SUBLANES - 1 + tm, :]
        p2 = pre_scr[V7X_SUBLANES - 2:V7X_SUBLANES - 2 + tm, :]
        conv = cw_ref[0:1, :] * p2 + cw_ref[1:2, :] * p1 + cw_ref[2:3, :] * pre + cb_ref[...]
        yb = bg * conv
        pre_scr[0:V7X_SUBLANES, :] = pre_scr[tm:tm + V7X_SUBLANES, :]

        y = jnp.concatenate(ya + [yb], axis=1).astype(BF16)
        mix_scr[...] = jnp.dot(y, wout_s[...], preferred_element_type=F32)

    @pl.when(i == last)
    def _():
        last_stage(None)

    @pl.when(i > 0)
    def _():
        for cp in h2_copies(i - 1, slot):
            cp.start()

    @pl.when(i == last)
    def _():
        for sl in range(2):
            for cp in h2_copies(0, sl):
                cp.wait()


def _mix(xt, mod3, g1, g2, win, wout, ws, bst, vg, cw, cb, wrt, br, seq):
    t, d = xt.shape
    tm = TOKEN_TILE
    nt = t // tm
    per_seq = seq // tm
    in_cols = win.shape[1]
    bw = cb.shape[1]
    words_per_row = d // 2
    assert words_per_row == V7X_SUBLANES * V7X_LANES
    assert d % MIX_STAGE_ROWS == 0 and d // MIX_STAGE_ROWS >= MIX_STAGE_SLOTS and wout.shape[1] <= in_cols
    const2 = lambda i: (0, 0)
    body = functools.partial(_mix_body, tiles_per_seq=per_seq)
    cur = lambda i: jnp.minimum(i, nt - 1)
    prev = lambda i: jnp.maximum(i - 1, 0)
    hbm = pl.BlockSpec(memory_space=pl.ANY)
    vmem = (2 * d * (in_cols + d)
            + MIX_STAGE_SLOTS * MIX_STAGE_ROWS * in_cols * 4
            + 3 * 2 * tm * d * 4
            + 2 * tm * d * 2
            + (tm + V7X_SUBLANES) * bw * 4 + tm * d * 4
            + tm * (in_cols + 4 * d) * 4)
    return pl.pallas_call(
        body,
        grid=(nt + 1,),
        in_specs=[
            pl.BlockSpec((tm, d), lambda i: (cur(i), 0)),
            pl.BlockSpec((tm, d), lambda i: (prev(i), 0)),
            pl.BlockSpec((1, 1, mod3.shape[2]), lambda i: (cur(i) // per_seq, 0, 0)),
            pl.BlockSpec((1, 1, mod3.shape[2]), lambda i: (prev(i) // per_seq, 0, 0)),
            pl.BlockSpec((1, d), const2),
            pl.BlockSpec((1, d), const2),
            hbm,
            hbm,
            pl.BlockSpec(ws.shape, lambda i: (0, 0, 0)),
            pl.BlockSpec(bst.shape, const2),
            pl.BlockSpec(vg.shape, const2),
            pl.BlockSpec(cw.shape, const2),
            pl.BlockSpec(cb.shape, const2),
            pl.BlockSpec(wrt.shape, const2),
            pl.BlockSpec(br.shape, const2),
        ],
        out_specs=[
            pl.BlockSpec((tm, d), lambda i: (prev(i), 0)),
            pl.BlockSpec((1, TOP_K, tm), lambda i: (prev(i), 0, 0)),
            pl.BlockSpec((tm, V7X_LANES), lambda i: (prev(i), 0)),
            hbm,
        ],
        out_shape=[
            jax.ShapeDtypeStruct((t, d), F32),
            jax.ShapeDtypeStruct((nt, TOP_K, tm), I32),
            jax.ShapeDtypeStruct((t, V7X_LANES), F32),
            jax.ShapeDtypeStruct((t // V7X_SUBLANES, V7X_SUBLANES, words_per_row // V7X_LANES, V7X_LANES), U32),
        ],
        scratch_shapes=[
            pltpu.VMEM((tm + V7X_SUBLANES, bw), F32),
            pltpu.VMEM((tm, d), F32),
            pltpu.VMEM((2, tm // V7X_SUBLANES, words_per_row // V7X_LANES, V7X_SUBLANES, V7X_LANES), U32),
            pltpu.VMEM(win.shape, BF16),
            pltpu.VMEM(wout.shape, BF16),
            pltpu.VMEM((MIX_STAGE_SLOTS, MIX_STAGE_ROWS, in_cols), F32),
            pltpu.SemaphoreType.DMA((2,)),
            pltpu.SemaphoreType.DMA((MIX_STAGE_SLOTS,)),
        ],
        compiler_params=_params(vmem, ("arbitrary",)),
        name="mix",
    )(xt, xt, mod3, mod3, g1, g2, win, wout, ws, bst, vg, cw, cb, wrt, br)


def _dispatch_body(eid_ref, texp_ref, tab_ref, rank_scr, dest_v, dest_s, init_v, sem,
                   *, n_e, n_tiles, n_tok):
    nch, _, c = eid_ref.shape
    te = EXPERT_TILE
    r = lax.broadcasted_iota(I32, (c, c), 0)
    q = lax.broadcasted_iota(I32, (c, c), 1)
    before = (r < q).astype(BF16)
    e_iota = lax.broadcasted_iota(I32, (n_e, c), 0)

    init_v[...] = (lax.broadcasted_iota(I32, init_v.shape, 0) & (DUMP_TILES * te - 1)) + TOP_K * n_tok
    cp_i = pltpu.make_async_copy(init_v, tab_ref, sem.at[0])
    cp_i.start()

    def count_body(ch, carry):
        e2 = eid_ref[ch]
        ranks = []
        for k in range(TOP_K):
            oh = e_iota == e2[k:k + 1]
            ohf = oh.astype(F32)
            pref = jnp.dot(ohf.astype(BF16), before, preferred_element_type=F32)
            ranks.append(jnp.sum(jnp.where(oh, pref + carry, 0.0), axis=0, keepdims=True))
            carry = carry + jnp.sum(ohf, axis=1, keepdims=True)
        rank_scr[ch] = jnp.concatenate(ranks, axis=0)
        return carry

    counts = lax.fori_loop(0, nch, count_body, jnp.zeros((n_e, 1), F32))

    padded = jnp.floor((counts + float(te - 1)) / float(te)) * float(te)
    sub = lax.broadcasted_iota(I32, (n_e, V7X_LANES), 0)
    lane = lax.broadcasted_iota(I32, (n_e, V7X_LANES), 1)
    pstart_row = jnp.sum(jnp.where(sub < lane, padded, 0.0), axis=0, keepdims=True)
    pstart = jnp.sum(jnp.where(sub == lane, pstart_row, 0.0), axis=1, keepdims=True)
    pend = pstart + padded
    total = jnp.max(pend, axis=0, keepdims=True)
    last_e = jnp.max(jnp.where(counts > 0.0, sub[:, 0:1].astype(F32), -1.0), axis=0, keepdims=True)

    rows_k = n_tok // V7X_LANES
    group = V7X_SUBLANES * V7X_LANES // c

    def dest_body(g, carry):
        rows = [[] for _ in range(TOP_K)]
        for j in range(group):
            e2 = eid_ref[g * group + j]
            rk = rank_scr[g * group + j]
            for k in range(TOP_K):
                oh = e_iota == e2[k:k + 1]
                off = jnp.sum(jnp.where(oh, pstart, 0.0), axis=0, keepdims=True)
                dest = (off + rk[k:k + 1]).astype(I32)
                rows[k] += [dest[:, h * V7X_LANES:(h + 1) * V7X_LANES] for h in range(c // V7X_LANES)]
        for k in range(TOP_K):
            word0 = pl.multiple_of((k * rows_k + g * V7X_SUBLANES) * V7X_LANES, V7X_SUBLANES * V7X_LANES)
            dest_v[pl.ds(word0, V7X_SUBLANES * V7X_LANES)] = jnp.concatenate(rows[k], axis=0).reshape(-1)
        return carry

    lax.fori_loop(0, nch // group, dest_body, 0)
    cp_d = pltpu.make_async_copy(dest_v, dest_s, sem.at[1])
    cp_d.start()

    w = texp_ref.shape[1]
    tj = lax.broadcasted_iota(I32, (n_e, w), 1).astype(F32) * float(te)
    texp = jnp.sum((pend <= tj).astype(F32), axis=0, keepdims=True)
    tj1 = tj[0:1]
    texp = jnp.where(tj1 < total, jnp.minimum(texp, float(n_e - 1)), last_e)
    lane_w = lax.broadcasted_iota(I32, (1, w), 1)
    texp = jnp.where(lane_w == n_tiles, total / float(te), texp)
    texp_ref[...] = texp.astype(I32)

    cp_i.wait()
    cp_d.wait()

    def lane_body(lane, carry):
        for row in range(TOP_K * rows_k):
            a = row * V7X_LANES + lane
            tab_ref[dest_s[a]] = a
        return carry

    lax.fori_loop(0, V7X_LANES, lane_body, 0)


def _dispatch(eid, n_e, n_tiles, n_tok):
    nch, k, c = eid.shape
    assert c == EXPERT_TILE and c % V7X_LANES == 0 and nch * c == n_tok and n_tok % (V7X_SUBLANES * V7X_LANES) == 0
    body = functools.partial(_dispatch_body, n_e=n_e, n_tiles=n_tiles, n_tok=n_tok)
    w = 2 * V7X_LANES
    assert n_tiles < w
    rows = n_tiles + V7X_SUBLANES
    rows += (-rows) % 4
    smem = pl.BlockSpec(memory_space=pltpu.SMEM)
    return pl.pallas_call(
        body,
        out_specs=[pl.BlockSpec(memory_space=pltpu.VMEM), smem],
        out_shape=[
            jax.ShapeDtypeStruct((1, w), I32),
            jax.ShapeDtypeStruct((rows * EXPERT_TILE,), I32),
        ],
        scratch_shapes=[
            pltpu.VMEM((nch, k, c), F32),
            pltpu.VMEM((k * n_tok,), I32),
            pltpu.SMEM((k * n_tok,), I32),
            pltpu.VMEM((rows * EXPERT_TILE,), I32),
            pltpu.SemaphoreType.DMA((2,)),
        ],
        compiler_params=_params(nch * V7X_SUBLANES * c * 4 + (k * n_tok + rows * EXPERT_TILE) * 4 + 4 * c * c * 4),
        name="dispatch",
    )(eid)


def _expert_body(tab_ref, texp_ref, h2_hbm, wgu_hbm, wd_hbm, y2_hbm,
                 x0, x1, x2, y0, y1, y2, wgu_f, wd_f, wgu_s, wd_s, gsem, ssem, wsem, zsem, *, n_tiles):
    te = EXPERT_TILE
    ring = EXPERT_RING
    de = wd_s.shape[0]
    n_tok = h2_hbm.shape[0]
    xs, ys = (x0, x1, x2), (y0, y1, y2)
    p = pl.program_id(0)
    n_used = texp_ref[n_tiles]
    t0 = ring * p

    def gather(tile, buf, sem):
        for r in range(te):
            tok = tab_ref[tile * te + r] & (n_tok - 1)
            pltpu.make_async_copy(h2_hbm.at[tok], buf.at[r // V7X_SUBLANES, :, r % V7X_SUBLANES, :], sem).start()

    def scatter(tile, buf, sem):
        for r in range(te):
            row = tab_ref[tile * te + r]
            pltpu.make_async_copy(buf.at[pl.ds(r, 1), :], y2_hbm.at[pl.ds(row, 1), :], sem).start()

    def wait_rows(buf, sem):
        pltpu.make_async_copy(buf, buf, sem).wait()

    def weight_copies(e):
        return (pltpu.make_async_copy(wgu_hbm.at[e], wgu_f, wsem.at[0]),
                pltpu.make_async_copy(wd_hbm.at[e], wd_f, wsem.at[1]))

    def dump_fill(h):
        rows = pl.ds(TOP_K * n_tok + h * te, te)
        return pltpu.make_async_copy(ys[ring - 1], y2_hbm.at[rows, :], zsem)

    def switch_weights(tile):
        e = texp_ref[tile]
        first = jnp.logical_or(tile == 0, texp_ref[jnp.maximum(tile - 1, 0)] != e)

        @pl.when(jnp.logical_and(first, tile < n_used))
        def _():
            @pl.when(tile == 0)
            def _():
                for cp in weight_copies(e):
                    cp.start(priority=1)

            for cp in weight_copies(e):
                cp.wait()
            wgu_s[...] = wgu_f[...].astype(BF16)
            wd_s[...] = wd_f[...].astype(BF16)
            nxt = lax.while_loop(
                lambda k: jnp.logical_and(k < n_used, texp_ref[jnp.minimum(k, n_tiles - 1)] == e),
                lambda k: k + 1, tile + 1)

            @pl.when(nxt < n_used)
            def _():
                for cp in weight_copies(texp_ref[jnp.minimum(nxt, n_tiles - 1)]):
                    cp.start(priority=1)

    def compute(xbuf, ybuf):
        words = jnp.concatenate([xbuf[:, s].reshape(te, V7X_LANES) for s in range(V7X_SUBLANES)], axis=1)
        xb16 = _unpack_bf16_pairs(words).astype(BF16)
        gu = jnp.dot(xb16, wgu_s[...], preferred_element_type=F32)
        a = (jax.nn.silu(gu[:, :de]) * gu[:, de:]).astype(BF16)
        ybuf[...] = _pack_bf16_pairs(jnp.dot(a, wd_s[...], preferred_element_type=F32))

    @pl.when(t0 < n_used)
    def _():
        @pl.when(p == 0)
        def _():
            ys[ring - 1][...] = jnp.zeros(ys[ring - 1].shape, U32)
            for h in range(DUMP_TILES):
                dump_fill(h).start()
            for h in range(DUMP_TILES):
                dump_fill(h).wait()
            for m in range(ring - 1):
                gather(m, xs[m], gsem.at[m])

        for m in range(ring):
            tile = t0 + m
            nm = (m + ring - 1) % ring
            switch_weights(tile)
            wait_rows(xs[m], gsem.at[m])
            if m == ring - 1:
                wait_rows(ys[m], ssem.at[m])
            else:
                @pl.when(p > 0)
                def _(m=m):
                    wait_rows(ys[m], ssem.at[m])
            gather(tile + ring - 1, xs[nm], gsem.at[nm])
            scatter(jnp.where(tile == 0, n_tiles, tile - 1), ys[nm], ssem.at[nm])
            compute(xs[m], ys[m])

        @pl.when(t0 + ring >= n_used)
        def _():
            for m in range(ring - 1):
                wait_rows(ys[m], ssem.at[m])
            scatter(t0 + ring - 1, ys[ring - 1], ssem.at[ring - 1])
            wait_rows(ys[ring - 1], ssem.at[ring - 1])
            for m in range(ring - 1):
                wait_rows(xs[m], gsem.at[m])


def _experts(tab, texp, h2p, wgu, wd, n_tiles):
    t = h2p.shape[0]
    dp = h2p.shape[1] * h2p.shape[2]
    te = EXPERT_TILE
    _, d, n_gu = wgu.shape
    de = wd.shape[1]
    assert n_tiles % EXPERT_RING == 0 and t & (t - 1) == 0
    body = functools.partial(_expert_body, n_tiles=n_tiles)
    vmem = (d * n_gu + de * d) * (4 + 2) + 2 * EXPERT_RING * te * dp * 4 + te * (n_gu + 3 * d) * 4
    hbm = pl.BlockSpec(memory_space=pl.ANY)
    row_buf = pltpu.VMEM((te, dp), U32)
    tile_buf = pltpu.VMEM((te // V7X_SUBLANES,) + h2p.shape[1:2] + (V7X_SUBLANES, V7X_LANES), U32)
    return pl.pallas_call(
        body,
        grid_spec=pltpu.PrefetchScalarGridSpec(
            num_scalar_prefetch=2,
            grid=(n_tiles // EXPERT_RING,),
            in_specs=[hbm, hbm, hbm],
            out_specs=hbm,
            scratch_shapes=[tile_buf] * EXPERT_RING + [row_buf] * EXPERT_RING + [
                pltpu.VMEM((d, n_gu), F32), pltpu.VMEM((de, d), F32),
                pltpu.VMEM((d, n_gu), BF16), pltpu.VMEM((de, d), BF16),
                pltpu.SemaphoreType.DMA((EXPERT_RING,)), pltpu.SemaphoreType.DMA((EXPERT_RING,)),
                pltpu.SemaphoreType.DMA((2,)), pltpu.SemaphoreType.DMA,
            ],
        ),
        out_shape=jax.ShapeDtypeStruct((TOP_K * t + DUMP_TILES * te, dp), U32),
        compiler_params=_params(vmem, ("arbitrary",)),
        name="experts",
    )(tab, texp, h2p, wgu, wd)


def _combine_body(x1_ref, y0_ref, y1_ref, gcol_ref, mod_ref, modf_ref, gf_ref, o_ref):
    d = x1_ref.shape[1]
    gate2 = mod_ref[0][:, 5 * d:6 * d]
    modf = modf_ref[0]
    shift_f, scale_f = modf[:, 0:d], modf[:, d:2 * d]
    y0 = _unpack_bf16_pairs(y0_ref[...])
    y1 = _unpack_bf16_pairs(y1_ref[...])
    ffn = y0 * gcol_ref[:, 0:1] + y1 * gcol_ref[:, 1:2]
    x2 = x1_ref[...] + gate2 * ffn
    o_ref[...] = _rms(x2, gf_ref[...]) * (1.0 + scale_f) + shift_f


def _combine(x1, y2, gcol, mod3, modf3, gf, seq):
    t, d = x1.shape
    tm = COMBINE_TILE
    per_seq = seq // tm
    return pl.pallas_call(
        _combine_body,
        grid=(t // tm,),
        in_specs=[
            pl.BlockSpec((tm, d), lambda i: (i, 0)),
            pl.BlockSpec((tm, d // 2), lambda i: (i, 0)),
            pl.BlockSpec((tm, d // 2), lambda i: (t // tm + i, 0)),
            pl.BlockSpec((tm, V7X_LANES), lambda i: (i, 0)),
            pl.BlockSpec((1, 1, mod3.shape[2]), lambda i: (i // per_seq, 0, 0)),
            pl.BlockSpec((1, 1, modf3.shape[2]), lambda i: (i // per_seq, 0, 0)),
            pl.BlockSpec((1, d), lambda i: (0, 0)),
        ],
        out_specs=pl.BlockSpec((tm, d), lambda i: (i, 0)),
        out_shape=jax.ShapeDtypeStruct((t, d), F32),
        compiler_params=_params(2 * tm * (3 * d + V7X_LANES) * 4, ("arbitrary",)),
        name="combine",
    )(x1, y2, y2, gcol, mod3, modf3, gf)


def kernel(x, c, w_ada, b_ada, norm1_g, w_in, w_out, gmlp_w_s, gmlp_b_s, gmlp_v_gain, conv_w, conv_b,
           norm2_g, w_router_group, b_router_group, w_router_expert, b_router_expert, w_gate_up, w_down,
           w_ada_final, b_ada_final, norm_f_g):
    bsz, seq, d = x.shape
    depth = w_ada.shape[0]
    n_e = w_router_expert.shape[2]
    t = bsz * seq
    assert seq % TOKEN_TILE == 0 and TOKEN_TILE % CHUNK == 0 and seq % COMBINE_TILE == 0
    assert w_router_group.shape[2] == N_GROUPS and n_e + N_GROUPS <= ROUTER_ROWS
    n_tiles = (t * TOP_K + n_e * (EXPERT_TILE - 1)) // EXPERT_TILE
    n_tiles = -(-n_tiles // EXPERT_RING) * EXPERT_RING

    modf3 = _ada(c, w_ada_final, b_ada_final).reshape(bsz, 1, 2 * d)
    xt = x.reshape(t, d)
    for l in range(depth):
        mod3 = _ada(c, w_ada[l], b_ada[l]).reshape(bsz, 1, -1)
        wr = jnp.concatenate([w_router_expert[l], w_router_group[l]], axis=1)
        wrt = jnp.pad(wr, ((0, 0), (0, ROUTER_ROWS - wr.shape[1]))).T.astype(BF16)
        br = jnp.concatenate([b_router_expert[l], b_router_group[l]])
        br = jnp.pad(br, (0, ROUTER_ROWS - br.shape[0])).reshape(ROUTER_ROWS, 1)
        x1, eid, gcol, h2 = _mix(
            xt, mod3, norm1_g[l].reshape(1, d), norm2_g[l].reshape(1, d),
            w_in[l], w_out[l], gmlp_w_s[l], gmlp_b_s[l].T,
            gmlp_v_gain[l].reshape(1, -1), conv_w[l], conv_b[l].reshape(1, -1), wrt, br, seq)
        texp, tab = _dispatch(eid, n_e, n_tiles, t)
        h2 = h2.reshape((t,) + h2.shape[2:])
        y2 = _experts(tab, texp.reshape(-1), h2, w_gate_up[l], w_down[l], n_tiles)
        assert depth == 1
        xt = _combine(x1, y2, gcol, mod3, modf3, norm_f_g.reshape(1, d), seq)
    return xt.reshape(bsz, seq, d)
```

```python
import functools

import jax
import jax.numpy as jnp
from jax import lax
from jax.experimental import pallas as pl
from jax.experimental.pallas import tpu as pltpu

F32 = jnp.float32
BF16 = jnp.bfloat16
I32 = jnp.int32
U32 = jnp.uint32

A_HEADS = 8
CHUNK = 128
N_GROUPS = 4
TOP_K = 2
EPS = 1e-6

V7X_LANES = 128
V7X_SUBLANES = 8
V7X_VMEM_BYTES = 64 * 1024 * 1024
V7X_VMEM_RESERVE = 6 * 1024 * 1024
VMEM_TEMP_ALLOWANCE = 8 * 1024 * 1024

TOKEN_TILE = 256
EXPERT_TILE = 256
COMBINE_TILE = 512
ADA_TILE = 1024
MIX_STAGE_ROWS = 64
MIX_STAGE_SLOTS = 4
ROUTER_ROWS = 40
COUNT_UNROLL = 16
EXPERT_RING = 3
DUMP_TILES = 2


def _rms(x, g):
    y = x * lax.rsqrt(jnp.mean(x * x, axis=-1, keepdims=True) + EPS)
    return y * g


def _pack_bf16_pairs(x):
    n = x.shape[1] // 2
    lo = lax.bitcast_convert_type(x[:, :n].astype(BF16).astype(F32), U32)
    hi = lax.bitcast_convert_type(x[:, n:].astype(BF16).astype(F32), U32)
    return hi | (lo >> 16)


def _unpack_bf16_pairs(w):
    lo = lax.bitcast_convert_type(w << 16, F32)
    hi = lax.bitcast_convert_type(w & jnp.uint32(0xFFFF0000), F32)
    return jnp.concatenate([lo, hi], axis=1)


def _zero_after(anchor, n):
    bits = lax.bitcast_convert_type(anchor, U32)
    zero = lax.bitcast_convert_type((bits >> 16) >> 16, F32)
    return jnp.concatenate([zero] * (n // anchor.shape[1]), axis=1)


def _params(buffer_bytes, semantics=None):
    vmem_bytes = min(buffer_bytes + VMEM_TEMP_ALLOWANCE, V7X_VMEM_BYTES - V7X_VMEM_RESERVE)
    kw = dict(vmem_limit_bytes=int(vmem_bytes))
    if semantics is not None:
        kw["dimension_semantics"] = semantics
    return pltpu.CompilerParams(**kw)


def _ada_body(c_ref, w_ref, b_ref, o_ref):
    ca = jax.nn.silu(c_ref[...]).astype(BF16)
    o_ref[...] = jnp.dot(ca, w_ref[...].astype(BF16), preferred_element_type=F32) + b_ref[...]


def _ada(c, w, b):
    bsz, d = c.shape
    n = w.shape[1]
    tn = ADA_TILE
    return pl.pallas_call(
        _ada_body,
        grid=(n // tn,),
        in_specs=[
            pl.BlockSpec((bsz, d), lambda j: (0, 0)),
            pl.BlockSpec((d, tn), lambda j: (0, j)),
            pl.BlockSpec((1, tn), lambda j: (0, j)),
        ],
        out_specs=pl.BlockSpec((bsz, tn), lambda j: (0, j)),
        out_shape=jax.ShapeDtypeStruct((bsz, n), F32),
        compiler_params=_params(2 * d * tn * 4 + d * tn * 2, ("arbitrary",)),
        name="ada",
    )(c, w, b.reshape(1, n))


def _route(lt, tm):
    n_e = lt.shape[0] - V7X_SUBLANES
    epg = n_e // N_GROUPS
    row = lax.broadcasted_iota(I32, (V7X_SUBLANES, tm), 0).astype(F32)
    gl = lt[n_e:n_e + V7X_SUBLANES]
    gvalid = row < float(N_GROUPS)
    glm = jnp.where(gvalid, gl, -jnp.inf)
    gmax = jnp.max(glm, axis=0, keepdims=True)
    garg = jnp.min(jnp.where(glm == gmax, row, float(V7X_SUBLANES)), axis=0, keepdims=True)
    gsum = jnp.sum(jnp.where(gvalid, jnp.exp(gl - gmax), 0.0), axis=0, keepdims=True)
    p_grp = 1.0 / gsum
    es = lt[0:epg]
    for g in range(1, N_GROUPS):
        es = jnp.where(garg == float(g), lt[g * epg:(g + 1) * epg], es)
    m1 = jnp.max(es, axis=0, keepdims=True)
    i1 = jnp.min(jnp.where(es == m1, row, float(epg)), axis=0, keepdims=True)
    es2 = jnp.where(row == i1, -jnp.inf, es)
    m2 = jnp.max(es2, axis=0, keepdims=True)
    i2 = jnp.min(jnp.where(es2 == m2, row, float(epg)), axis=0, keepdims=True)
    z = jnp.exp(m2 - m1)
    den = 1.0 + z
    g0 = p_grp / den
    g1 = p_grp * z / den
    e0 = garg * float(epg) + i1
    e1 = garg * float(epg) + i2
    zero = jnp.zeros_like(g0)
    return jnp.concatenate([g0, g1, e0, e1, zero, zero, zero, zero], axis=0)


def _load_cast(src_hbm, dst, stage, sem):
    slots, rows = stage.shape[0], stage.shape[1]
    cols = src_hbm.shape[1]
    n = src_hbm.shape[0] // rows
    ahead = slots - 1

    def copy(c, slot):
        return pltpu.make_async_copy(src_hbm.at[pl.ds(c * rows, rows), :],
                                     stage.at[slot, :, pl.ds(0, cols)], sem.at[slot])

    for c in range(ahead):
        copy(c, c).start()

    def body(c, carry):
        slot = c % slots

        @pl.when(c + ahead < n)
        def _():
            copy(c + ahead, (c + ahead) % slots).start()

        copy(c, slot).wait()
        r0 = pl.multiple_of(c * rows, rows)
        dst[pl.ds(r0, rows), :] = stage[slot, :, 0:cols].astype(BF16)
        return carry

    lax.fori_loop(0, n, body, 0)


def _mix_body(x_ref, xp_ref, mod_ref, modp_ref, g1_ref, g2_ref, win_hbm, wout_hbm, ws_ref, bst_ref, vg_ref,
              cw_ref, cb_ref, wrt_ref, br_ref,
              x1_ref, eid_ref, gcol_ref, h2_hbm,
              pre_scr, mix_scr, h2_scr, win_s, wout_s, stage, h2_sem, w_sem, *, tiles_per_seq):
    tm, d = x_ref.shape
    aw = vg_ref.shape[1]
    bw = cb_ref.shape[1]
    hd = aw // A_HEADS
    nch = tm // CHUNK
    i = pl.program_id(0)
    last = pl.num_programs(0) - 1
    groups = tm // V7X_SUBLANES
    slot = i % 2

    def h2_copies(tile, sl):
        return [pltpu.make_async_copy(h2_scr.at[sl, :, s], h2_hbm.at[pl.ds(tile * groups, groups), :, s, :],
                                      h2_sem.at[sl])
                for s in range(V7X_SUBLANES)]

    @pl.when(i == 0)
    def _():
        mix_scr[...] = jnp.zeros(mix_scr.shape, F32)
        _load_cast(win_hbm, win_s, stage, w_sem)
        _load_cast(wout_hbm, wout_s, stage, w_sem)

    @pl.when(i > 2)
    def _():
        for cp in h2_copies(0, slot):
            cp.wait()

    def last_stage(anchor):
        modp = modp_ref[0]
        gate1p, shift2p, scale2p = modp[:, 2 * d:3 * d], modp[:, 3 * d:4 * d], modp[:, 4 * d:5 * d]
        if anchor is not None:
            gate1p = gate1p + _zero_after(anchor, d)
        x1 = xp_ref[...] + gate1p * mix_scr[...]
        x1_ref[...] = x1
        h2 = _rms(x1, g2_ref[...]) * (1.0 + scale2p) + shift2p
        words = _pack_bf16_pairs(h2)
        for s in range(V7X_SUBLANES):
            h2_scr[slot, :, s] = words[:, s * V7X_LANES:(s + 1) * V7X_LANES].reshape(
                groups, V7X_SUBLANES, V7X_LANES)
        lt = lax.dot_general(wrt_ref[...], h2.astype(BF16), (((1,), (1,)), ((), ())),
                             preferred_element_type=F32) + br_ref[...]
        slab = _route(lt, tm)
        eid_ref[0] = slab[2:4].astype(I32)
        wide = jnp.concatenate([slab, jnp.zeros((V7X_LANES - V7X_SUBLANES, tm), F32)], axis=0)
        gcol_ref[...] = wide.T

    @pl.when(i < last)
    def _():
        x = x_ref[...]
        mod = mod_ref[0]
        shift1, scale1 = mod[:, 0:d], mod[:, d:2 * d]
        h = _rms(x, g1_ref[...]) * (1.0 + scale1) + shift1
        hb = h.astype(BF16)

        uv = jnp.dot(hb, win_s[:, 0:2 * aw], preferred_element_type=F32)
        last_stage(uv[0:1, 0:V7X_LANES])
        uv = jax.nn.gelu(uv)
        u, v = uv[:, :aw], uv[:, aw:]
        tq_r = lax.broadcasted_iota(I32, (CHUNK, CHUNK), 0)
        tq_c = lax.broadcasted_iota(I32, (CHUNK, CHUNK), 1)
        causal = tq_c <= tq_r
        ya = []
        for hh in range(A_HEADS):
            sl = slice(hh * hd, (hh + 1) * hd)
            vh = _rms(v[:, sl], vg_ref[:, sl]).astype(BF16)
            rhs = jnp.concatenate([vh[c * CHUNK:(c + 1) * CHUNK] for c in range(nch)], axis=1)
            w = jnp.where(causal, ws_ref[hh], 0.0).astype(BF16)
            zs = jnp.dot(w, rhs, preferred_element_type=F32) + bst_ref[:, hh:hh + 1]
            zs = jnp.concatenate([zs[:, c * hd:(c + 1) * hd] for c in range(nch)], axis=0)
            ya.append(u[:, sl] * zs)

        bcx = jnp.dot(hb, win_s[:, 2 * aw:], preferred_element_type=F32)
        bg, cg, xin = bcx[:, :bw], bcx[:, bw:2 * bw], bcx[:, 2 * bw:]
        pre = cg * xin

        @pl.when(i % tiles_per_seq == 0)
        def _():
            pre_scr[0:V7X_SUBLANES, :] = jnp.zeros((V7X_SUBLANES, bw), F32)

        pre_scr[V7X_SUBLANES:V7X_SUBLANES + tm, :] = pre
        p1 = pre_scr[V7X_SUBLANES - 1:V7X_SUBLANES - 1 + tm, :]
        p2 = pre_scr[V7X_SUBLANES - 2:V7X_SUBLANES - 2 + tm, :]
        conv = cw_ref[0:1, :] * p2 + cw_ref[1:2, :] * p1 + cw_ref[2:3, :] * pre + cb_ref[...]
        yb = bg * conv
        pre_scr[0:V7X_SUBLANES, :] = pre_scr[tm:tm + V7X_SUBLANES, :]

        y = jnp.concatenate(ya + [yb], axis=1).astype(BF16)
        mix_scr[...] = jnp.dot(y, wout_s[...], preferred_element_type=F32)

    @pl.when(i == last)
    def _():
        last_stage(None)

    @pl.when(i > 0)
    def _():
        for cp in h2_copies(i - 1, slot):
            cp.start()

    @pl.when(i == last)
    def _():
        for sl in range(2):
            for cp in h2_copies(0, sl):
                cp.wait()


def _mix(xt, mod3, g1, g2, win, wout, ws, bst, vg, cw, cb, wrt, br, seq):
    t, d = xt.shape
    tm = TOKEN_TILE
    nt = t // tm
    per_seq = seq // tm
    in_cols = win.shape[1]
    bw = cb.shape[1]
    words_per_row = d // 2
    assert words_per_row == V7X_SUBLANES * V7X_LANES
    assert d % MIX_STAGE_ROWS == 0 and d // MIX_STAGE_ROWS >= MIX_STAGE_SLOTS and wout.shape[1] <= in_cols
    const2 = lambda i: (0, 0)
    body = functools.partial(_mix_body, tiles_per_seq=per_seq)
    cur = lambda i: jnp.minimum(i, nt - 1)
    prev = lambda i: jnp.maximum(i - 1, 0)
    hbm = pl.BlockSpec(memory_space=pl.ANY)
    vmem = (2 * d * (in_cols + d)
            + MIX_STAGE_SLOTS * MIX_STAGE_ROWS * in_cols * 4
            + 3 * 2 * tm * d * 4
            + 2 * tm * d * 2
            + (tm + V7X_SUBLANES) * bw * 4 + tm * d * 4
            + tm * (in_cols + 4 * d) * 4)
    return pl.pallas_call(
        body,
        grid=(nt + 1,),
        in_specs=[
            pl.BlockSpec((tm, d), lambda i: (cur(i), 0)),
            pl.BlockSpec((tm, d), lambda i: (prev(i), 0)),
            pl.BlockSpec((1, 1, mod3.shape[2]), lambda i: (cur(i) // per_seq, 0, 0)),
            pl.BlockSpec((1, 1, mod3.shape[2]), lambda i: (prev(i) // per_seq, 0, 0)),
            pl.BlockSpec((1, d), const2),
            pl.BlockSpec((1, d), const2),
            hbm,
            hbm,
            pl.BlockSpec(ws.shape, lambda i: (0, 0, 0)),
            pl.BlockSpec(bst.shape, const2),
            pl.BlockSpec(vg.shape, const2),
            pl.BlockSpec(cw.shape, const2),
            pl.BlockSpec(cb.shape, const2),
            pl.BlockSpec(wrt.shape, const2),
            pl.BlockSpec(br.shape, const2),
        ],
        out_specs=[
            pl.BlockSpec((tm, d), lambda i: (prev(i), 0)),
            pl.BlockSpec((1, TOP_K, tm), lambda i: (prev(i), 0, 0)),
            pl.BlockSpec((tm, V7X_LANES), lambda i: (prev(i), 0)),
            hbm,
        ],
        out_shape=[
            jax.ShapeDtypeStruct((t, d), F32),
            jax.ShapeDtypeStruct((nt, TOP_K, tm), I32),
            jax.ShapeDtypeStruct((t, V7X_LANES), F32),
            jax.ShapeDtypeStruct((t // V7X_SUBLANES, V7X_SUBLANES, words_per_row // V7X_LANES, V7X_LANES), U32),
        ],
        scratch_shapes=[
            pltpu.VMEM((tm + V7X_SUBLANES, bw), F32),
            pltpu.VMEM((tm, d), F32),
            pltpu.VMEM((2, tm // V7X_SUBLANES, words_per_row // V7X_LANES, V7X_SUBLANES, V7X_LANES), U32),
            pltpu.VMEM(win.shape, BF16),
            pltpu.VMEM(wout.shape, BF16),
            pltpu.VMEM((MIX_STAGE_SLOTS, MIX_STAGE_ROWS, in_cols), F32),
            pltpu.SemaphoreType.DMA((2,)),
            pltpu.SemaphoreType.DMA((MIX_STAGE_SLOTS,)),
        ],
        compiler_params=_params(vmem, ("arbitrary",)),
        name="mix",
    )(xt, xt, mod3, mod3, g1, g2, win, wout, ws, bst, vg, cw, cb, wrt, br)


def _dispatch_body(eid_ref, texp_ref, tab_ref, rank_scr, dest_v, dest_s, init_v, sem,
                   *, n_e, n_tiles, n_tok):
    nch, _, c = eid_ref.shape
    te = EXPERT_TILE
    r = lax.broadcasted_iota(I32, (c, c), 0)
    q = lax.broadcasted_iota(I32, (c, c), 1)
    before = (r < q).astype(BF16)
    e_iota = lax.broadcasted_iota(I32, (n_e, c), 0)

    init_v[...] = (lax.broadcasted_iota(I32, init_v.shape, 0) & (DUMP_TILES * te - 1)) + TOP_K * n_tok
    cp_i = pltpu.make_async_copy(init_v, tab_ref, sem.at[0])
    cp_i.start()

    def count_body(ch, carry):
        e2 = eid_ref[ch]
        ranks = []
        for k in range(TOP_K):
            oh = e_iota == e2[k:k + 1]
            ohf = oh.astype(F32)
            pref = jnp.dot(ohf.astype(BF16), before, preferred_element_type=F32)
            ranks.append(jnp.sum(jnp.where(oh, pref + carry, 0.0), axis=0, keepdims=True))
            carry = carry + jnp.sum(ohf, axis=1, keepdims=True)
        rank_scr[ch] = jnp.concatenate(ranks, axis=0)
        return carry

    counts = lax.fori_loop(0, nch, count_body, jnp.zeros((n_e, 1), F32), unroll=COUNT_UNROLL)

    padded = jnp.floor((counts + float(te - 1)) / float(te)) * float(te)
    sub = lax.broadcasted_iota(I32, (n_e, V7X_LANES), 0)
    lane = lax.broadcasted_iota(I32, (n_e, V7X_LANES), 1)
    pstart_row = jnp.sum(jnp.where(sub < lane, padded, 0.0), axis=0, keepdims=True)
    pstart = jnp.sum(jnp.where(sub == lane, pstart_row, 0.0), axis=1, keepdims=True)
    pend = pstart + padded
    total = jnp.max(pend, axis=0, keepdims=True)
    last_e = jnp.max(jnp.where(counts > 0.0, sub[:, 0:1].astype(F32), -1.0), axis=0, keepdims=True)

    rows_k = n_tok // V7X_LANES
    group = V7X_SUBLANES * V7X_LANES // c

    def dest_body(g, carry):
        rows = [[] for _ in range(TOP_K)]
        for j in range(group):
            e2 = eid_ref[g * group + j]
            rk = rank_scr[g * group + j]
            for k in range(TOP_K):
                oh = e_iota == e2[k:k + 1]
                off = jnp.sum(jnp.where(oh, pstart, 0.0), axis=0, keepdims=True)
                dest = (off + rk[k:k + 1]).astype(I32)
                rows[k] += [dest[:, h * V7X_LANES:(h + 1) * V7X_LANES] for h in range(c // V7X_LANES)]
        for k in range(TOP_K):
            word0 = pl.multiple_of((k * rows_k + g * V7X_SUBLANES) * V7X_LANES, V7X_SUBLANES * V7X_LANES)
            dest_v[pl.ds(word0, V7X_SUBLANES * V7X_LANES)] = jnp.concatenate(rows[k], axis=0).reshape(-1)
        return carry

    lax.fori_loop(0, nch // group, dest_body, 0)
    cp_d = pltpu.make_async_copy(dest_v, dest_s, sem.at[1])
    cp_d.start()

    w = texp_ref.shape[1]
    tj = lax.broadcasted_iota(I32, (n_e, w), 1).astype(F32) * float(te)
    texp = jnp.sum((pend <= tj).astype(F32), axis=0, keepdims=True)
    tj1 = tj[0:1]
    texp = jnp.where(tj1 < total, jnp.minimum(texp, float(n_e - 1)), last_e)
    lane_w = lax.broadcasted_iota(I32, (1, w), 1)
    texp = jnp.where(lane_w == n_tiles, total / float(te), texp)
    texp_ref[...] = texp.astype(I32)

    cp_i.wait()
    cp_d.wait()

    def lane_body(lane, carry):
        for row in range(TOP_K * rows_k):
            a = row * V7X_LANES + lane
            tab_ref[dest_s[a]] = a
        return carry

    lax.fori_loop(0, V7X_LANES, lane_body, 0)


def _dispatch(eid, n_e, n_tiles, n_tok):
    nch, k, c = eid.shape
    assert c == EXPERT_TILE and c % V7X_LANES == 0 and nch * c == n_tok and n_tok % (V7X_SUBLANES * V7X_LANES) == 0
    body = functools.partial(_dispatch_body, n_e=n_e, n_tiles=n_tiles, n_tok=n_tok)
    w = 2 * V7X_LANES
    assert n_tiles < w
    rows = n_tiles + V7X_SUBLANES
    rows += (-rows) % 4
    smem = pl.BlockSpec(memory_space=pltpu.SMEM)
    return pl.pallas_call(
        body,
        out_specs=[pl.BlockSpec(memory_space=pltpu.VMEM), smem],
        out_shape=[
            jax.ShapeDtypeStruct((1, w), I32),
            jax.ShapeDtypeStruct((rows * EXPERT_TILE,), I32),
        ],
        scratch_shapes=[
            pltpu.VMEM((nch, k, c), F32),
            pltpu.VMEM((k * n_tok,), I32),
            pltpu.SMEM((k * n_tok,), I32),
            pltpu.VMEM((rows * EXPERT_TILE,), I32),
            pltpu.SemaphoreType.DMA((2,)),
        ],
        compiler_params=_params(nch * V7X_SUBLANES * c * 4 + (k * n_tok + rows * EXPERT_TILE) * 4 + 4 * c * c * 4),
        name="dispatch",
    )(eid)


def _expert_body(tab_ref, texp_ref, h2_hbm, wgu_hbm, wd_hbm, y2_hbm,
                 x0, x1, x2, y0, y1, y2, wgu_f, wd_f, wgu_s, wd_s, gsem, ssem, wsem, zsem, *, n_tiles):
    te = EXPERT_TILE
    ring = EXPERT_RING
    de = wd_s.shape[0]
    n_tok = h2_hbm.shape[0]
    xs, ys = (x0, x1, x2), (y0, y1, y2)
    p = pl.program_id(0)
    n_used = texp_ref[n_tiles]
    t0 = ring * p

    def gather(tile, buf, sem):
        for r in range(te):
            tok = tab_ref[tile * te + r] & (n_tok - 1)
            pltpu.make_async_copy(h2_hbm.at[tok], buf.at[r // V7X_SUBLANES, :, r % V7X_SUBLANES, :], sem).start()

    def scatter(tile, buf, sem):
        for r in range(te):
            row = tab_ref[tile * te + r]
            pltpu.make_async_copy(buf.at[pl.ds(r, 1), :], y2_hbm.at[pl.ds(row, 1), :], sem).start()

    def wait_rows(buf, sem):
        pltpu.make_async_copy(buf, buf, sem).wait()

    def weight_copies(e):
        return (pltpu.make_async_copy(wgu_hbm.at[e], wgu_f, wsem.at[0]),
                pltpu.make_async_copy(wd_hbm.at[e], wd_f, wsem.at[1]))

    def dump_fill(h):
        rows = pl.ds(TOP_K * n_tok + h * te, te)
        return pltpu.make_async_copy(ys[ring - 1], y2_hbm.at[rows, :], zsem)

    def switch_weights(tile):
        e = texp_ref[tile]
        first = jnp.logical_or(tile == 0, texp_ref[jnp.maximum(tile - 1, 0)] != e)

        @pl.when(jnp.logical_and(first, tile < n_used))
        def _():
            @pl.when(tile == 0)
            def _():
                for cp in weight_copies(e):
                    cp.start(priority=1)

            for cp in weight_copies(e):
                cp.wait()
            wgu_s[...] = wgu_f[...].astype(BF16)
            wd_s[...] = wd_f[...].astype(BF16)
            nxt = lax.while_loop(
                lambda k: jnp.logical_and(k < n_used, texp_ref[jnp.minimum(k, n_tiles - 1)] == e),
                lambda k: k + 1, tile + 1)

            @pl.when(nxt < n_used)
            def _():
                for cp in weight_copies(texp_ref[jnp.minimum(nxt, n_tiles - 1)]):
                    cp.start(priority=1)

    def compute(xbuf, ybuf):
        words = jnp.concatenate([xbuf[:, s].reshape(te, V7X_LANES) for s in range(V7X_SUBLANES)], axis=1)
        xb16 = _unpack_bf16_pairs(words).astype(BF16)
        gu = jnp.dot(xb16, wgu_s[...], preferred_element_type=F32)
        a = (jax.nn.silu(gu[:, :de]) * gu[:, de:]).astype(BF16)
        ybuf[...] = _pack_bf16_pairs(jnp.dot(a, wd_s[...], preferred_element_type=F32))

    @pl.when(t0 < n_used)
    def _():
        @pl.when(p == 0)
        def _():
            ys[ring - 1][...] = jnp.zeros(ys[ring - 1].shape, U32)
            for h in range(DUMP_TILES):
                dump_fill(h).start()
            for h in range(DUMP_TILES):
                dump_fill(h).wait()
            for m in range(ring - 1):
                gather(m, xs[m], gsem.at[m])

        for m in range(ring):
            tile = t0 + m
            nm = (m + ring - 1) % ring
            switch_weights(tile)
            wait_rows(xs[m], gsem.at[m])
            if m == ring - 1:
                wait_rows(ys[m], ssem.at[m])
            else:
                @pl.when(p > 0)
                def _(m=m):
                    wait_rows(ys[m], ssem.at[m])
            gather(tile + ring - 1, xs[nm], gsem.at[nm])
            scatter(jnp.where(tile == 0, n_tiles, tile - 1), ys[nm], ssem.at[nm])
            compute(xs[m], ys[m])

        @pl.when(t0 + ring >= n_used)
        def _():
            for m in range(ring - 1):
                wait_rows(ys[m], ssem.at[m])
            scatter(t0 + ring - 1, ys[ring - 1], ssem.at[ring - 1])
            wait_rows(ys[ring - 1], ssem.at[ring - 1])
            for m in range(ring - 1):
                wait_rows(xs[m], gsem.at[m])


def _experts(tab, texp, h2p, wgu, wd, n_tiles):
    t = h2p.shape[0]
    dp = h2p.shape[1] * h2p.shape[2]
    te = EXPERT_TILE
    _, d, n_gu = wgu.shape
    de = wd.shape[1]
    assert n_tiles % EXPERT_RING == 0 and t & (t - 1) == 0
    body = functools.partial(_expert_body, n_tiles=n_tiles)
    vmem = (d * n_gu + de * d) * (4 + 2) + 2 * EXPERT_RING * te * dp * 4 + te * (n_gu + 3 * d) * 4
    hbm = pl.BlockSpec(memory_space=pl.ANY)
    row_buf = pltpu.VMEM((te, dp), U32)
    tile_buf = pltpu.VMEM((te // V7X_SUBLANES,) + h2p.shape[1:2] + (V7X_SUBLANES, V7X_LANES), U32)
    return pl.pallas_call(
        body,
        grid_spec=pltpu.PrefetchScalarGridSpec(
            num_scalar_prefetch=2,
            grid=(n_tiles // EXPERT_RING,),
            in_specs=[hbm, hbm, hbm],
            out_specs=hbm,
            scratch_shapes=[tile_buf] * EXPERT_RING + [row_buf] * EXPERT_RING + [
                pltpu.VMEM((d, n_gu), F32), pltpu.VMEM((de, d), F32),
                pltpu.VMEM((d, n_gu), BF16), pltpu.VMEM((de, d), BF16),
                pltpu.SemaphoreType.DMA((EXPERT_RING,)), pltpu.SemaphoreType.DMA((EXPERT_RING,)),
                pltpu.SemaphoreType.DMA((2,)), pltpu.SemaphoreType.DMA,
            ],
        ),
        out_shape=jax.ShapeDtypeStruct((TOP_K * t + DUMP_TILES * te, dp), U32),
        compiler_params=_params(vmem, ("arbitrary",)),
        name="experts",
    )(tab, texp, h2p, wgu, wd)


def _combine_body(x1_ref, y0_ref, y1_ref, gcol_ref, mod_ref, modf_ref, gf_ref, o_ref):
    d = x1_ref.shape[1]
    gate2 = mod_ref[0][:, 5 * d:6 * d]
    modf = modf_ref[0]
    shift_f, scale_f = modf[:, 0:d], modf[:, d:2 * d]
    y0 = _unpack_bf16_pairs(y0_ref[...])
    y1 = _unpack_bf16_pairs(y1_ref[...])
    ffn = y0 * gcol_ref[:, 0:1] + y1 * gcol_ref[:, 1:2]
    x2 = x1_ref[...] + gate2 * ffn
    o_ref[...] = _rms(x2, gf_ref[...]) * (1.0 + scale_f) + shift_f


def _combine(x1, y2, gcol, mod3, modf3, gf, seq):
    t, d = x1.shape
    tm = COMBINE_TILE
    per_seq = seq // tm
    return pl.pallas_call(
        _combine_body,
        grid=(t // tm,),
        in_specs=[
            pl.BlockSpec((tm, d), lambda i: (i, 0)),
            pl.BlockSpec((tm, d // 2), lambda i: (i, 0)),
            pl.BlockSpec((tm, d // 2), lambda i: (t // tm + i, 0)),
            pl.BlockSpec((tm, V7X_LANES), lambda i: (i, 0)),
            pl.BlockSpec((1, 1, mod3.shape[2]), lambda i: (i // per_seq, 0, 0)),
            pl.BlockSpec((1, 1, modf3.shape[2]), lambda i: (i // per_seq, 0, 0)),
            pl.BlockSpec((1, d), lambda i: (0, 0)),
        ],
        out_specs=pl.BlockSpec((tm, d), lambda i: (i, 0)),
        out_shape=jax.ShapeDtypeStruct((t, d), F32),
        compiler_params=_params(2 * tm * (3 * d + V7X_LANES) * 4, ("arbitrary",)),
        name="combine",
    )(x1, y2, y2, gcol, mod3, modf3, gf)


def kernel(x, c, w_ada, b_ada, norm1_g, w_in, w_out, gmlp_w_s, gmlp_b_s, gmlp_v_gain, conv_w, conv_b,
           norm2_g, w_router_group, b_router_group, w_router_expert, b_router_expert, w_gate_up, w_down,
           w_ada_final, b_ada_final, norm_f_g):
    bsz, seq, d = x.shape
    depth = w_ada.shape[0]
    n_e = w_router_expert.shape[2]
    t = bsz * seq
    assert seq % TOKEN_TILE == 0 and TOKEN_TILE % CHUNK == 0 and seq % COMBINE_TILE == 0
    assert w_router_group.shape[2] == N_GROUPS <= V7X_SUBLANES and n_e == N_GROUPS * V7X_SUBLANES
    assert ROUTER_ROWS == n_e + V7X_SUBLANES
    n_tiles = (t * TOP_K + n_e * (EXPERT_TILE - 1)) // EXPERT_TILE
    n_tiles = -(-n_tiles // EXPERT_RING) * EXPERT_RING

    modf3 = _ada(c, w_ada_final, b_ada_final).reshape(bsz, 1, 2 * d)
    xt = x.reshape(t, d)
    for l in range(depth):
        mod3 = _ada(c, w_ada[l], b_ada[l]).reshape(bsz, 1, -1)
        wr = jnp.concatenate([w_router_expert[l], w_router_group[l]], axis=1)
        wrt = jnp.pad(wr, ((0, 0), (0, ROUTER_ROWS - wr.shape[1]))).T.astype(BF16)
        br = jnp.concatenate([b_router_expert[l], b_router_group[l]])
        br = jnp.pad(br, (0, ROUTER_ROWS - br.shape[0])).reshape(ROUTER_ROWS, 1)
        x1, eid, gcol, h2 = _mix(
            xt, mod3, norm1_g[l].reshape(1, d), norm2_g[l].reshape(1, d),
            w_in[l], w_out[l], gmlp_w_s[l], gmlp_b_s[l].T,
            gmlp_v_gain[l].reshape(1, -1), conv_w[l], conv_b[l].reshape(1, -1), wrt, br, seq)
        texp, tab = _dispatch(eid, n_e, n_tiles, t)
        h2 = h2.reshape((t,) + h2.shape[2:])
        y2 = _experts(tab, texp.reshape(-1), h2, w_gate_up[l], w_down[l], n_tiles)
        assert depth == 1
        xt = _combine(x1, y2, gcol, mod3, modf3, norm_f_g.reshape(1, d), seq)
    return xt.reshape(bsz, seq, d)
```

```python
import functools

import jax
import jax.numpy as jnp
from jax import lax
from jax.experimental import pallas as pl
from jax.experimental.pallas import tpu as pltpu

F32 = jnp.float32
BF16 = jnp.bfloat16
I32 = jnp.int32
U32 = jnp.uint32

A_HEADS = 8
CHUNK = 128
N_GROUPS = 4
TOP_K = 2
EPS = 1e-6

V7X_LANES = 128
V7X_SUBLANES = 8
V7X_VMEM_BYTES = 64 * 1024 * 1024
V7X_VMEM_RESERVE = 6 * 1024 * 1024
VMEM_TEMP_ALLOWANCE = 8 * 1024 * 1024

TOKEN_TILE = 256
EXPERT_TILE = 256
COMBINE_TILE = 512
ADA_TILE = 1024
MIX_STAGE_ROWS = 64
MIX_STAGE_SLOTS = 4
ROUTER_ROWS = 40
COUNT_UNROLL = 16
EXPERT_RING = 3
DUMP_TILES = 2


def _rms(x, g):
    y = x * lax.rsqrt(jnp.mean(x * x, axis=-1, keepdims=True) + EPS)
    return y * g


def _pack_bf16_pairs(x):
    n = x.shape[1] // 2
    lo = lax.bitcast_convert_type(x[:, :n].astype(BF16).astype(F32), U32)
    hi = lax.bitcast_convert_type(x[:, n:].astype(BF16).astype(F32), U32)
    return hi | (lo >> 16)


def _unpack_bf16_pairs(w):
    lo = lax.bitcast_convert_type(w << 16, F32)
    hi = lax.bitcast_convert_type(w & jnp.uint32(0xFFFF0000), F32)
    return jnp.concatenate([lo, hi], axis=1)


def _zero_after(anchor, n):
    bits = lax.bitcast_convert_type(anchor, U32)
    zero = lax.bitcast_convert_type((bits >> 16) >> 16, F32)
    return jnp.concatenate([zero] * (n // anchor.shape[1]), axis=1)


def _params(buffer_bytes, semantics=None):
    vmem_bytes = min(buffer_bytes + VMEM_TEMP_ALLOWANCE, V7X_VMEM_BYTES - V7X_VMEM_RESERVE)
    kw = dict(vmem_limit_bytes=int(vmem_bytes))
    if semantics is not None:
        kw["dimension_semantics"] = semantics
    return pltpu.CompilerParams(**kw)


def _ada_body(c_ref, w_ref, b_ref, o_ref):
    ca = jax.nn.silu(c_ref[...]).astype(BF16)
    o_ref[...] = jnp.dot(ca, w_ref[...].astype(BF16), preferred_element_type=F32) + b_ref[...]


def _ada(c, w, b):
    bsz, d = c.shape
    n = w.shape[1]
    tn = ADA_TILE
    return pl.pallas_call(
        _ada_body,
        grid=(n // tn,),
        in_specs=[
            pl.BlockSpec((bsz, d), lambda j: (0, 0)),
            pl.BlockSpec((d, tn), lambda j: (0, j)),
            pl.BlockSpec((1, tn), lambda j: (0, j)),
        ],
        out_specs=pl.BlockSpec((bsz, tn), lambda j: (0, j)),
        out_shape=jax.ShapeDtypeStruct((bsz, n), F32),
        compiler_params=_params(2 * d * tn * 4 + d * tn * 2, ("arbitrary",)),
        name="ada",
    )(c, w, b.reshape(1, n))


def _route(lt, tm):
    n_e = lt.shape[0] - V7X_SUBLANES
    epg = n_e // N_GROUPS
    row = lax.broadcasted_iota(I32, (V7X_SUBLANES, tm), 0).astype(F32)
    gl = lt[n_e:n_e + V7X_SUBLANES]
    gvalid = row < float(N_GROUPS)
    glm = jnp.where(gvalid, gl, -jnp.inf)
    gmax = jnp.max(glm, axis=0, keepdims=True)
    garg = jnp.min(jnp.where(glm == gmax, row, float(V7X_SUBLANES)), axis=0, keepdims=True)
    gsum = jnp.sum(jnp.where(gvalid, jnp.exp(gl - gmax), 0.0), axis=0, keepdims=True)
    p_grp = 1.0 / gsum
    es = lt[0:epg]
    for g in range(1, N_GROUPS):
        es = jnp.where(garg == float(g), lt[g * epg:(g + 1) * epg], es)
    m1 = jnp.max(es, axis=0, keepdims=True)
    i1 = jnp.min(jnp.where(es == m1, row, float(epg)), axis=0, keepdims=True)
    es2 = jnp.where(row == i1, -jnp.inf, es)
    m2 = jnp.max(es2, axis=0, keepdims=True)
    i2 = jnp.min(jnp.where(es2 == m2, row, float(epg)), axis=0, keepdims=True)
    z = jnp.exp(m2 - m1)
    den = 1.0 + z
    g0 = p_grp / den
    g1 = p_grp * z / den
    e0 = garg * float(epg) + i1
    e1 = garg * float(epg) + i2
    zero = jnp.zeros_like(g0)
    return jnp.concatenate([g0, g1, e0, e1, zero, zero, zero, zero], axis=0)


def _load_cast(src_hbm, dst, stage, sem):
    slots, rows, width = stage.shape
    cols = src_hbm.shape[1]
    parts = width // cols
    n = src_hbm.shape[0] // (rows * parts)
    ahead = slots - 1

    def copies(c, slot):
        return [pltpu.make_async_copy(src_hbm.at[pl.ds((c * parts + q) * rows, rows), :],
                                      stage.at[slot, :, pl.ds(q * cols, cols)], sem.at[slot])
                for q in range(parts)]

    for c in range(ahead):
        for cp in copies(c, c):
            cp.start()

    def body(c, carry):
        slot = c % slots

        @pl.when(c + ahead < n)
        def _():
            for cp in copies(c + ahead, (c + ahead) % slots):
                cp.start()

        for cp in copies(c, slot):
            cp.wait()
        for q in range(parts):
            r0 = pl.multiple_of((c * parts + q) * rows, rows)
            dst[pl.ds(r0, rows), :] = stage[slot, :, q * cols:(q + 1) * cols].astype(BF16)
        return carry

    lax.fori_loop(0, n, body, 0)


def _mix_body(x_ref, xp_ref, mod_ref, modp_ref, g1_ref, g2_ref, win_hbm, wout_hbm, ws_ref, bst_ref, vg_ref,
              cw_ref, cb_ref, wrt_ref, br_ref,
              x1_ref, eid_ref, gcol_ref, h2_hbm,
              pre_scr, mix_scr, h2_scr, win_s, wout_s, stage, h2_sem, w_sem, *, tiles_per_seq):
    tm, d = x_ref.shape
    aw = vg_ref.shape[1]
    bw = cb_ref.shape[1]
    hd = aw // A_HEADS
    nch = tm // CHUNK
    i = pl.program_id(0)
    last = pl.num_programs(0) - 1
    groups = tm // V7X_SUBLANES
    slot = i % 2

    def h2_copies(tile, sl):
        return [pltpu.make_async_copy(h2_scr.at[sl, :, s], h2_hbm.at[pl.ds(tile * groups, groups), :, s, :],
                                      h2_sem.at[sl])
                for s in range(V7X_SUBLANES)]

    @pl.when(i == 0)
    def _():
        mix_scr[...] = jnp.zeros(mix_scr.shape, F32)
        _load_cast(win_hbm, win_s, stage, w_sem)
        _load_cast(wout_hbm, wout_s, stage, w_sem)

    @pl.when(i > 2)
    def _():
        for cp in h2_copies(0, slot):
            cp.wait()

    def last_stage(anchor):
        modp = modp_ref[0]
        gate1p, shift2p, scale2p = modp[:, 2 * d:3 * d], modp[:, 3 * d:4 * d], modp[:, 4 * d:5 * d]
        if anchor is not None:
            gate1p = gate1p + _zero_after(anchor, d)
        x1 = xp_ref[...] + gate1p * mix_scr[...]
        x1_ref[...] = x1
        h2 = _rms(x1, g2_ref[...]) * (1.0 + scale2p) + shift2p
        words = _pack_bf16_pairs(h2)
        for s in range(V7X_SUBLANES):
            h2_scr[slot, :, s] = words[:, s * V7X_LANES:(s + 1) * V7X_LANES].reshape(
                groups, V7X_SUBLANES, V7X_LANES)
        lt = lax.dot_general(wrt_ref[...], h2.astype(BF16), (((1,), (1,)), ((), ())),
                             preferred_element_type=F32) + br_ref[...]
        slab = _route(lt, tm)
        eid_ref[0] = slab[2:4].astype(I32)
        wide = jnp.concatenate([slab, jnp.zeros((V7X_LANES - V7X_SUBLANES, tm), F32)], axis=0)
        gcol_ref[...] = wide.T

    @pl.when(i < last)
    def _():
        x = x_ref[...]
        mod = mod_ref[0]
        shift1, scale1 = mod[:, 0:d], mod[:, d:2 * d]
        h = _rms(x, g1_ref[...]) * (1.0 + scale1) + shift1
        hb = h.astype(BF16)

        uv = jnp.dot(hb, win_s[:, 0:2 * aw], preferred_element_type=F32)
        last_stage(uv[0:1, 0:V7X_LANES])
        uv = jax.nn.gelu(uv)
        u, v = uv[:, :aw], uv[:, aw:]
        tq_r = lax.broadcasted_iota(I32, (CHUNK, CHUNK), 0)
        tq_c = lax.broadcasted_iota(I32, (CHUNK, CHUNK), 1)
        causal = tq_c <= tq_r
        ya = []
        for hh in range(A_HEADS):
            sl = slice(hh * hd, (hh + 1) * hd)
            vh = _rms(v[:, sl], vg_ref[:, sl]).astype(BF16)
            rhs = jnp.concatenate([vh[c * CHUNK:(c + 1) * CHUNK] for c in range(nch)], axis=1)
            w = jnp.where(causal, ws_ref[hh], 0.0).astype(BF16)
            zs = jnp.dot(w, rhs, preferred_element_type=F32) + bst_ref[:, hh:hh + 1]
            zs = jnp.concatenate([zs[:, c * hd:(c + 1) * hd] for c in range(nch)], axis=0)
            ya.append(u[:, sl] * zs)

        bcx = jnp.dot(hb, win_s[:, 2 * aw:], preferred_element_type=F32)
        bg, cg, xin = bcx[:, :bw], bcx[:, bw:2 * bw], bcx[:, 2 * bw:]
        pre = cg * xin

        @pl.when(i % tiles_per_seq == 0)
        def _():
            pre_scr[0:V7X_SUBLANES, :] = jnp.zeros((V7X_SUBLANES, bw), F32)

        pre_scr[V7X_SUBLANES:V7X_SUBLANES + tm, :] = pre
        p1 = pre_scr[V7X_SUBLANES - 1:V7X_SUBLANES - 1 + tm, :]
        p2 = pre_scr[V7X_SUBLANES - 2:V7X_SUBLANES - 2 + tm, :]
        conv = cw_ref[0:1, :] * p2 + cw_ref[1:2, :] * p1 + cw_ref[2:3, :] * pre + cb_ref[...]
        yb = bg * conv
        pre_scr[0:V7X_SUBLANES, :] = pre_scr[tm:tm + V7X_SUBLANES, :]

        y = jnp.concatenate(ya + [yb], axis=1).astype(BF16)
        mix_scr[...] = jnp.dot(y, wout_s[...], preferred_element_type=F32)

    @pl.when(i == last)
    def _():
        last_stage(None)

    @pl.when(i > 0)
    def _():
        for cp in h2_copies(i - 1, slot):
            cp.start()

    @pl.when(i == last)
    def _():
        for sl in range(2):
            for cp in h2_copies(0, sl):
                cp.wait()


def _mix(xt, mod3, g1, g2, win, wout, ws, bst, vg, cw, cb, wrt, br, seq):
    t, d = xt.shape
    tm = TOKEN_TILE
    nt = t // tm
    per_seq = seq // tm
    in_cols = win.shape[1]
    bw = cb.shape[1]
    words_per_row = d // 2
    assert words_per_row == V7X_SUBLANES * V7X_LANES
    assert d % (MIX_STAGE_ROWS * (in_cols // wout.shape[1])) == 0 and d // MIX_STAGE_ROWS >= 2 * MIX_STAGE_SLOTS
    const2 = lambda i: (0, 0)
    body = functools.partial(_mix_body, tiles_per_seq=per_seq)
    cur = lambda i: jnp.minimum(i, nt - 1)
    prev = lambda i: jnp.maximum(i - 1, 0)
    hbm = pl.BlockSpec(memory_space=pl.ANY)
    vmem = (2 * d * (in_cols + d)
            + MIX_STAGE_SLOTS * MIX_STAGE_ROWS * in_cols * 4
            + 3 * 2 * tm * d * 4
            + 2 * tm * d * 2
            + (tm + V7X_SUBLANES) * bw * 4 + tm * d * 4
            + tm * (in_cols + 4 * d) * 4)
    return pl.pallas_call(
        body,
        grid=(nt + 1,),
        in_specs=[
            pl.BlockSpec((tm, d), lambda i: (cur(i), 0)),
            pl.BlockSpec((tm, d), lambda i: (prev(i), 0)),
            pl.BlockSpec((1, 1, mod3.shape[2]), lambda i: (cur(i) // per_seq, 0, 0)),
            pl.BlockSpec((1, 1, mod3.shape[2]), lambda i: (prev(i) // per_seq, 0, 0)),
            pl.BlockSpec((1, d), const2),
            pl.BlockSpec((1, d), const2),
            hbm,
            hbm,
            pl.BlockSpec(ws.shape, lambda i: (0, 0, 0)),
            pl.BlockSpec(bst.shape, const2),
            pl.BlockSpec(vg.shape, const2),
            pl.BlockSpec(cw.shape, const2),
            pl.BlockSpec(cb.shape, const2),
            pl.BlockSpec(wrt.shape, const2),
            pl.BlockSpec(br.shape, const2),
        ],
        out_specs=[
            pl.BlockSpec((tm, d), lambda i: (prev(i), 0)),
            pl.BlockSpec((1, TOP_K, tm), lambda i: (prev(i), 0, 0)),
            pl.BlockSpec((tm, V7X_LANES), lambda i: (prev(i), 0)),
            hbm,
        ],
        out_shape=[
            jax.ShapeDtypeStruct((t, d), F32),
            jax.ShapeDtypeStruct((nt, TOP_K, tm), I32),
            jax.ShapeDtypeStruct((t, V7X_LANES), F32),
            jax.ShapeDtypeStruct((t // V7X_SUBLANES, V7X_SUBLANES, words_per_row // V7X_LANES, V7X_LANES), U32),
        ],
        scratch_shapes=[
            pltpu.VMEM((tm + V7X_SUBLANES, bw), F32),
            pltpu.VMEM((tm, d), F32),
            pltpu.VMEM((2, tm // V7X_SUBLANES, words_per_row // V7X_LANES, V7X_SUBLANES, V7X_LANES), U32),
            pltpu.VMEM(win.shape, BF16),
            pltpu.VMEM(wout.shape, BF16),
            pltpu.VMEM((MIX_STAGE_SLOTS, MIX_STAGE_ROWS, in_cols), F32),
            pltpu.SemaphoreType.DMA((2,)),
            pltpu.SemaphoreType.DMA((MIX_STAGE_SLOTS,)),
        ],
        compiler_params=_params(vmem, ("arbitrary",)),
        name="mix",
    )(xt, xt, mod3, mod3, g1, g2, win, wout, ws, bst, vg, cw, cb, wrt, br)


def _dispatch_body(eid_ref, texp_ref, tab_ref, rank_scr, dest_v, dest_s, init_v, sem,
                   *, n_e, n_tiles, n_tok):
    nch, _, c = eid_ref.shape
    te = EXPERT_TILE
    r = lax.broadcasted_iota(I32, (c, c), 0)
    q = lax.broadcasted_iota(I32, (c, c), 1)
    before = (r < q).astype(BF16)
    e_iota = lax.broadcasted_iota(I32, (n_e, c), 0)

    init_v[...] = (lax.broadcasted_iota(I32, init_v.shape, 0) & (DUMP_TILES * te - 1)) + TOP_K * n_tok
    cp_i = pltpu.make_async_copy(init_v, tab_ref, sem.at[0])
    cp_i.start()

    def count_body(ch, carry):
        e2 = eid_ref[ch]
        ranks = []
        for k in range(TOP_K):
            oh = e_iota == e2[k:k + 1]
            ohf = oh.astype(F32)
            pref = jnp.dot(ohf.astype(BF16), before, preferred_element_type=F32)
            ranks.append(jnp.sum(jnp.where(oh, pref + carry, 0.0), axis=0, keepdims=True))
            carry = carry + jnp.sum(ohf, axis=1, keepdims=True)
        rank_scr[ch] = jnp.concatenate(ranks, axis=0)
        return carry

    counts = lax.fori_loop(0, nch, count_body, jnp.zeros((n_e, 1), F32), unroll=COUNT_UNROLL)

    padded = jnp.floor((counts + float(te - 1)) / float(te)) * float(te)
    sub = lax.broadcasted_iota(I32, (n_e, V7X_LANES), 0)
    lane = lax.broadcasted_iota(I32, (n_e, V7X_LANES), 1)
    pstart_row = jnp.sum(jnp.where(sub < lane, padded, 0.0), axis=0, keepdims=True)
    pstart = jnp.sum(jnp.where(sub == lane, pstart_row, 0.0), axis=1, keepdims=True)
    pend = pstart + padded
    total = jnp.max(pend, axis=0, keepdims=True)
    last_e = jnp.max(jnp.where(counts > 0.0, sub[:, 0:1].astype(F32), -1.0), axis=0, keepdims=True)

    rows_k = n_tok // V7X_LANES
    group = V7X_SUBLANES * V7X_LANES // c

    def dest_body(g, carry):
        rows = [[] for _ in range(TOP_K)]
        for j in range(group):
            e2 = eid_ref[g * group + j]
            rk = rank_scr[g * group + j]
            for k in range(TOP_K):
                oh = e_iota == e2[k:k + 1]
                off = jnp.sum(jnp.where(oh, pstart, 0.0), axis=0, keepdims=True)
                dest = (off + rk[k:k + 1]).astype(I32)
                rows[k] += [dest[:, h * V7X_LANES:(h + 1) * V7X_LANES] for h in range(c // V7X_LANES)]
        for k in range(TOP_K):
            word0 = pl.multiple_of((k * rows_k + g * V7X_SUBLANES) * V7X_LANES, V7X_SUBLANES * V7X_LANES)
            dest_v[pl.ds(word0, V7X_SUBLANES * V7X_LANES)] = jnp.concatenate(rows[k], axis=0).reshape(-1)
        return carry

    lax.fori_loop(0, nch // group, dest_body, 0)
    cp_d = pltpu.make_async_copy(dest_v, dest_s, sem.at[1])
    cp_d.start()

    w = texp_ref.shape[1]
    tj = lax.broadcasted_iota(I32, (n_e, w), 1).astype(F32) * float(te)
    texp = jnp.sum((pend <= tj).astype(F32), axis=0, keepdims=True)
    tj1 = tj[0:1]
    texp = jnp.where(tj1 < total, jnp.minimum(texp, float(n_e - 1)), last_e)
    lane_w = lax.broadcasted_iota(I32, (1, w), 1)
    texp = jnp.where(lane_w == n_tiles, total / float(te), texp)
    texp_ref[...] = texp.astype(I32)

    cp_i.wait()
    cp_d.wait()

    def lane_body(lane, carry):
        for row in range(TOP_K * rows_k):
            a = row * V7X_LANES + lane
            tab_ref[dest_s[a]] = a
        return carry

    lax.fori_loop(0, V7X_LANES, lane_body, 0)


def _dispatch(eid, n_e, n_tiles, n_tok):
    nch, k, c = eid.shape
    assert c == EXPERT_TILE and c % V7X_LANES == 0 and nch * c == n_tok and n_tok % (V7X_SUBLANES * V7X_LANES) == 0
    body = functools.partial(_dispatch_body, n_e=n_e, n_tiles=n_tiles, n_tok=n_tok)
    w = 2 * V7X_LANES
    assert n_tiles < w
    rows = n_tiles + V7X_SUBLANES
    rows += (-rows) % 4
    smem = pl.BlockSpec(memory_space=pltpu.SMEM)
    return pl.pallas_call(
        body,
        out_specs=[pl.BlockSpec(memory_space=pltpu.VMEM), smem],
        out_shape=[
            jax.ShapeDtypeStruct((1, w), I32),
            jax.ShapeDtypeStruct((rows * EXPERT_TILE,), I32),
        ],
        scratch_shapes=[
            pltpu.VMEM((nch, k, c), F32),
            pltpu.VMEM((k * n_tok,), I32),
            pltpu.SMEM((k * n_tok,), I32),
            pltpu.VMEM((rows * EXPERT_TILE,), I32),
            pltpu.SemaphoreType.DMA((2,)),
        ],
        compiler_params=_params(nch * V7X_SUBLANES * c * 4 + (k * n_tok + rows * EXPERT_TILE) * 4 + 4 * c * c * 4),
        name="dispatch",
    )(eid)


def _expert_body(tab_ref, texp_ref, h2_hbm, wgu_hbm, wd_hbm, y2_hbm,
                 x0, x1, x2, y0, y1, y2, wgu_f, wd_f, wgu_s, wd_s, gsem, ssem, wsem, zsem, *, n_tiles):
    te = EXPERT_TILE
    ring = EXPERT_RING
    de = wd_s.shape[0]
    n_tok = h2_hbm.shape[0]
    xs, ys = (x0, x1, x2), (y0, y1, y2)
    p = pl.program_id(0)
    n_used = texp_ref[n_tiles]
    t0 = ring * p

    def gather(tile, buf, sem):
        for r in range(te):
            tok = tab_ref[tile * te + r] & (n_tok - 1)
            pltpu.make_async_copy(h2_hbm.at[tok], buf.at[r // V7X_SUBLANES, :, r % V7X_SUBLANES, :], sem).start()

    def scatter(tile, buf, sem):
        for r in range(te):
            row = tab_ref[tile * te + r]
            pltpu.make_async_copy(buf.at[pl.ds(r, 1), :], y2_hbm.at[pl.ds(row, 1), :], sem).start()

    def wait_rows(buf, sem):
        pltpu.make_async_copy(buf, buf, sem).wait()

    def weight_copies(e):
        return (pltpu.make_async_copy(wgu_hbm.at[e], wgu_f, wsem.at[0]),
                pltpu.make_async_copy(wd_hbm.at[e], wd_f, wsem.at[1]))

    def dump_fill(h):
        rows = pl.ds(TOP_K * n_tok + h * te, te)
        return pltpu.make_async_copy(ys[ring - 1], y2_hbm.at[rows, :], zsem)

    def switch_weights(tile):
        e = texp_ref[tile]
        first = jnp.logical_or(tile == 0, texp_ref[jnp.maximum(tile - 1, 0)] != e)

        @pl.when(jnp.logical_and(first, tile < n_used))
        def _():
            for cp in weight_copies(e):
                cp.wait()
            wgu_s[...] = wgu_f[...].astype(BF16)
            wd_s[...] = wd_f[...].astype(BF16)
            nxt = lax.while_loop(
                lambda k: jnp.logical_and(k < n_used, texp_ref[jnp.minimum(k, n_tiles - 1)] == e),
                lambda k: k + 1, tile + 1)

            @pl.when(nxt < n_used)
            def _():
                for cp in weight_copies(texp_ref[jnp.minimum(nxt, n_tiles - 1)]):
                    cp.start(priority=1)

    def compute(xbuf, ybuf):
        words = jnp.concatenate([xbuf[:, s].reshape(te, V7X_LANES) for s in range(V7X_SUBLANES)], axis=1)
        xb16 = _unpack_bf16_pairs(words).astype(BF16)
        gu = jnp.dot(xb16, wgu_s[...], preferred_element_type=F32)
        a = (jax.nn.silu(gu[:, :de]) * gu[:, de:]).astype(BF16)
        ybuf[...] = _pack_bf16_pairs(jnp.dot(a, wd_s[...], preferred_element_type=F32))

    @pl.when(t0 < n_used)
    def _():
        @pl.when(p == 0)
        def _():
            for cp in weight_copies(texp_ref[0]):
                cp.start(priority=1)
            ys[ring - 1][...] = jnp.zeros(ys[ring - 1].shape, U32)
            for h in range(DUMP_TILES):
                dump_fill(h).start()
            for h in range(DUMP_TILES):
                dump_fill(h).wait()
            for m in range(ring - 1):
                gather(m, xs[m], gsem.at[m])

        for m in range(ring):
            tile = t0 + m
            nm = (m + ring - 1) % ring
            switch_weights(tile)
            wait_rows(xs[m], gsem.at[m])
            if m == ring - 1:
                wait_rows(ys[m], ssem.at[m])
            else:
                @pl.when(p > 0)
                def _(m=m):
                    wait_rows(ys[m], ssem.at[m])
            gather(tile + ring - 1, xs[nm], gsem.at[nm])
            scatter(jnp.where(tile == 0, n_tiles, tile - 1), ys[nm], ssem.at[nm])
            compute(xs[m], ys[m])

        @pl.when(t0 + ring >= n_used)
        def _():
            for m in range(ring - 1):
                wait_rows(ys[m], ssem.at[m])
            scatter(t0 + ring - 1, ys[ring - 1], ssem.at[ring - 1])
            wait_rows(ys[ring - 1], ssem.at[ring - 1])
            for m in range(ring - 1):
                wait_rows(xs[m], gsem.at[m])


def _experts(tab, texp, h2p, wgu, wd, n_tiles):
    t = h2p.shape[0]
    dp = h2p.shape[1] * h2p.shape[2]
    te = EXPERT_TILE
    _, d, n_gu = wgu.shape
    de = wd.shape[1]
    assert n_tiles % EXPERT_RING == 0 and t & (t - 1) == 0
    body = functools.partial(_expert_body, n_tiles=n_tiles)
    vmem = (d * n_gu + de * d) * (4 + 2) + 2 * EXPERT_RING * te * dp * 4 + te * (n_gu + 3 * d) * 4
    hbm = pl.BlockSpec(memory_space=pl.ANY)
    row_buf = pltpu.VMEM((te, dp), U32)
    tile_buf = pltpu.VMEM((te // V7X_SUBLANES,) + h2p.shape[1:2] + (V7X_SUBLANES, V7X_LANES), U32)
    return pl.pallas_call(
        body,
        grid_spec=pltpu.PrefetchScalarGridSpec(
            num_scalar_prefetch=2,
            grid=(n_tiles // EXPERT_RING,),
            in_specs=[hbm, hbm, hbm],
            out_specs=hbm,
            scratch_shapes=[tile_buf] * EXPERT_RING + [row_buf] * EXPERT_RING + [
                pltpu.VMEM((d, n_gu), F32), pltpu.VMEM((de, d), F32),
                pltpu.VMEM((d, n_gu), BF16), pltpu.VMEM((de, d), BF16),
                pltpu.SemaphoreType.DMA((EXPERT_RING,)), pltpu.SemaphoreType.DMA((EXPERT_RING,)),
                pltpu.SemaphoreType.DMA((2,)), pltpu.SemaphoreType.DMA,
            ],
        ),
        out_shape=jax.ShapeDtypeStruct((TOP_K * t + DUMP_TILES * te, dp), U32),
        compiler_params=_params(vmem, ("arbitrary",)),
        name="experts",
    )(tab, texp, h2p, wgu, wd)


def _combine_body(x1_ref, y0_ref, y1_ref, gcol_ref, mod_ref, modf_ref, gf_ref, o_ref):
    d = x1_ref.shape[1]
    gate2 = mod_ref[0][:, 5 * d:6 * d]
    modf = modf_ref[0]
    shift_f, scale_f = modf[:, 0:d], modf[:, d:2 * d]
    y0 = _unpack_bf16_pairs(y0_ref[...])
    y1 = _unpack_bf16_pairs(y1_ref[...])
    ffn = y0 * gcol_ref[:, 0:1] + y1 * gcol_ref[:, 1:2]
    x2 = x1_ref[...] + gate2 * ffn
    o_ref[...] = _rms(x2, gf_ref[...]) * (1.0 + scale_f) + shift_f


def _combine(x1, y2, gcol, mod3, modf3, gf, seq):
    t, d = x1.shape
    tm = COMBINE_TILE
    per_seq = seq // tm
    return pl.pallas_call(
        _combine_body,
        grid=(t // tm,),
        in_specs=[
            pl.BlockSpec((tm, d), lambda i: (i, 0)),
            pl.BlockSpec((tm, d // 2), lambda i: (i, 0)),
            pl.BlockSpec((tm, d // 2), lambda i: (t // tm + i, 0)),
            pl.BlockSpec((tm, V7X_LANES), lambda i: (i, 0)),
            pl.BlockSpec((1, 1, mod3.shape[2]), lambda i: (i // per_seq, 0, 0)),
            pl.BlockSpec((1, 1, modf3.shape[2]), lambda i: (i // per_seq, 0, 0)),
            pl.BlockSpec((1, d), lambda i: (0, 0)),
        ],
        out_specs=pl.BlockSpec((tm, d), lambda i: (i, 0)),
        out_shape=jax.ShapeDtypeStruct((t, d), F32),
        compiler_params=_params(2 * tm * (3 * d + V7X_LANES) * 4, ("arbitrary",)),
        name="combine",
    )(x1, y2, y2, gcol, mod3, modf3, gf)


def kernel(x, c, w_ada, b_ada, norm1_g, w_in, w_out, gmlp_w_s, gmlp_b_s, gmlp_v_gain, conv_w, conv_b,
           norm2_g, w_router_group, b_router_group, w_router_expert, b_router_expert, w_gate_up, w_down,
           w_ada_final, b_ada_final, norm_f_g):
    bsz, seq, d = x.shape
    depth = w_ada.shape[0]
    n_e = w_router_expert.shape[2]
    t = bsz * seq
    assert seq % TOKEN_TILE == 0 and TOKEN_TILE % CHUNK == 0 and seq % COMBINE_TILE == 0
    assert w_router_group.shape[2] == N_GROUPS <= V7X_SUBLANES and n_e == N_GROUPS * V7X_SUBLANES
    assert ROUTER_ROWS == n_e + V7X_SUBLANES
    n_tiles = (t * TOP_K + n_e * (EXPERT_TILE - 1)) // EXPERT_TILE
    n_tiles = -(-n_tiles // EXPERT_RING) * EXPERT_RING

    modf3 = _ada(c, w_ada_final, b_ada_final).reshape(bsz, 1, 2 * d)
    xt = x.reshape(t, d)
    for l in range(depth):
        mod3 = _ada(c, w_ada[l], b_ada[l]).reshape(bsz, 1, -1)
        wr = jnp.concatenate([w_router_expert[l], w_router_group[l]], axis=1)
        wrt = jnp.pad(wr, ((0, 0), (0, ROUTER_ROWS - wr.shape[1]))).T.astype(BF16)
        br = jnp.concatenate([b_router_expert[l], b_router_group[l]])
        br = jnp.pad(br, (0, ROUTER_ROWS - br.shape[0])).reshape(ROUTER_ROWS, 1)
        x1, eid, gcol, h2 = _mix(
            xt, mod3, norm1_g[l].reshape(1, d), norm2_g[l].reshape(1, d),
            w_in[l], w_out[l], gmlp_w_s[l], gmlp_b_s[l].T,
            gmlp_v_gain[l].reshape(1, -1), conv_w[l], conv_b[l].reshape(1, -1), wrt, br, seq)
        texp, tab = _dispatch(eid, n_e, n_tiles, t)
        h2 = h2.reshape((t,) + h2.shape[2:])
        y2 = _experts(tab, texp.reshape(-1), h2, w_gate_up[l], w_down[l], n_tiles)
        assert depth == 1
        xt = _combine(x1, y2, gcol, mod3, modf3, norm_f_g.reshape(1, d), seq)
    return xt.reshape(bsz, seq, d)
```

```python
import functools

import jax
import jax.numpy as jnp
from jax import lax
from jax.experimental import pallas as pl
from jax.experimental.pallas import tpu as pltpu

F32 = jnp.float32
BF16 = jnp.bfloat16
I32 = jnp.int32
U32 = jnp.uint32

A_HEADS = 8
CHUNK = 128
N_GROUPS = 4
TOP_K = 2
EPS = 1e-6

V7X_LANES = 128
V7X_SUBLANES = 8
V7X_VMEM_BYTES = 64 * 1024 * 1024
V7X_VMEM_RESERVE = 6 * 1024 * 1024
VMEM_TEMP_ALLOWANCE = 8 * 1024 * 1024

TOKEN_TILE = 256
EXPERT_TILE = 256
COMBINE_TILE = 512
ADA_TILE = 1024
MIX_STAGE_ROWS = 64
MIX_STAGE_SLOTS = 5
ROUTER_ROWS = 40
COUNT_UNROLL = 16
EXPERT_RING = 3
DUMP_TILES = 2


def _rms(x, g):
    y = x * lax.rsqrt(jnp.mean(x * x, axis=-1, keepdims=True) + EPS)
    return y * g


def _pack_bf16_pairs(x):
    n = x.shape[1] // 2
    lo = lax.bitcast_convert_type(x[:, :n].astype(BF16).astype(F32), U32)
    hi = lax.bitcast_convert_type(x[:, n:].astype(BF16).astype(F32), U32)
    return hi | (lo >> 16)


def _unpack_bf16_pairs(w):
    lo = lax.bitcast_convert_type(w << 16, F32)
    hi = lax.bitcast_convert_type(w & jnp.uint32(0xFFFF0000), F32)
    return jnp.concatenate([lo, hi], axis=1)


def _zero_after(anchor, n):
    bits = lax.bitcast_convert_type(anchor, U32)
    zero = lax.bitcast_convert_type((bits >> 16) >> 16, F32)
    return jnp.concatenate([zero] * (n // anchor.shape[1]), axis=1)


def _params(buffer_bytes, semantics=None):
    vmem_bytes = min(buffer_bytes + VMEM_TEMP_ALLOWANCE, V7X_VMEM_BYTES - V7X_VMEM_RESERVE)
    kw = dict(vmem_limit_bytes=int(vmem_bytes))
    if semantics is not None:
        kw["dimension_semantics"] = semantics
    return pltpu.CompilerParams(**kw)


def _ada_body(c_ref, w_ref, b_ref, o_ref):
    ca = jax.nn.silu(c_ref[...]).astype(BF16)
    o_ref[...] = jnp.dot(ca, w_ref[...].astype(BF16), preferred_element_type=F32) + b_ref[...]


def _ada(c, w, b):
    bsz, d = c.shape
    n = w.shape[1]
    tn = ADA_TILE
    return pl.pallas_call(
        _ada_body,
        grid=(n // tn,),
        in_specs=[
            pl.BlockSpec((bsz, d), lambda j: (0, 0)),
            pl.BlockSpec((d, tn), lambda j: (0, j)),
            pl.BlockSpec((1, tn), lambda j: (0, j)),
        ],
        out_specs=pl.BlockSpec((bsz, tn), lambda j: (0, j)),
        out_shape=jax.ShapeDtypeStruct((bsz, n), F32),
        compiler_params=_params(2 * d * tn * 4 + d * tn * 2, ("arbitrary",)),
        name="ada",
    )(c, w, b.reshape(1, n))


def _route(lt, tm):
    n_e = lt.shape[0] - V7X_SUBLANES
    epg = n_e // N_GROUPS
    row = lax.broadcasted_iota(I32, (V7X_SUBLANES, tm), 0).astype(F32)
    gl = lt[n_e:n_e + V7X_SUBLANES]
    gvalid = row < float(N_GROUPS)
    glm = jnp.where(gvalid, gl, -jnp.inf)
    gmax = jnp.max(glm, axis=0, keepdims=True)
    garg = jnp.min(jnp.where(glm == gmax, row, float(V7X_SUBLANES)), axis=0, keepdims=True)
    gsum = jnp.sum(jnp.where(gvalid, jnp.exp(gl - gmax), 0.0), axis=0, keepdims=True)
    p_grp = 1.0 / gsum
    es = lt[0:epg]
    for g in range(1, N_GROUPS):
        es = jnp.where(garg == float(g), lt[g * epg:(g + 1) * epg], es)
    m1 = jnp.max(es, axis=0, keepdims=True)
    i1 = jnp.min(jnp.where(es == m1, row, float(epg)), axis=0, keepdims=True)
    es2 = jnp.where(row == i1, -jnp.inf, es)
    m2 = jnp.max(es2, axis=0, keepdims=True)
    i2 = jnp.min(jnp.where(es2 == m2, row, float(epg)), axis=0, keepdims=True)
    z = jnp.exp(m2 - m1)
    den = 1.0 + z
    g0 = p_grp / den
    g1 = p_grp * z / den
    e0 = garg * float(epg) + i1
    e1 = garg * float(epg) + i2
    zero = jnp.zeros_like(g0)
    return jnp.concatenate([g0, g1, e0, e1, zero, zero, zero, zero], axis=0)


def _load_cast(src_hbm, dst, stage, sem):
    slots, rows, width = stage.shape
    cols = src_hbm.shape[1]
    parts = width // cols
    n = src_hbm.shape[0] // (rows * parts)
    ahead = slots - 1

    def copies(c, slot):
        return [pltpu.make_async_copy(src_hbm.at[pl.ds((c * parts + q) * rows, rows), :],
                                      stage.at[slot, :, pl.ds(q * cols, cols)], sem.at[slot])
                for q in range(parts)]

    for c in range(ahead):
        for cp in copies(c, c):
            cp.start()

    def body(c, carry):
        slot = c % slots

        @pl.when(c + ahead < n)
        def _():
            for cp in copies(c + ahead, (c + ahead) % slots):
                cp.start()

        for cp in copies(c, slot):
            cp.wait()
        for q in range(parts):
            r0 = pl.multiple_of((c * parts + q) * rows, rows)
            dst[pl.ds(r0, rows), :] = stage[slot, :, q * cols:(q + 1) * cols].astype(BF16)
        return carry

    lax.fori_loop(0, n, body, 0)


def _mix_body(x_ref, xp_ref, mod_ref, modp_ref, g1_ref, g2_ref, win_hbm, wout_hbm, ws_ref, bst_ref, vg_ref,
              cw_ref, cb_ref, wrt_ref, br_ref,
              x1_ref, eid_ref, gcol_ref, h2_hbm,
              pre_scr, mix_scr, h2_scr, win_s, wout_s, stage, h2_sem, w_sem, *, tiles_per_seq):
    tm, d = x_ref.shape
    aw = vg_ref.shape[1]
    bw = cb_ref.shape[1]
    hd = aw // A_HEADS
    nch = tm // CHUNK
    i = pl.program_id(0)
    last = pl.num_programs(0) - 1
    groups = tm // V7X_SUBLANES
    slot = i % 2

    def h2_copies(tile, sl):
        return [pltpu.make_async_copy(h2_scr.at[sl, :, s], h2_hbm.at[pl.ds(tile * groups, groups), :, s, :],
                                      h2_sem.at[sl])
                for s in range(V7X_SUBLANES)]

    @pl.when(i == 0)
    def _():
        mix_scr[...] = jnp.zeros(mix_scr.shape, F32)
        _load_cast(win_hbm, win_s, stage, w_sem)
        _load_cast(wout_hbm, wout_s, stage, w_sem)

    @pl.when(i > 2)
    def _():
        for cp in h2_copies(0, slot):
            cp.wait()

    def last_stage(anchor):
        modp = modp_ref[0]
        gate1p, shift2p, scale2p = modp[:, 2 * d:3 * d], modp[:, 3 * d:4 * d], modp[:, 4 * d:5 * d]
        if anchor is not None:
            gate1p = gate1p + _zero_after(anchor, d)
        x1 = xp_ref[...] + gate1p * mix_scr[...]
        x1_ref[...] = x1
        h2 = _rms(x1, g2_ref[...]) * (1.0 + scale2p) + shift2p
        words = _pack_bf16_pairs(h2)
        for s in range(V7X_SUBLANES):
            h2_scr[slot, :, s] = words[:, s * V7X_LANES:(s + 1) * V7X_LANES].reshape(
                groups, V7X_SUBLANES, V7X_LANES)
        lt = lax.dot_general(wrt_ref[...], h2.astype(BF16), (((1,), (1,)), ((), ())),
                             preferred_element_type=F32) + br_ref[...]
        slab = _route(lt, tm)
        eid_ref[0] = slab[2:4].astype(I32)
        wide = jnp.concatenate([slab, jnp.zeros((V7X_LANES - V7X_SUBLANES, tm), F32)], axis=0)
        gcol_ref[...] = wide.T

    @pl.when(i < last)
    def _():
        x = x_ref[...]
        mod = mod_ref[0]
        shift1, scale1 = mod[:, 0:d], mod[:, d:2 * d]
        h = _rms(x, g1_ref[...]) * (1.0 + scale1) + shift1
        hb = h.astype(BF16)

        uv = jnp.dot(hb, win_s[:, 0:2 * aw], preferred_element_type=F32)
        last_stage(uv[0:1, 0:V7X_LANES])
        uv = jax.nn.gelu(uv)
        u, v = uv[:, :aw], uv[:, aw:]
        tq_r = lax.broadcasted_iota(I32, (CHUNK, CHUNK), 0)
        tq_c = lax.broadcasted_iota(I32, (CHUNK, CHUNK), 1)
        causal = tq_c <= tq_r
        ya = []
        for hh in range(A_HEADS):
            sl = slice(hh * hd, (hh + 1) * hd)
            vh = _rms(v[:, sl], vg_ref[:, sl]).astype(BF16)
            rhs = jnp.concatenate([vh[c * CHUNK:(c + 1) * CHUNK] for c in range(nch)], axis=1)
            w = jnp.where(causal, ws_ref[hh], 0.0).astype(BF16)
            zs = jnp.dot(w, rhs, preferred_element_type=F32) + bst_ref[:, hh:hh + 1]
            zs = jnp.concatenate([zs[:, c * hd:(c + 1) * hd] for c in range(nch)], axis=0)
            ya.append(u[:, sl] * zs)

        bcx = jnp.dot(hb, win_s[:, 2 * aw:], preferred_element_type=F32)
        bg, cg, xin = bcx[:, :bw], bcx[:, bw:2 * bw], bcx[:, 2 * bw:]
        pre = cg * xin

        @pl.when(i % tiles_per_seq == 0)
        def _():
            pre_scr[0:V7X_SUBLANES, :] = jnp.zeros((V7X_SUBLANES, bw), F32)

        pre_scr[V7X_SUBLANES:V7X_SUBLANES + tm, :] = pre
        p1 = pre_scr[V7X_SUBLANES - 1:V7X_SUBLANES - 1 + tm, :]
        p2 = pre_scr[V7X_SUBLANES - 2:V7X_SUBLANES - 2 + tm, :]
        conv = cw_ref[0:1, :] * p2 + cw_ref[1:2, :] * p1 + cw_ref[2:3, :] * pre + cb_ref[...]
        yb = bg * conv
        pre_scr[0:V7X_SUBLANES, :] = pre_scr[tm:tm + V7X_SUBLANES, :]

        y = jnp.concatenate(ya + [yb], axis=1).astype(BF16)
        mix_scr[...] = jnp.dot(y, wout_s[...], preferred_element_type=F32)

    @pl.when(i == last)
    def _():
        last_stage(None)

    @pl.when(i > 0)
    def _():
        for cp in h2_copies(i - 1, slot):
            cp.start()

    @pl.when(i == last)
    def _():
        for sl in range(2):
            for cp in h2_copies(0, sl):
                cp.wait()


def _mix(xt, mod3, g1, g2, win, wout, ws, bst, vg, cw, cb, wrt, br, seq):
    t, d = xt.shape
    tm = TOKEN_TILE
    nt = t // tm
    per_seq = seq // tm
    in_cols = win.shape[1]
    bw = cb.shape[1]
    words_per_row = d // 2
    assert words_per_row == V7X_SUBLANES * V7X_LANES
    assert d % (MIX_STAGE_ROWS * (in_cols // wout.shape[1])) == 0 and d // MIX_STAGE_ROWS >= 2 * MIX_STAGE_SLOTS
    const2 = lambda i: (0, 0)
    body = functools.partial(_mix_body, tiles_per_seq=per_seq)
    cur = lambda i: jnp.minimum(i, nt - 1)
    prev = lambda i: jnp.maximum(i - 1, 0)
    hbm = pl.BlockSpec(memory_space=pl.ANY)
    vmem = (2 * d * (in_cols + d)
            + MIX_STAGE_SLOTS * MIX_STAGE_ROWS * in_cols * 4
            + 3 * 2 * tm * d * 4
            + 2 * tm * d * 2
            + (tm + V7X_SUBLANES) * bw * 4 + tm * d * 4
            + tm * (in_cols + 4 * d) * 4)
    return pl.pallas_call(
        body,
        grid=(nt + 1,),
        in_specs=[
            pl.BlockSpec((tm, d), lambda i: (cur(i), 0)),
            pl.BlockSpec((tm, d), lambda i: (prev(i), 0)),
            pl.BlockSpec((1, 1, mod3.shape[2]), lambda i: (cur(i) // per_seq, 0, 0)),
            pl.BlockSpec((1, 1, mod3.shape[2]), lambda i: (prev(i) // per_seq, 0, 0)),
            pl.BlockSpec((1, d), const2),
            pl.BlockSpec((1, d), const2),
            hbm,
            hbm,
            pl.BlockSpec(ws.shape, lambda i: (0, 0, 0)),
            pl.BlockSpec(bst.shape, const2),
            pl.BlockSpec(vg.shape, const2),
            pl.BlockSpec(cw.shape, const2),
            pl.BlockSpec(cb.shape, const2),
            pl.BlockSpec(wrt.shape, const2),
            pl.BlockSpec(br.shape, const2),
        ],
        out_specs=[
            pl.BlockSpec((tm, d), lambda i: (prev(i), 0)),
            pl.BlockSpec((1, TOP_K, tm), lambda i: (prev(i), 0, 0)),
            pl.BlockSpec((tm, V7X_LANES), lambda i: (prev(i), 0)),
            hbm,
        ],
        out_shape=[
            jax.ShapeDtypeStruct((t, d), F32),
            jax.ShapeDtypeStruct((nt, TOP_K, tm), I32),
            jax.ShapeDtypeStruct((t, V7X_LANES), F32),
            jax.ShapeDtypeStruct((t // V7X_SUBLANES, V7X_SUBLANES, words_per_row // V7X_LANES, V7X_LANES), U32),
        ],
        scratch_shapes=[
            pltpu.VMEM((tm + V7X_SUBLANES, bw), F32),
            pltpu.VMEM((tm, d), F32),
            pltpu.VMEM((2, tm // V7X_SUBLANES, words_per_row // V7X_LANES, V7X_SUBLANES, V7X_LANES), U32),
            pltpu.VMEM(win.shape, BF16),
            pltpu.VMEM(wout.shape, BF16),
            pltpu.VMEM((MIX_STAGE_SLOTS, MIX_STAGE_ROWS, in_cols), F32),
            pltpu.SemaphoreType.DMA((2,)),
            pltpu.SemaphoreType.DMA((MIX_STAGE_SLOTS,)),
        ],
        compiler_params=_params(vmem, ("arbitrary",)),
        name="mix",
    )(xt, xt, mod3, mod3, g1, g2, win, wout, ws, bst, vg, cw, cb, wrt, br)


def _dispatch_body(eid_ref, texp_ref, tab_ref, rank_scr, dest_v, dest_s, init_v, sem,
                   *, n_e, n_tiles, n_tok):
    nch, _, c = eid_ref.shape
    te = EXPERT_TILE
    r = lax.broadcasted_iota(I32, (c, c), 0)
    q = lax.broadcasted_iota(I32, (c, c), 1)
    before = (r < q).astype(BF16)
    e_iota = lax.broadcasted_iota(I32, (n_e, c), 0)

    init_v[...] = (lax.broadcasted_iota(I32, init_v.shape, 0) & (DUMP_TILES * te - 1)) + TOP_K * n_tok
    cp_i = pltpu.make_async_copy(init_v, tab_ref, sem.at[0])
    cp_i.start()

    def count_body(ch, carry):
        e2 = eid_ref[ch]
        ranks = []
        for k in range(TOP_K):
            oh = e_iota == e2[k:k + 1]
            ohf = oh.astype(F32)
            pref = jnp.dot(ohf.astype(BF16), before, preferred_element_type=F32)
            ranks.append(jnp.sum(jnp.where(oh, pref + carry, 0.0), axis=0, keepdims=True))
            carry = carry + jnp.sum(ohf, axis=1, keepdims=True)
        rank_scr[ch] = jnp.concatenate(ranks, axis=0)
        return carry

    counts = lax.fori_loop(0, nch, count_body, jnp.zeros((n_e, 1), F32), unroll=COUNT_UNROLL)

    padded = jnp.floor((counts + float(te - 1)) / float(te)) * float(te)
    sub = lax.broadcasted_iota(I32, (n_e, V7X_LANES), 0)
    lane = lax.broadcasted_iota(I32, (n_e, V7X_LANES), 1)
    pstart_row = jnp.sum(jnp.where(sub < lane, padded, 0.0), axis=0, keepdims=True)
    pstart = jnp.sum(jnp.where(sub == lane, pstart_row, 0.0), axis=1, keepdims=True)
    pend = pstart + padded
    total = jnp.max(pend, axis=0, keepdims=True)
    last_e = jnp.max(jnp.where(counts > 0.0, sub[:, 0:1].astype(F32), -1.0), axis=0, keepdims=True)

    rows_k = n_tok // V7X_LANES
    group = V7X_SUBLANES * V7X_LANES // c

    def dest_body(g, carry):
        rows = [[] for _ in range(TOP_K)]
        for j in range(group):
            e2 = eid_ref[g * group + j]
            rk = rank_scr[g * group + j]
            for k in range(TOP_K):
                oh = e_iota == e2[k:k + 1]
                off = jnp.sum(jnp.where(oh, pstart, 0.0), axis=0, keepdims=True)
                dest = (off + rk[k:k + 1]).astype(I32)
                rows[k] += [dest[:, h * V7X_LANES:(h + 1) * V7X_LANES] for h in range(c // V7X_LANES)]
        for k in range(TOP_K):
            word0 = pl.multiple_of((k * rows_k + g * V7X_SUBLANES) * V7X_LANES, V7X_SUBLANES * V7X_LANES)
            dest_v[pl.ds(word0, V7X_SUBLANES * V7X_LANES)] = jnp.concatenate(rows[k], axis=0).reshape(-1)
        return carry

    lax.fori_loop(0, nch // group, dest_body, 0)
    cp_d = pltpu.make_async_copy(dest_v, dest_s, sem.at[1])
    cp_d.start()

    w = texp_ref.shape[1]
    tj = lax.broadcasted_iota(I32, (n_e, w), 1).astype(F32) * float(te)
    texp = jnp.sum((pend <= tj).astype(F32), axis=0, keepdims=True)
    tj1 = tj[0:1]
    texp = jnp.where(tj1 < total, jnp.minimum(texp, float(n_e - 1)), last_e)
    lane_w = lax.broadcasted_iota(I32, (1, w), 1)
    texp = jnp.where(lane_w == n_tiles, total / float(te), texp)
    texp_ref[...] = texp.astype(I32)

    cp_i.wait()
    cp_d.wait()

    def lane_body(lane, carry):
        for row in range(TOP_K * rows_k):
            a = row * V7X_LANES + lane
            tab_ref[dest_s[a]] = a
        return carry

    lax.fori_loop(0, V7X_LANES, lane_body, 0)


def _dispatch(eid, n_e, n_tiles, n_tok):
    nch, k, c = eid.shape
    assert c == EXPERT_TILE and c % V7X_LANES == 0 and nch * c == n_tok and n_tok % (V7X_SUBLANES * V7X_LANES) == 0
    body = functools.partial(_dispatch_body, n_e=n_e, n_tiles=n_tiles, n_tok=n_tok)
    w = 2 * V7X_LANES
    assert n_tiles < w
    rows = n_tiles + V7X_SUBLANES
    rows += (-rows) % 4
    smem = pl.BlockSpec(memory_space=pltpu.SMEM)
    return pl.pallas_call(
        body,
        out_specs=[pl.BlockSpec(memory_space=pltpu.VMEM), smem],
        out_shape=[
            jax.ShapeDtypeStruct((1, w), I32),
            jax.ShapeDtypeStruct((rows * EXPERT_TILE,), I32),
        ],
        scratch_shapes=[
            pltpu.VMEM((nch, k, c), F32),
            pltpu.VMEM((k * n_tok,), I32),
            pltpu.SMEM((k * n_tok,), I32),
            pltpu.VMEM((rows * EXPERT_TILE,), I32),
            pltpu.SemaphoreType.DMA((2,)),
        ],
        compiler_params=_params(nch * V7X_SUBLANES * c * 4 + (k * n_tok + rows * EXPERT_TILE) * 4 + 4 * c * c * 4),
        name="dispatch",
    )(eid)


def _expert_body(tab_ref, texp_ref, h2_hbm, wgu_hbm, wd_hbm, y2_hbm,
                 x0, x1, x2, y0, y1, y2, wgu_f, wd_f, wgu_s, wd_s, gsem, ssem, wsem, zsem, *, n_tiles):
    te = EXPERT_TILE
    ring = EXPERT_RING
    de = wd_s.shape[0]
    n_tok = h2_hbm.shape[0]
    xs, ys = (x0, x1, x2), (y0, y1, y2)
    p = pl.program_id(0)
    n_used = texp_ref[n_tiles]
    t0 = ring * p

    def gather(tile, buf, sem):
        for r in range(te):
            tok = tab_ref[tile * te + r] & (n_tok - 1)
            pltpu.make_async_copy(h2_hbm.at[tok], buf.at[r // V7X_SUBLANES, :, r % V7X_SUBLANES, :], sem).start()

    def scatter(tile, buf, sem):
        for r in range(te):
            row = tab_ref[tile * te + r]
            pltpu.make_async_copy(buf.at[pl.ds(r, 1), :], y2_hbm.at[pl.ds(row, 1), :], sem).start()

    def wait_rows(buf, sem):
        pltpu.make_async_copy(buf, buf, sem).wait()

    def weight_copies(e):
        return (pltpu.make_async_copy(wgu_hbm.at[e], wgu_f, wsem.at[0]),
                pltpu.make_async_copy(wd_hbm.at[e], wd_f, wsem.at[1]))

    def dump_fill(h):
        rows = pl.ds(TOP_K * n_tok + h * te, te)
        return pltpu.make_async_copy(ys[ring - 1], y2_hbm.at[rows, :], zsem)

    def switch_weights(tile):
        e = texp_ref[tile]
        first = jnp.logical_or(tile == 0, texp_ref[jnp.maximum(tile - 1, 0)] != e)

        @pl.when(jnp.logical_and(first, tile < n_used))
        def _():
            for cp in weight_copies(e):
                cp.wait()
            wgu_s[...] = wgu_f[...].astype(BF16)
            wd_s[...] = wd_f[...].astype(BF16)
            nxt = lax.while_loop(
                lambda k: jnp.logical_and(k < n_used, texp_ref[jnp.minimum(k, n_tiles - 1)] == e),
                lambda k: k + 1, tile + 1)

            @pl.when(nxt < n_used)
            def _():
                for cp in weight_copies(texp_ref[jnp.minimum(nxt, n_tiles - 1)]):
                    cp.start(priority=1)

    def compute(xbuf, ybuf):
        words = jnp.concatenate([xbuf[:, s].reshape(te, V7X_LANES) for s in range(V7X_SUBLANES)], axis=1)
        xb16 = _unpack_bf16_pairs(words).astype(BF16)
        gu = jnp.dot(xb16, wgu_s[...], preferred_element_type=F32)
        a = (jax.nn.silu(gu[:, :de]) * gu[:, de:]).astype(BF16)
        ybuf[...] = _pack_bf16_pairs(jnp.dot(a, wd_s[...], preferred_element_type=F32))

    @pl.when(t0 < n_used)
    def _():
        @pl.when(p == 0)
        def _():
            for cp in weight_copies(texp_ref[0]):
                cp.start(priority=1)
            ys[ring - 1][...] = jnp.zeros(ys[ring - 1].shape, U32)
            for h in range(DUMP_TILES):
                dump_fill(h).start()
            for h in range(DUMP_TILES):
                dump_fill(h).wait()
            for m in range(ring - 1):
                gather(m, xs[m], gsem.at[m])

        for m in range(ring):
            tile = t0 + m
            nm = (m + ring - 1) % ring
            switch_weights(tile)
            wait_rows(xs[m], gsem.at[m])
            if m == ring - 1:
                wait_rows(ys[m], ssem.at[m])
            else:
                @pl.when(p > 0)
                def _(m=m):
                    wait_rows(ys[m], ssem.at[m])
            gather(tile + ring - 1, xs[nm], gsem.at[nm])
            scatter(jnp.where(tile == 0, n_tiles, tile - 1), ys[nm], ssem.at[nm])
            compute(xs[m], ys[m])

        @pl.when(t0 + ring >= n_used)
        def _():
            for m in range(ring - 1):
                wait_rows(ys[m], ssem.at[m])
            scatter(t0 + ring - 1, ys[ring - 1], ssem.at[ring - 1])
            wait_rows(ys[ring - 1], ssem.at[ring - 1])
            for m in range(ring - 1):
                wait_rows(xs[m], gsem.at[m])


def _experts(tab, texp, h2p, wgu, wd, n_tiles):
    t = h2p.shape[0]
    dp = h2p.shape[1] * h2p.shape[2]
    te = EXPERT_TILE
    _, d, n_gu = wgu.shape
    de = wd.shape[1]
    assert n_tiles % EXPERT_RING == 0 and t & (t - 1) == 0
    body = functools.partial(_expert_body, n_tiles=n_tiles)
    vmem = (d * n_gu + de * d) * (4 + 2) + 2 * EXPERT_RING * te * dp * 4 + te * (n_gu + 3 * d) * 4
    hbm = pl.BlockSpec(memory_space=pl.ANY)
    row_buf = pltpu.VMEM((te, dp), U32)
    tile_buf = pltpu.VMEM((te // V7X_SUBLANES,) + h2p.shape[1:2] + (V7X_SUBLANES, V7X_LANES), U32)
    return pl.pallas_call(
        body,
        grid_spec=pltpu.PrefetchScalarGridSpec(
            num_scalar_prefetch=2,
            grid=(n_tiles // EXPERT_RING,),
            in_specs=[hbm, hbm, hbm],
            out_specs=hbm,
            scratch_shapes=[tile_buf] * EXPERT_RING + [row_buf] * EXPERT_RING + [
                pltpu.VMEM((d, n_gu), F32), pltpu.VMEM((de, d), F32),
                pltpu.VMEM((d, n_gu), BF16), pltpu.VMEM((de, d), BF16),
                pltpu.SemaphoreType.DMA((EXPERT_RING,)), pltpu.SemaphoreType.DMA((EXPERT_RING,)),
                pltpu.SemaphoreType.DMA((2,)), pltpu.SemaphoreType.DMA,
            ],
        ),
        out_shape=jax.ShapeDtypeStruct((TOP_K * t + DUMP_TILES * te, dp), U32),
        compiler_params=_params(vmem, ("arbitrary",)),
        name="experts",
    )(tab, texp, h2p, wgu, wd)


def _combine_body(x1_ref, y0_ref, y1_ref, gcol_ref, mod_ref, modf_ref, gf_ref, o_ref):
    d = x1_ref.shape[1]
    gate2 = mod_ref[0][:, 5 * d:6 * d]
    modf = modf_ref[0]
    shift_f, scale_f = modf[:, 0:d], modf[:, d:2 * d]
    y0 = _unpack_bf16_pairs(y0_ref[...])
    y1 = _unpack_bf16_pairs(y1_ref[...])
    ffn = y0 * gcol_ref[:, 0:1] + y1 * gcol_ref[:, 1:2]
    x2 = x1_ref[...] + gate2 * ffn
    o_ref[...] = _rms(x2, gf_ref[...]) * (1.0 + scale_f) + shift_f


def _combine(x1, y2, gcol, mod3, modf3, gf, seq):
    t, d = x1.shape
    tm = COMBINE_TILE
    per_seq = seq // tm
    return pl.pallas_call(
        _combine_body,
        grid=(t // tm,),
        in_specs=[
            pl.BlockSpec((tm, d), lambda i: (i, 0)),
            pl.BlockSpec((tm, d // 2), lambda i: (i, 0)),
            pl.BlockSpec((tm, d // 2), lambda i: (t // tm + i, 0)),
            pl.BlockSpec((tm, V7X_LANES), lambda i: (i, 0)),
            pl.BlockSpec((1, 1, mod3.shape[2]), lambda i: (i // per_seq, 0, 0)),
            pl.BlockSpec((1, 1, modf3.shape[2]), lambda i: (i // per_seq, 0, 0)),
            pl.BlockSpec((1, d), lambda i: (0, 0)),
        ],
        out_specs=pl.BlockSpec((tm, d), lambda i: (i, 0)),
        out_shape=jax.ShapeDtypeStruct((t, d), F32),
        compiler_params=_params(2 * tm * (3 * d + V7X_LANES) * 4, ("arbitrary",)),
        name="combine",
    )(x1, y2, y2, gcol, mod3, modf3, gf)


def kernel(x, c, w_ada, b_ada, norm1_g, w_in, w_out, gmlp_w_s, gmlp_b_s, gmlp_v_gain, conv_w, conv_b,
           norm2_g, w_router_group, b_router_group, w_router_expert, b_router_expert, w_gate_up, w_down,
           w_ada_final, b_ada_final, norm_f_g):
    bsz, seq, d = x.shape
    depth = w_ada.shape[0]
    n_e = w_router_expert.shape[2]
    t = bsz * seq
    assert seq % TOKEN_TILE == 0 and TOKEN_TILE % CHUNK == 0 and seq % COMBINE_TILE == 0
    assert w_router_group.shape[2] == N_GROUPS <= V7X_SUBLANES and n_e == N_GROUPS * V7X_SUBLANES
    assert ROUTER_ROWS == n_e + V7X_SUBLANES
    n_tiles = (t * TOP_K + n_e * (EXPERT_TILE - 1)) // EXPERT_TILE
    n_tiles = -(-n_tiles // EXPERT_RING) * EXPERT_RING

    modf3 = _ada(c, w_ada_final, b_ada_final).reshape(bsz, 1, 2 * d)
    xt = x.reshape(t, d)
    for l in range(depth):
        mod3 = _ada(c, w_ada[l], b_ada[l]).reshape(bsz, 1, -1)
        wr = jnp.concatenate([w_router_expert[l], w_router_group[l]], axis=1)
        wrt = jnp.pad(wr, ((0, 0), (0, ROUTER_ROWS - wr.shape[1]))).T.astype(BF16)
        br = jnp.concatenate([b_router_expert[l], b_router_group[l]])
        br = jnp.pad(br, (0, ROUTER_ROWS - br.shape[0])).reshape(ROUTER_ROWS, 1)
        x1, eid, gcol, h2 = _mix(
            xt, mod3, norm1_g[l].reshape(1, d), norm2_g[l].reshape(1, d),
            w_in[l], w_out[l], gmlp_w_s[l], gmlp_b_s[l].T,
            gmlp_v_gain[l].reshape(1, -1), conv_w[l], conv_b[l].reshape(1, -1), wrt, br, seq)
        texp, tab = _dispatch(eid, n_e, n_tiles, t)
        h2 = h2.reshape((t,) + h2.shape[2:])
        y2 = _experts(tab, texp.reshape(-1), h2, w_gate_up[l], w_down[l], n_tiles)
        assert depth == 1
        xt = _combine(x1, y2, gcol, mod3, modf3, norm_f_g.reshape(1, d), seq)
    return xt.reshape(bsz, seq, d)
```

```python
import functools

import jax
import jax.numpy as jnp
from jax import lax
from jax.experimental import pallas as pl
from jax.experimental.pallas import tpu as pltpu

F32 = jnp.float32
BF16 = jnp.bfloat16
I32 = jnp.int32
U32 = jnp.uint32

A_HEADS = 8
CHUNK = 128
N_GROUPS = 4
TOP_K = 2
EPS = 1e-6

V7X_LANES = 128
V7X_SUBLANES = 8
V7X_VMEM_BYTES = 64 * 1024 * 1024
V7X_VMEM_RESERVE = 6 * 1024 * 1024
VMEM_TEMP_ALLOWANCE = 8 * 1024 * 1024

TOKEN_TILE = 256
EXPERT_TILE = 256
COMBINE_TILE = 512
ADA_TILE = 1024
MIX_STAGE_ROWS = 64
MIX_STAGE_SLOTS = 4
ROUTER_ROWS = 40
COUNT_UNROLL = 16
EXPERT_RING = 3
DUMP_TILES = 2


def _rms(x, g):
    y = x * lax.rsqrt(jnp.mean(x * x, axis=-1, keepdims=True) + EPS)
    return y * g


def _pack_bf16_pairs(x):
    n = x.shape[1] // 2
    lo = lax.bitcast_convert_type(x[:, :n].astype(BF16).astype(F32), U32)
    hi = lax.bitcast_convert_type(x[:, n:].astype(BF16).astype(F32), U32)
    return hi | (lo >> 16)


def _unpack_bf16_pairs(w):
    lo = lax.bitcast_convert_type(w << 16, F32)
    hi = lax.bitcast_convert_type(w & jnp.uint32(0xFFFF0000), F32)
    return jnp.concatenate([lo, hi], axis=1)


def _zero_after(anchor, n):
    bits = lax.bitcast_convert_type(anchor, U32)
    zero = lax.bitcast_convert_type((bits >> 16) >> 16, F32)
    return jnp.concatenate([zero] * (n // anchor.shape[1]), axis=1)


def _params(buffer_bytes, semantics=None):
    vmem_bytes = min(buffer_bytes + VMEM_TEMP_ALLOWANCE, V7X_VMEM_BYTES - V7X_VMEM_RESERVE)
    kw = dict(vmem_limit_bytes=int(vmem_bytes))
    if semantics is not None:
        kw["dimension_semantics"] = semantics
    return pltpu.CompilerParams(**kw)


def _ada_body(c_ref, w_ref, b_ref, o_ref):
    ca = jax.nn.silu(c_ref[...]).astype(BF16)
    o_ref[...] = jnp.dot(ca, w_ref[...].astype(BF16), preferred_element_type=F32) + b_ref[...]


def _ada(c, w, b):
    bsz, d = c.shape
    n = w.shape[1]
    tn = ADA_TILE
    return pl.pallas_call(
        _ada_body,
        grid=(n // tn,),
        in_specs=[
            pl.BlockSpec((bsz, d), lambda j: (0, 0)),
            pl.BlockSpec((d, tn), lambda j: (0, j)),
            pl.BlockSpec((1, tn), lambda j: (0, j)),
        ],
        out_specs=pl.BlockSpec((bsz, tn), lambda j: (0, j)),
        out_shape=jax.ShapeDtypeStruct((bsz, n), F32),
        compiler_params=_params(2 * d * tn * 4 + d * tn * 2, ("arbitrary",)),
        name="ada",
    )(c, w, b.reshape(1, n))


def _route(lt, tm):
    n_e = lt.shape[0] - V7X_SUBLANES
    epg = n_e // N_GROUPS
    row = lax.broadcasted_iota(I32, (V7X_SUBLANES, tm), 0).astype(F32)
    gl = lt[n_e:n_e + V7X_SUBLANES]
    gvalid = row < float(N_GROUPS)
    glm = jnp.where(gvalid, gl, -jnp.inf)
    gmax = jnp.max(glm, axis=0, keepdims=True)
    garg = jnp.min(jnp.where(glm == gmax, row, float(V7X_SUBLANES)), axis=0, keepdims=True)
    gsum = jnp.sum(jnp.where(gvalid, jnp.exp(gl - gmax), 0.0), axis=0, keepdims=True)
    p_grp = 1.0 / gsum
    es = lt[0:epg]
    for g in range(1, N_GROUPS):
        es = jnp.where(garg == float(g), lt[g * epg:(g + 1) * epg], es)
    m1 = jnp.max(es, axis=0, keepdims=True)
    i1 = jnp.min(jnp.where(es == m1, row, float(epg)), axis=0, keepdims=True)
    es2 = jnp.where(row == i1, -jnp.inf, es)
    m2 = jnp.max(es2, axis=0, keepdims=True)
    i2 = jnp.min(jnp.where(es2 == m2, row, float(epg)), axis=0, keepdims=True)
    z = jnp.exp(m2 - m1)
    den = 1.0 + z
    g0 = p_grp / den
    g1 = p_grp * z / den
    e0 = garg * float(epg) + i1
    e1 = garg * float(epg) + i2
    zero = jnp.zeros_like(g0)
    return jnp.concatenate([g0, g1, e0, e1, zero, zero, zero, zero], axis=0)


def _load_cast(src_hbm, dst, stage, sem):
    slots, rows, width = stage.shape
    cols = src_hbm.shape[1]
    parts = width // cols
    n = src_hbm.shape[0] // (rows * parts)
    ahead = slots - 1

    def copies(c, slot):
        return [pltpu.make_async_copy(src_hbm.at[pl.ds((c * parts + q) * rows, rows), :],
                                      stage.at[slot, :, pl.ds(q * cols, cols)], sem.at[slot])
                for q in range(parts)]

    for c in range(ahead):
        for cp in copies(c, c):
            cp.start()

    def body(c, carry):
        slot = c % slots

        @pl.when(c + ahead < n)
        def _():
            for cp in copies(c + ahead, (c + ahead) % slots):
                cp.start()

        for cp in copies(c, slot):
            cp.wait()
        for q in range(parts):
            r0 = pl.multiple_of((c * parts + q) * rows, rows)
            dst[pl.ds(r0, rows), :] = stage[slot, :, q * cols:(q + 1) * cols].astype(BF16)
        return carry

    lax.fori_loop(0, n, body, 0)


def _mix_body(x_ref, xp_ref, mod_ref, g1_ref, g2_ref, win_hbm, wout_hbm, ws_ref, bst_ref, vg_ref,
              cw_ref, cb_ref, wrt_ref, br_ref,
              x1_ref, eid_ref, gcol_ref, h2_hbm,
              pre_scr, mix_scr, h2_scr, win_s, wout_s, stage, h2_sem, w_sem, *, tiles_per_seq):
    tm, d = x_ref.shape
    aw = vg_ref.shape[1]
    bw = cb_ref.shape[1]
    hd = aw // A_HEADS
    nch = tm // CHUNK
    i = pl.program_id(0)
    last = pl.num_programs(0) - 1
    groups = tm // V7X_SUBLANES
    slot = i % 2

    def h2_copies(tile, sl):
        return [pltpu.make_async_copy(h2_scr.at[sl, :, s], h2_hbm.at[pl.ds(tile * groups, groups), :, s, :],
                                      h2_sem.at[sl])
                for s in range(V7X_SUBLANES)]

    @pl.when(i == 0)
    def _():
        mix_scr[...] = jnp.zeros(mix_scr.shape, F32)
        _load_cast(win_hbm, win_s, stage, w_sem)
        _load_cast(wout_hbm, wout_s, stage, w_sem)

    @pl.when(i > 2)
    def _():
        for cp in h2_copies(0, slot):
            cp.wait()

    def last_stage(anchor):
        rowp = pl.ds(jnp.maximum(i - 1, 0) // tiles_per_seq, 1)
        gate1p, shift2p, scale2p = (mod_ref[rowp, k * d:(k + 1) * d] for k in (2, 3, 4))
        if anchor is not None:
            gate1p = gate1p + _zero_after(anchor, d)
        x1 = xp_ref[...] + gate1p * mix_scr[...]
        x1_ref[...] = x1
        h2 = _rms(x1, g2_ref[...]) * (1.0 + scale2p) + shift2p
        words = _pack_bf16_pairs(h2)
        for s in range(V7X_SUBLANES):
            h2_scr[slot, :, s] = words[:, s * V7X_LANES:(s + 1) * V7X_LANES].reshape(
                groups, V7X_SUBLANES, V7X_LANES)
        lt = lax.dot_general(wrt_ref[...], h2.astype(BF16), (((1,), (1,)), ((), ())),
                             preferred_element_type=F32) + br_ref[...]
        slab = _route(lt, tm)
        eid_ref[0] = slab[2:4].astype(I32)
        wide = jnp.concatenate([slab, jnp.zeros((V7X_LANES - V7X_SUBLANES, tm), F32)], axis=0)
        gcol_ref[...] = wide.T

    @pl.when(i < last)
    def _():
        x = x_ref[...]
        row = pl.ds(i // tiles_per_seq, 1)
        shift1, scale1 = mod_ref[row, 0:d], mod_ref[row, d:2 * d]
        h = _rms(x, g1_ref[...]) * (1.0 + scale1) + shift1
        hb = h.astype(BF16)

        uv = jnp.dot(hb, win_s[:, 0:2 * aw], preferred_element_type=F32)
        last_stage(uv[0:1, 0:V7X_LANES])
        uv = jax.nn.gelu(uv)
        u, v = uv[:, :aw], uv[:, aw:]
        tq_r = lax.broadcasted_iota(I32, (CHUNK, CHUNK), 0)
        tq_c = lax.broadcasted_iota(I32, (CHUNK, CHUNK), 1)
        causal = tq_c <= tq_r
        ya = []
        for hh in range(A_HEADS):
            sl = slice(hh * hd, (hh + 1) * hd)
            vh = _rms(v[:, sl], vg_ref[:, sl]).astype(BF16)
            rhs = jnp.concatenate([vh[c * CHUNK:(c + 1) * CHUNK] for c in range(nch)], axis=1)
            w = jnp.where(causal, ws_ref[hh], 0.0).astype(BF16)
            zs = jnp.dot(w, rhs, preferred_element_type=F32) + bst_ref[:, hh:hh + 1]
            zs = jnp.concatenate([zs[:, c * hd:(c + 1) * hd] for c in range(nch)], axis=0)
            ya.append(u[:, sl] * zs)

        bcx = jnp.dot(hb, win_s[:, 2 * aw:], preferred_element_type=F32)
        bg, cg, xin = bcx[:, :bw], bcx[:, bw:2 * bw], bcx[:, 2 * bw:]
        pre = cg * xin

        @pl.when(i % tiles_per_seq == 0)
        def _():
            pre_scr[0:V7X_SUBLANES, :] = jnp.zeros((V7X_SUBLANES, bw), F32)

        pre_scr[V7X_SUBLANES:V7X_SUBLANES + tm, :] = pre
        p1 = pre_scr[V7X_SUBLANES - 1:V7X_SUBLANES - 1 + tm, :]
        p2 = pre_scr[V7X_SUBLANES - 2:V7X_SUBLANES - 2 + tm, :]
        conv = cw_ref[0:1, :] * p2 + cw_ref[1:2, :] * p1 + cw_ref[2:3, :] * pre + cb_ref[...]
        yb = bg * conv
        pre_scr[0:V7X_SUBLANES, :] = pre_scr[tm:tm + V7X_SUBLANES, :]

        y = jnp.concatenate(ya + [yb], axis=1).astype(BF16)
        mix_scr[...] = jnp.dot(y, wout_s[...], preferred_element_type=F32)

    @pl.when(i == last)
    def _():
        last_stage(None)

    @pl.when(i > 0)
    def _():
        for cp in h2_copies(i - 1, slot):
            cp.start()

    @pl.when(i == last)
    def _():
        for sl in range(2):
            for cp in h2_copies(0, sl):
                cp.wait()


def _mix(xt, mod, g1, g2, win, wout, ws, bst, vg, cw, cb, wrt, br, seq):
    t, d = xt.shape
    tm = TOKEN_TILE
    nt = t // tm
    per_seq = seq // tm
    in_cols = win.shape[1]
    bw = cb.shape[1]
    words_per_row = d // 2
    assert words_per_row == V7X_SUBLANES * V7X_LANES
    assert d % (MIX_STAGE_ROWS * (in_cols // wout.shape[1])) == 0 and d // MIX_STAGE_ROWS >= 2 * MIX_STAGE_SLOTS
    const2 = lambda i: (0, 0)
    body = functools.partial(_mix_body, tiles_per_seq=per_seq)
    cur = lambda i: jnp.minimum(i, nt - 1)
    prev = lambda i: jnp.maximum(i - 1, 0)
    hbm = pl.BlockSpec(memory_space=pl.ANY)
    vmem = (2 * d * (in_cols + d)
            + MIX_STAGE_SLOTS * MIX_STAGE_ROWS * in_cols * 4
            + 3 * 2 * tm * d * 4
            + 2 * tm * d * 2
            + (tm + V7X_SUBLANES) * bw * 4 + tm * d * 4
            + tm * (in_cols + 4 * d) * 4)
    return pl.pallas_call(
        body,
        grid=(nt + 1,),
        in_specs=[
            pl.BlockSpec((tm, d), lambda i: (cur(i), 0)),
            pl.BlockSpec((tm, d), lambda i: (prev(i), 0)),
            pl.BlockSpec(mod.shape, const2),
            pl.BlockSpec((1, d), const2),
            pl.BlockSpec((1, d), const2),
            hbm,
            hbm,
            pl.BlockSpec(ws.shape, lambda i: (0, 0, 0)),
            pl.BlockSpec(bst.shape, const2),
            pl.BlockSpec(vg.shape, const2),
            pl.BlockSpec(cw.shape, const2),
            pl.BlockSpec(cb.shape, const2),
            pl.BlockSpec(wrt.shape, const2),
            pl.BlockSpec(br.shape, const2),
        ],
        out_specs=[
            pl.BlockSpec((tm, d), lambda i: (prev(i), 0)),
            pl.BlockSpec((1, TOP_K, tm), lambda i: (prev(i), 0, 0)),
            pl.BlockSpec((tm, V7X_LANES), lambda i: (prev(i), 0)),
            hbm,
        ],
        out_shape=[
            jax.ShapeDtypeStruct((t, d), F32),
            jax.ShapeDtypeStruct((nt, TOP_K, tm), I32),
            jax.ShapeDtypeStruct((t, V7X_LANES), F32),
            jax.ShapeDtypeStruct((t // V7X_SUBLANES, V7X_SUBLANES, words_per_row // V7X_LANES, V7X_LANES), U32),
        ],
        scratch_shapes=[
            pltpu.VMEM((tm + V7X_SUBLANES, bw), F32),
            pltpu.VMEM((tm, d), F32),
            pltpu.VMEM((2, tm // V7X_SUBLANES, words_per_row // V7X_LANES, V7X_SUBLANES, V7X_LANES), U32),
            pltpu.VMEM(win.shape, BF16),
            pltpu.VMEM(wout.shape, BF16),
            pltpu.VMEM((MIX_STAGE_SLOTS, MIX_STAGE_ROWS, in_cols), F32),
            pltpu.SemaphoreType.DMA((2,)),
            pltpu.SemaphoreType.DMA((MIX_STAGE_SLOTS,)),
        ],
        compiler_params=_params(vmem, ("arbitrary",)),
        name="mix",
    )(xt, xt, mod, g1, g2, win, wout, ws, bst, vg, cw, cb, wrt, br)


def _dispatch_body(eid_ref, texp_ref, tab_ref, rank_scr, dest_v, dest_s, init_v, sem,
                   *, n_e, n_tiles, n_tok):
    nch, _, c = eid_ref.shape
    te = EXPERT_TILE
    r = lax.broadcasted_iota(I32, (c, c), 0)
    q = lax.broadcasted_iota(I32, (c, c), 1)
    before = (r < q).astype(BF16)
    e_iota = lax.broadcasted_iota(I32, (n_e, c), 0)

    init_v[...] = (lax.broadcasted_iota(I32, init_v.shape, 0) & (DUMP_TILES * te - 1)) + TOP_K * n_tok
    cp_i = pltpu.make_async_copy(init_v, tab_ref, sem.at[0])
    cp_i.start()

    def count_body(ch, carry):
        e2 = eid_ref[ch]
        ranks = []
        for k in range(TOP_K):
            oh = e_iota == e2[k:k + 1]
            ohf = oh.astype(F32)
            pref = jnp.dot(ohf.astype(BF16), before, preferred_element_type=F32)
            ranks.append(jnp.sum(jnp.where(oh, pref + carry, 0.0), axis=0, keepdims=True))
            carry = carry + jnp.sum(ohf, axis=1, keepdims=True)
        rank_scr[ch] = jnp.concatenate(ranks, axis=0)
        return carry

    counts = lax.fori_loop(0, nch, count_body, jnp.zeros((n_e, 1), F32), unroll=COUNT_UNROLL)

    padded = jnp.floor((counts + float(te - 1)) / float(te)) * float(te)
    sub = lax.broadcasted_iota(I32, (n_e, V7X_LANES), 0)
    lane = lax.broadcasted_iota(I32, (n_e, V7X_LANES), 1)
    pstart_row = jnp.sum(jnp.where(sub < lane, padded, 0.0), axis=0, keepdims=True)
    pstart = jnp.sum(jnp.where(sub == lane, pstart_row, 0.0), axis=1, keepdims=True)
    pend = pstart + padded
    total = jnp.max(pend, axis=0, keepdims=True)
    last_e = jnp.max(jnp.where(counts > 0.0, sub[:, 0:1].astype(F32), -1.0), axis=0, keepdims=True)

    rows_k = n_tok // V7X_LANES
    group = V7X_SUBLANES * V7X_LANES // c

    def dest_body(g, carry):
        rows = [[] for _ in range(TOP_K)]
        for j in range(group):
            e2 = eid_ref[g * group + j]
            rk = rank_scr[g * group + j]
            for k in range(TOP_K):
                oh = e_iota == e2[k:k + 1]
                off = jnp.sum(jnp.where(oh, pstart, 0.0), axis=0, keepdims=True)
                dest = (off + rk[k:k + 1]).astype(I32)
                rows[k] += [dest[:, h * V7X_LANES:(h + 1) * V7X_LANES] for h in range(c // V7X_LANES)]
        for k in range(TOP_K):
            word0 = pl.multiple_of((k * rows_k + g * V7X_SUBLANES) * V7X_LANES, V7X_SUBLANES * V7X_LANES)
            dest_v[pl.ds(word0, V7X_SUBLANES * V7X_LANES)] = jnp.concatenate(rows[k], axis=0).reshape(-1)
        return carry

    lax.fori_loop(0, nch // group, dest_body, 0)
    cp_d = pltpu.make_async_copy(dest_v, dest_s, sem.at[1])
    cp_d.start()

    w = texp_ref.shape[1]
    tj = lax.broadcasted_iota(I32, (n_e, w), 1).astype(F32) * float(te)
    texp = jnp.sum((pend <= tj).astype(F32), axis=0, keepdims=True)
    tj1 = tj[0:1]
    texp = jnp.where(tj1 < total, jnp.minimum(texp, float(n_e - 1)), last_e)
    lane_w = lax.broadcasted_iota(I32, (1, w), 1)
    texp = jnp.where(lane_w == n_tiles, total / float(te), texp)
    texp_ref[...] = texp.astype(I32)

    cp_i.wait()
    cp_d.wait()

    def lane_body(lane, carry):
        for row in range(TOP_K * rows_k):
            a = row * V7X_LANES + lane
            tab_ref[dest_s[a]] = a
        return carry

    lax.fori_loop(0, V7X_LANES, lane_body, 0)


def _dispatch(eid, n_e, n_tiles, n_tok):
    nch, k, c = eid.shape
    assert c == EXPERT_TILE and c % V7X_LANES == 0 and nch * c == n_tok and n_tok % (V7X_SUBLANES * V7X_LANES) == 0
    body = functools.partial(_dispatch_body, n_e=n_e, n_tiles=n_tiles, n_tok=n_tok)
    w = 2 * V7X_LANES
    assert n_tiles < w
    rows = n_tiles + V7X_SUBLANES
    rows += (-rows) % 4
    smem = pl.BlockSpec(memory_space=pltpu.SMEM)
    return pl.pallas_call(
        body,
        out_specs=[pl.BlockSpec(memory_space=pltpu.VMEM), smem],
        out_shape=[
            jax.ShapeDtypeStruct((1, w), I32),
            jax.ShapeDtypeStruct((rows * EXPERT_TILE,), I32),
        ],
        scratch_shapes=[
            pltpu.VMEM((nch, k, c), F32),
            pltpu.VMEM((k * n_tok,), I32),
            pltpu.SMEM((k * n_tok,), I32),
            pltpu.VMEM((rows * EXPERT_TILE,), I32),
            pltpu.SemaphoreType.DMA((2,)),
        ],
        compiler_params=_params(nch * V7X_SUBLANES * c * 4 + (k * n_tok + rows * EXPERT_TILE) * 4 + 4 * c * c * 4),
        name="dispatch",
    )(eid)


def _expert_body(tab_ref, texp_ref, h2_hbm, wgu_hbm, wd_hbm, y2_hbm,
                 x0, x1, x2, y0, y1, y2, wgu_f, wd_f, wgu_s, wd_s, gsem, ssem, wsem, zsem, *, n_tiles):
    te = EXPERT_TILE
    ring = EXPERT_RING
    de = wd_s.shape[0]
    n_tok = h2_hbm.shape[0]
    xs, ys = (x0, x1, x2), (y0, y1, y2)
    p = pl.program_id(0)
    n_used = texp_ref[n_tiles]
    t0 = ring * p

    def gather(tile, buf, sem):
        for r in range(te):
            tok = tab_ref[tile * te + r] & (n_tok - 1)
            pltpu.make_async_copy(h2_hbm.at[tok], buf.at[r // V7X_SUBLANES, :, r % V7X_SUBLANES, :], sem).start()

    def scatter(tile, buf, sem):
        for r in range(te):
            row = tab_ref[tile * te + r]
            pltpu.make_async_copy(buf.at[pl.ds(r, 1), :], y2_hbm.at[pl.ds(row, 1), :], sem).start()

    def wait_rows(buf, sem):
        pltpu.make_async_copy(buf, buf, sem).wait()

    def weight_copies(e):
        return (pltpu.make_async_copy(wgu_hbm.at[e], wgu_f, wsem.at[0]),
                pltpu.make_async_copy(wd_hbm.at[e], wd_f, wsem.at[1]))

    def dump_fill(h):
        rows = pl.ds(TOP_K * n_tok + h * te, te)
        return pltpu.make_async_copy(ys[ring - 1], y2_hbm.at[rows, :], zsem)

    def switch_weights(tile):
        e = texp_ref[tile]
        first = jnp.logical_or(tile == 0, texp_ref[jnp.maximum(tile - 1, 0)] != e)

        @pl.when(jnp.logical_and(first, tile < n_used))
        def _():
            for cp in weight_copies(e):
                cp.wait()
            wgu_s[...] = wgu_f[...].astype(BF16)
            wd_s[...] = wd_f[...].astype(BF16)
            nxt = lax.while_loop(
                lambda k: jnp.logical_and(k < n_used, texp_ref[jnp.minimum(k, n_tiles - 1)] == e),
                lambda k: k + 1, tile + 1)

            @pl.when(nxt < n_used)
            def _():
                for cp in weight_copies(texp_ref[jnp.minimum(nxt, n_tiles - 1)]):
                    cp.start(priority=1)

    def compute(xbuf, ybuf):
        words = jnp.concatenate([xbuf[:, s].reshape(te, V7X_LANES) for s in range(V7X_SUBLANES)], axis=1)
        xb16 = _unpack_bf16_pairs(words).astype(BF16)
        gu = jnp.dot(xb16, wgu_s[...], preferred_element_type=F32)
        a = (jax.nn.silu(gu[:, :de]) * gu[:, de:]).astype(BF16)
        ybuf[...] = _pack_bf16_pairs(jnp.dot(a, wd_s[...], preferred_element_type=F32))

    @pl.when(t0 < n_used)
    def _():
        @pl.when(p == 0)
        def _():
            for cp in weight_copies(texp_ref[0]):
                cp.start(priority=1)
            ys[ring - 1][...] = jnp.zeros(ys[ring - 1].shape, U32)
            for h in range(DUMP_TILES):
                dump_fill(h).start()
            for h in range(DUMP_TILES):
                dump_fill(h).wait()
            for m in range(ring - 1):
                gather(m, xs[m], gsem.at[m])

        for m in range(ring):
            tile = t0 + m
            nm = (m + ring - 1) % ring
            switch_weights(tile)
            wait_rows(xs[m], gsem.at[m])
            if m == ring - 1:
                wait_rows(ys[m], ssem.at[m])
            else:
                @pl.when(p > 0)
                def _(m=m):
                    wait_rows(ys[m], ssem.at[m])
            gather(tile + ring - 1, xs[nm], gsem.at[nm])
            scatter(jnp.where(tile == 0, n_tiles, tile - 1), ys[nm], ssem.at[nm])
            compute(xs[m], ys[m])

        @pl.when(t0 + ring >= n_used)
        def _():
            for m in range(ring - 1):
                wait_rows(ys[m], ssem.at[m])
            scatter(t0 + ring - 1, ys[ring - 1], ssem.at[ring - 1])
            wait_rows(ys[ring - 1], ssem.at[ring - 1])
            for m in range(ring - 1):
                wait_rows(xs[m], gsem.at[m])


def _experts(tab, texp, h2p, wgu, wd, n_tiles):
    t = h2p.shape[0]
    dp = h2p.shape[1] * h2p.shape[2]
    te = EXPERT_TILE
    _, d, n_gu = wgu.shape
    de = wd.shape[1]
    assert n_tiles % EXPERT_RING == 0 and t & (t - 1) == 0
    body = functools.partial(_expert_body, n_tiles=n_tiles)
    vmem = (d * n_gu + de * d) * (4 + 2) + 2 * EXPERT_RING * te * dp * 4 + te * (n_gu + 3 * d) * 4
    hbm = pl.BlockSpec(memory_space=pl.ANY)
    row_buf = pltpu.VMEM((te, dp), U32)
    tile_buf = pltpu.VMEM((te // V7X_SUBLANES,) + h2p.shape[1:2] + (V7X_SUBLANES, V7X_LANES), U32)
    return pl.pallas_call(
        body,
        grid_spec=pltpu.PrefetchScalarGridSpec(
            num_scalar_prefetch=2,
            grid=(n_tiles // EXPERT_RING,),
            in_specs=[hbm, hbm, hbm],
            out_specs=hbm,
            scratch_shapes=[tile_buf] * EXPERT_RING + [row_buf] * EXPERT_RING + [
                pltpu.VMEM((d, n_gu), F32), pltpu.VMEM((de, d), F32),
                pltpu.VMEM((d, n_gu), BF16), pltpu.VMEM((de, d), BF16),
                pltpu.SemaphoreType.DMA((EXPERT_RING,)), pltpu.SemaphoreType.DMA((EXPERT_RING,)),
                pltpu.SemaphoreType.DMA((2,)), pltpu.SemaphoreType.DMA,
            ],
        ),
        out_shape=jax.ShapeDtypeStruct((TOP_K * t + DUMP_TILES * te, dp), U32),
        compiler_params=_params(vmem, ("arbitrary",)),
        name="experts",
    )(tab, texp, h2p, wgu, wd)


def _combine_body(x1_ref, y0_ref, y1_ref, gcol_ref, mod_ref, modf_ref, gf_ref, o_ref, *, tiles_per_seq):
    d = x1_ref.shape[1]
    row = pl.ds(pl.program_id(0) // tiles_per_seq, 1)
    gate2 = mod_ref[row, 5 * d:6 * d]
    shift_f, scale_f = modf_ref[row, 0:d], modf_ref[row, d:2 * d]
    y0 = _unpack_bf16_pairs(y0_ref[...])
    y1 = _unpack_bf16_pairs(y1_ref[...])
    ffn = y0 * gcol_ref[:, 0:1] + y1 * gcol_ref[:, 1:2]
    x2 = x1_ref[...] + gate2 * ffn
    o_ref[...] = _rms(x2, gf_ref[...]) * (1.0 + scale_f) + shift_f


def _combine(x1, y2, gcol, mod, modf, gf, seq):
    t, d = x1.shape
    tm = COMBINE_TILE
    per_seq = seq // tm
    return pl.pallas_call(
        functools.partial(_combine_body, tiles_per_seq=per_seq),
        grid=(t // tm,),
        in_specs=[
            pl.BlockSpec((tm, d), lambda i: (i, 0)),
            pl.BlockSpec((tm, d // 2), lambda i: (i, 0)),
            pl.BlockSpec((tm, d // 2), lambda i: (t // tm + i, 0)),
            pl.BlockSpec((tm, V7X_LANES), lambda i: (i, 0)),
            pl.BlockSpec(mod.shape, lambda i: (0, 0)),
            pl.BlockSpec(modf.shape, lambda i: (0, 0)),
            pl.BlockSpec((1, d), lambda i: (0, 0)),
        ],
        out_specs=pl.BlockSpec((tm, d), lambda i: (i, 0)),
        out_shape=jax.ShapeDtypeStruct((t, d), F32),
        compiler_params=_params(2 * tm * (3 * d + V7X_LANES) * 4, ("arbitrary",)),
        name="combine",
    )(x1, y2, y2, gcol, mod, modf, gf)


def kernel(x, c, w_ada, b_ada, norm1_g, w_in, w_out, gmlp_w_s, gmlp_b_s, gmlp_v_gain, conv_w, conv_b,
           norm2_g, w_router_group, b_router_group, w_router_expert, b_router_expert, w_gate_up, w_down,
           w_ada_final, b_ada_final, norm_f_g):
    bsz, seq, d = x.shape
    depth = w_ada.shape[0]
    n_e = w_router_expert.shape[2]
    t = bsz * seq
    assert seq % TOKEN_TILE == 0 and TOKEN_TILE % CHUNK == 0 and seq % COMBINE_TILE == 0
    assert w_router_group.shape[2] == N_GROUPS <= V7X_SUBLANES and n_e == N_GROUPS * V7X_SUBLANES
    assert ROUTER_ROWS == n_e + V7X_SUBLANES
    n_tiles = (t * TOP_K + n_e * (EXPERT_TILE - 1)) // EXPERT_TILE
    n_tiles = -(-n_tiles // EXPERT_RING) * EXPERT_RING

    modf = _ada(c, w_ada_final, b_ada_final)
    xt = x.reshape(t, d)
    for l in range(depth):
        mod = _ada(c, w_ada[l], b_ada[l])
        wr = jnp.concatenate([w_router_expert[l], w_router_group[l]], axis=1)
        wrt = jnp.pad(wr, ((0, 0), (0, ROUTER_ROWS - wr.shape[1]))).T.astype(BF16)
        br = jnp.concatenate([b_router_expert[l], b_router_group[l]])
        br = jnp.pad(br, (0, ROUTER_ROWS - br.shape[0])).reshape(ROUTER_ROWS, 1)
        x1, eid, gcol, h2 = _mix(
            xt, mod, norm1_g[l].reshape(1, d), norm2_g[l].reshape(1, d),
            w_in[l], w_out[l], gmlp_w_s[l], gmlp_b_s[l].T,
            gmlp_v_gain[l].reshape(1, -1), conv_w[l], conv_b[l].reshape(1, -1), wrt, br, seq)
        texp, tab = _dispatch(eid, n_e, n_tiles, t)
        h2 = h2.reshape((t,) + h2.shape[2:])
        y2 = _experts(tab, texp.reshape(-1), h2, w_gate_up[l], w_down[l], n_tiles)
        assert depth == 1
        xt = _combine(x1, y2, gcol, mod, modf, norm_f_g.reshape(1, d), seq)
    return xt.reshape(bsz, seq, d)
```

```python
import functools

import jax
import jax.numpy as jnp
from jax import lax
from jax.experimental import pallas as pl
from jax.experimental.pallas import tpu as pltpu

F32 = jnp.float32
BF16 = jnp.bfloat16
I32 = jnp.int32
U32 = jnp.uint32

A_HEADS = 8
CHUNK = 128
N_GROUPS = 4
TOP_K = 2
EPS = 1e-6

V7X_LANES = 128
V7X_SUBLANES = 8
V7X_VMEM_BYTES = 64 * 1024 * 1024
V7X_VMEM_RESERVE = 6 * 1024 * 1024
VMEM_TEMP_ALLOWANCE = 8 * 1024 * 1024

TOKEN_TILE = 256
EXPERT_TILE = 256
COMBINE_TILE = 512
ADA_TILE = 1024
MIX_STAGE_ROWS = 64
MIX_STAGE_SLOTS = 4
ROUTER_ROWS = 40
COUNT_UNROLL = 16
EXPERT_RING = 3
DUMP_TILES = 2


def _rms(x, g):
    y = x * lax.rsqrt(jnp.mean(x * x, axis=-1, keepdims=True) + EPS)
    return y * g


def _pack_bf16_pairs(x):
    n = x.shape[1] // 2
    lo = lax.bitcast_convert_type(x[:, :n].astype(BF16).astype(F32), U32)
    hi = lax.bitcast_convert_type(x[:, n:].astype(BF16).astype(F32), U32)
    return hi | (lo >> 16)


def _unpack_bf16_pairs(w):
    lo = lax.bitcast_convert_type(w << 16, F32)
    hi = lax.bitcast_convert_type(w & jnp.uint32(0xFFFF0000), F32)
    return jnp.concatenate([lo, hi], axis=1)


def _zero_after(anchor, n):
    bits = lax.bitcast_convert_type(anchor, U32)
    zero = lax.bitcast_convert_type((bits >> 16) >> 16, F32)
    return jnp.concatenate([zero] * (n // anchor.shape[1]), axis=1)


def _params(buffer_bytes, semantics=None):
    vmem_bytes = min(buffer_bytes + VMEM_TEMP_ALLOWANCE, V7X_VMEM_BYTES - V7X_VMEM_RESERVE)
    kw = dict(vmem_limit_bytes=int(vmem_bytes))
    if semantics is not None:
        kw["dimension_semantics"] = semantics
    return pltpu.CompilerParams(**kw)


def _ada_body(c_ref, w_ref, b_ref, o_ref):
    ca = jax.nn.silu(c_ref[...]).astype(BF16)
    o_ref[...] = jnp.dot(ca, w_ref[...].astype(BF16), preferred_element_type=F32) + b_ref[...]


def _ada(c, w, b):
    bsz, d = c.shape
    n = w.shape[1]
    tn = ADA_TILE
    return pl.pallas_call(
        _ada_body,
        grid=(n // tn,),
        in_specs=[
            pl.BlockSpec((bsz, d), lambda j: (0, 0)),
            pl.BlockSpec((d, tn), lambda j: (0, j)),
            pl.BlockSpec((1, tn), lambda j: (0, j)),
        ],
        out_specs=pl.BlockSpec((bsz, tn), lambda j: (0, j)),
        out_shape=jax.ShapeDtypeStruct((bsz, n), F32),
        compiler_params=_params(2 * d * tn * 4 + d * tn * 2, ("arbitrary",)),
        name="ada",
    )(c, w, b.reshape(1, n))


def _route(lt, tm):
    n_e = lt.shape[0] - V7X_SUBLANES
    epg = n_e // N_GROUPS
    row = lax.broadcasted_iota(I32, (V7X_SUBLANES, tm), 0).astype(F32)
    gl = lt[n_e:n_e + V7X_SUBLANES]
    gvalid = row < float(N_GROUPS)
    glm = jnp.where(gvalid, gl, -jnp.inf)
    gmax = jnp.max(glm, axis=0, keepdims=True)
    garg = jnp.min(jnp.where(glm == gmax, row, float(V7X_SUBLANES)), axis=0, keepdims=True)
    gsum = jnp.sum(jnp.where(gvalid, jnp.exp(gl - gmax), 0.0), axis=0, keepdims=True)
    p_grp = 1.0 / gsum
    es = lt[0:epg]
    for g in range(1, N_GROUPS):
        es = jnp.where(garg == float(g), lt[g * epg:(g + 1) * epg], es)
    m1 = jnp.max(es, axis=0, keepdims=True)
    i1 = jnp.min(jnp.where(es == m1, row, float(epg)), axis=0, keepdims=True)
    es2 = jnp.where(row == i1, -jnp.inf, es)
    m2 = jnp.max(es2, axis=0, keepdims=True)
    i2 = jnp.min(jnp.where(es2 == m2, row, float(epg)), axis=0, keepdims=True)
    z = jnp.exp(m2 - m1)
    den = 1.0 + z
    g0 = p_grp / den
    g1 = p_grp * z / den
    e0 = garg * float(epg) + i1
    e1 = garg * float(epg) + i2
    zero = jnp.zeros_like(g0)
    return jnp.concatenate([g0, g1, e0, e1, zero, zero, zero, zero], axis=0)


def _load_cast(src_hbm, dst, stage, sem):
    slots, rows, width = stage.shape
    cols = src_hbm.shape[1]
    parts = width // cols
    n = src_hbm.shape[0] // (rows * parts)
    ahead = slots - 1

    def copies(c, slot):
        return [pltpu.make_async_copy(src_hbm.at[pl.ds((c * parts + q) * rows, rows), :],
                                      stage.at[slot, :, pl.ds(q * cols, cols)], sem.at[slot])
                for q in range(parts)]

    for c in range(ahead):
        for cp in copies(c, c):
            cp.start()

    def body(c, carry):
        slot = c % slots

        @pl.when(c + ahead < n)
        def _():
            for cp in copies(c + ahead, (c + ahead) % slots):
                cp.start()

        for cp in copies(c, slot):
            cp.wait()
        for q in range(parts):
            r0 = pl.multiple_of((c * parts + q) * rows, rows)
            dst[pl.ds(r0, rows), :] = stage[slot, :, q * cols:(q + 1) * cols].astype(BF16)
        return carry

    lax.fori_loop(0, n, body, 0)


def _mix_body(x_ref, xp_ref, mod_ref, modp_ref, g1_ref, g2_ref, win_hbm, wout_hbm, ws_ref, bst_ref, vg_ref,
              cw_ref, cb_ref, wrt_ref, br_ref, wf_ref,
              x1_ref, eid_ref, gcol_ref, wf16_ref, h2_hbm,
              pre_scr, mix_scr, h2_scr, win_s, wout_s, stage, h2_sem, w_sem, *, tiles_per_seq):
    tm, d = x_ref.shape
    aw = vg_ref.shape[1]
    bw = cb_ref.shape[1]
    hd = aw // A_HEADS
    nch = tm // CHUNK
    i = pl.program_id(0)
    last = pl.num_programs(0) - 1
    groups = tm // V7X_SUBLANES
    slot = i % 2

    def h2_copies(tile, sl):
        return [pltpu.make_async_copy(h2_scr.at[sl, :, s], h2_hbm.at[pl.ds(tile * groups, groups), :, s, :],
                                      h2_sem.at[sl])
                for s in range(V7X_SUBLANES)]

    @pl.when(i == 0)
    def _():
        mix_scr[...] = jnp.zeros(mix_scr.shape, F32)
        _load_cast(win_hbm, win_s, stage, w_sem)
        _load_cast(wout_hbm, wout_s, stage, w_sem)

    @pl.when(i > 2)
    def _():
        for cp in h2_copies(0, slot):
            cp.wait()

    def last_stage(anchor):
        modp = modp_ref[0]
        gate1p, shift2p, scale2p = modp[:, 2 * d:3 * d], modp[:, 3 * d:4 * d], modp[:, 4 * d:5 * d]
        if anchor is not None:
            gate1p = gate1p + _zero_after(anchor, d)
        x1 = xp_ref[...] + gate1p * mix_scr[...]
        x1_ref[...] = x1
        h2 = _rms(x1, g2_ref[...]) * (1.0 + scale2p) + shift2p
        words = _pack_bf16_pairs(h2)
        for s in range(V7X_SUBLANES):
            h2_scr[slot, :, s] = words[:, s * V7X_LANES:(s + 1) * V7X_LANES].reshape(
                groups, V7X_SUBLANES, V7X_LANES)
        lt = lax.dot_general(wrt_ref[...], h2.astype(BF16), (((1,), (1,)), ((), ())),
                             preferred_element_type=F32) + br_ref[...]
        slab = _route(lt, tm)
        eid_ref[0] = slab[2:4].astype(I32)
        wide = jnp.concatenate([slab, jnp.zeros((V7X_LANES - V7X_SUBLANES, tm), F32)], axis=0)
        gcol_ref[...] = wide.T

    @pl.when(i < last)
    def _():
        x = x_ref[...]
        mod = mod_ref[0]
        shift1, scale1 = mod[:, 0:d], mod[:, d:2 * d]
        h = _rms(x, g1_ref[...]) * (1.0 + scale1) + shift1
        hb = h.astype(BF16)

        uv = jnp.dot(hb, win_s[:, 0:2 * aw], preferred_element_type=F32)
        last_stage(uv[0:1, 0:V7X_LANES])
        uv = jax.nn.gelu(uv)
        u, v = uv[:, :aw], uv[:, aw:]
        tq_r = lax.broadcasted_iota(I32, (CHUNK, CHUNK), 0)
        tq_c = lax.broadcasted_iota(I32, (CHUNK, CHUNK), 1)
        causal = tq_c <= tq_r
        ya = []
        for hh in range(A_HEADS):
            sl = slice(hh * hd, (hh + 1) * hd)
            vh = _rms(v[:, sl], vg_ref[:, sl]).astype(BF16)
            rhs = jnp.concatenate([vh[c * CHUNK:(c + 1) * CHUNK] for c in range(nch)], axis=1)
            w = jnp.where(causal, ws_ref[hh], 0.0).astype(BF16)
            zs = jnp.dot(w, rhs, preferred_element_type=F32) + bst_ref[:, hh:hh + 1]
            zs = jnp.concatenate([zs[:, c * hd:(c + 1) * hd] for c in range(nch)], axis=0)
            ya.append(u[:, sl] * zs)

        bcx = jnp.dot(hb, win_s[:, 2 * aw:], preferred_element_type=F32)
        bg, cg, xin = bcx[:, :bw], bcx[:, bw:2 * bw], bcx[:, 2 * bw:]
        pre = cg * xin

        @pl.when(i % tiles_per_seq == 0)
        def _():
            pre_scr[0:V7X_SUBLANES, :] = jnp.zeros((V7X_SUBLANES, bw), F32)

        pre_scr[V7X_SUBLANES:V7X_SUBLANES + tm, :] = pre
        p1 = pre_scr[V7X_SUBLANES - 1:V7X_SUBLANES - 1 + tm, :]
        p2 = pre_scr[V7X_SUBLANES - 2:V7X_SUBLANES - 2 + tm, :]
        conv = cw_ref[0:1, :] * p2 + cw_ref[1:2, :] * p1 + cw_ref[2:3, :] * pre + cb_ref[...]
        yb = bg * conv
        pre_scr[0:V7X_SUBLANES, :] = pre_scr[tm:tm + V7X_SUBLANES, :]

        y = jnp.concatenate(ya + [yb], axis=1).astype(BF16)
        mix_scr[...] = jnp.dot(y, wout_s[...], preferred_element_type=F32)
        wf16_ref[...] = wf_ref[...].astype(BF16)

    @pl.when(i == last)
    def _():
        last_stage(None)

    @pl.when(i > 0)
    def _():
        for cp in h2_copies(i - 1, slot):
            cp.start()

    @pl.when(i == last)
    def _():
        for sl in range(2):
            for cp in h2_copies(0, sl):
                cp.wait()


def _mix(xt, mod3, g1, g2, win, wout, ws, bst, vg, cw, cb, wrt, br, wf, seq):
    t, d = xt.shape
    tm = TOKEN_TILE
    nt = t // tm
    per_seq = seq // tm
    in_cols = win.shape[1]
    bw = cb.shape[1]
    words_per_row = d // 2
    assert words_per_row == V7X_SUBLANES * V7X_LANES
    assert d % (MIX_STAGE_ROWS * (in_cols // wout.shape[1])) == 0 and d // MIX_STAGE_ROWS >= 2 * MIX_STAGE_SLOTS
    wf_rows = wf.shape[0] // nt
    assert wf.shape[0] % nt == 0 and wf_rows % (2 * V7X_SUBLANES) == 0
    const2 = lambda i: (0, 0)
    body = functools.partial(_mix_body, tiles_per_seq=per_seq)
    cur = lambda i: jnp.minimum(i, nt - 1)
    prev = lambda i: jnp.maximum(i - 1, 0)
    hbm = pl.BlockSpec(memory_space=pl.ANY)
    vmem = (2 * d * (in_cols + d)
            + MIX_STAGE_SLOTS * MIX_STAGE_ROWS * in_cols * 4
            + 3 * 2 * tm * d * 4
            + 2 * tm * d * 2
            + (tm + V7X_SUBLANES) * bw * 4 + tm * d * 4
            + 2 * wf_rows * wf.shape[1] * (4 + 2)
            + tm * (in_cols + 4 * d) * 4)
    return pl.pallas_call(
        body,
        grid=(nt + 1,),
        in_specs=[
            pl.BlockSpec((tm, d), lambda i: (cur(i), 0)),
            pl.BlockSpec((tm, d), lambda i: (prev(i), 0)),
            pl.BlockSpec((1, 1, mod3.shape[2]), lambda i: (cur(i) // per_seq, 0, 0)),
            pl.BlockSpec((1, 1, mod3.shape[2]), lambda i: (prev(i) // per_seq, 0, 0)),
            pl.BlockSpec((1, d), const2),
            pl.BlockSpec((1, d), const2),
            hbm,
            hbm,
            pl.BlockSpec(ws.shape, lambda i: (0, 0, 0)),
            pl.BlockSpec(bst.shape, const2),
            pl.BlockSpec(vg.shape, const2),
            pl.BlockSpec(cw.shape, const2),
            pl.BlockSpec(cb.shape, const2),
            pl.BlockSpec(wrt.shape, const2),
            pl.BlockSpec(br.shape, const2),
            pl.BlockSpec((wf_rows, wf.shape[1]), lambda i: (cur(i), 0)),
        ],
        out_specs=[
            pl.BlockSpec((tm, d), lambda i: (prev(i), 0)),
            pl.BlockSpec((1, TOP_K, tm), lambda i: (prev(i), 0, 0)),
            pl.BlockSpec((tm, V7X_LANES), lambda i: (prev(i), 0)),
            pl.BlockSpec((wf_rows, wf.shape[1]), lambda i: (cur(i), 0)),
            hbm,
        ],
        out_shape=[
            jax.ShapeDtypeStruct((t, d), F32),
            jax.ShapeDtypeStruct((nt, TOP_K, tm), I32),
            jax.ShapeDtypeStruct((t, V7X_LANES), F32),
            jax.ShapeDtypeStruct(wf.shape, BF16),
            jax.ShapeDtypeStruct((t // V7X_SUBLANES, V7X_SUBLANES, words_per_row // V7X_LANES, V7X_LANES), U32),
        ],
        scratch_shapes=[
            pltpu.VMEM((tm + V7X_SUBLANES, bw), F32),
            pltpu.VMEM((tm, d), F32),
            pltpu.VMEM((2, tm // V7X_SUBLANES, words_per_row // V7X_LANES, V7X_SUBLANES, V7X_LANES), U32),
            pltpu.VMEM(win.shape, BF16),
            pltpu.VMEM(wout.shape, BF16),
            pltpu.VMEM((MIX_STAGE_SLOTS, MIX_STAGE_ROWS, in_cols), F32),
            pltpu.SemaphoreType.DMA((2,)),
            pltpu.SemaphoreType.DMA((MIX_STAGE_SLOTS,)),
        ],
        compiler_params=_params(vmem, ("arbitrary",)),
        name="mix",
    )(xt, xt, mod3, mod3, g1, g2, win, wout, ws, bst, vg, cw, cb, wrt, br, wf)


def _dispatch_body(eid_ref, texp_ref, tab_ref, rank_scr, dest_v, dest_s, init_v, sem,
                   *, n_e, n_tiles, n_tok):
    nch, _, c = eid_ref.shape
    te = EXPERT_TILE
    r = lax.broadcasted_iota(I32, (c, c), 0)
    q = lax.broadcasted_iota(I32, (c, c), 1)
    before = (r < q).astype(BF16)
    e_iota = lax.broadcasted_iota(I32, (n_e, c), 0)

    init_v[...] = (lax.broadcasted_iota(I32, init_v.shape, 0) & (DUMP_TILES * te - 1)) + TOP_K * n_tok
    cp_i = pltpu.make_async_copy(init_v, tab_ref, sem.at[0])
    cp_i.start()

    def count_body(ch, carry):
        e2 = eid_ref[ch]
        ranks = []
        for k in range(TOP_K):
            oh = e_iota == e2[k:k + 1]
            ohf = oh.astype(F32)
            pref = jnp.dot(ohf.astype(BF16), before, preferred_element_type=F32)
            ranks.append(jnp.sum(jnp.where(oh, pref + carry, 0.0), axis=0, keepdims=True))
            carry = carry + jnp.sum(ohf, axis=1, keepdims=True)
        rank_scr[ch] = jnp.concatenate(ranks, axis=0)
        return carry

    counts = lax.fori_loop(0, nch, count_body, jnp.zeros((n_e, 1), F32), unroll=COUNT_UNROLL)

    padded = jnp.floor((counts + float(te - 1)) / float(te)) * float(te)
    sub = lax.broadcasted_iota(I32, (n_e, V7X_LANES), 0)
    lane = lax.broadcasted_iota(I32, (n_e, V7X_LANES), 1)
    pstart_row = jnp.sum(jnp.where(sub < lane, padded, 0.0), axis=0, keepdims=True)
    pstart = jnp.sum(jnp.where(sub == lane, pstart_row, 0.0), axis=1, keepdims=True)
    pend = pstart + padded
    total = jnp.max(pend, axis=0, keepdims=True)
    last_e = jnp.max(jnp.where(counts > 0.0, sub[:, 0:1].astype(F32), -1.0), axis=0, keepdims=True)

    rows_k = n_tok // V7X_LANES
    group = V7X_SUBLANES * V7X_LANES // c

    def dest_body(g, carry):
        rows = [[] for _ in range(TOP_K)]
        for j in range(group):
            e2 = eid_ref[g * group + j]
            rk = rank_scr[g * group + j]
            for k in range(TOP_K):
                oh = e_iota == e2[k:k + 1]
                off = jnp.sum(jnp.where(oh, pstart, 0.0), axis=0, keepdims=True)
                dest = (off + rk[k:k + 1]).astype(I32)
                rows[k] += [dest[:, h * V7X_LANES:(h + 1) * V7X_LANES] for h in range(c // V7X_LANES)]
        for k in range(TOP_K):
            word0 = pl.multiple_of((k * rows_k + g * V7X_SUBLANES) * V7X_LANES, V7X_SUBLANES * V7X_LANES)
            dest_v[pl.ds(word0, V7X_SUBLANES * V7X_LANES)] = jnp.concatenate(rows[k], axis=0).reshape(-1)
        return carry

    lax.fori_loop(0, nch // group, dest_body, 0)
    cp_d = pltpu.make_async_copy(dest_v, dest_s, sem.at[1])
    cp_d.start()

    w = texp_ref.shape[1]
    tj = lax.broadcasted_iota(I32, (n_e, w), 1).astype(F32) * float(te)
    texp = jnp.sum((pend <= tj).astype(F32), axis=0, keepdims=True)
    tj1 = tj[0:1]
    texp = jnp.where(tj1 < total, jnp.minimum(texp, float(n_e - 1)), last_e)
    lane_w = lax.broadcasted_iota(I32, (1, w), 1)
    texp = jnp.where(lane_w == n_tiles, total / float(te), texp)
    texp_ref[...] = texp.astype(I32)

    cp_i.wait()
    cp_d.wait()

    def lane_body(lane, carry):
        for row in range(TOP_K * rows_k):
            a = row * V7X_LANES + lane
            tab_ref[dest_s[a]] = a
        return carry

    lax.fori_loop(0, V7X_LANES, lane_body, 0)


def _dispatch(eid, n_e, n_tiles, n_tok):
    nch, k, c = eid.shape
    assert c == EXPERT_TILE and c % V7X_LANES == 0 and nch * c == n_tok and n_tok % (V7X_SUBLANES * V7X_LANES) == 0
    body = functools.partial(_dispatch_body, n_e=n_e, n_tiles=n_tiles, n_tok=n_tok)
    w = 2 * V7X_LANES
    assert n_tiles < w
    rows = n_tiles + V7X_SUBLANES
    rows += (-rows) % 4
    smem = pl.BlockSpec(memory_space=pltpu.SMEM)
    return pl.pallas_call(
        body,
        out_specs=[pl.BlockSpec(memory_space=pltpu.VMEM), smem],
        out_shape=[
            jax.ShapeDtypeStruct((1, w), I32),
            jax.ShapeDtypeStruct((rows * EXPERT_TILE,), I32),
        ],
        scratch_shapes=[
            pltpu.VMEM((nch, k, c), F32),
            pltpu.VMEM((k * n_tok,), I32),
            pltpu.SMEM((k * n_tok,), I32),
            pltpu.VMEM((rows * EXPERT_TILE,), I32),
            pltpu.SemaphoreType.DMA((2,)),
        ],
        compiler_params=_params(nch * V7X_SUBLANES * c * 4 + (k * n_tok + rows * EXPERT_TILE) * 4 + 4 * c * c * 4),
        name="dispatch",
    )(eid)


def _expert_body(tab_ref, texp_ref, h2_hbm, wgu_hbm, wd_hbm, y2_hbm,
                 x0, x1, x2, y0, y1, y2, wgu_f, wd_f, wgu_s, wd_s, gsem, ssem, wsem, zsem, *, n_tiles):
    te = EXPERT_TILE
    ring = EXPERT_RING
    de = wd_s.shape[0]
    n_tok = h2_hbm.shape[0]
    xs, ys = (x0, x1, x2), (y0, y1, y2)
    p = pl.program_id(0)
    n_used = texp_ref[n_tiles]
    t0 = ring * p

    def gather(tile, buf, sem):
        for r in range(te):
            tok = tab_ref[tile * te + r] & (n_tok - 1)
            pltpu.make_async_copy(h2_hbm.at[tok], buf.at[r // V7X_SUBLANES, :, r % V7X_SUBLANES, :], sem).start()

    def scatter(tile, buf, sem):
        for r in range(te):
            row = tab_ref[tile * te + r]
            pltpu.make_async_copy(buf.at[pl.ds(r, 1), :], y2_hbm.at[pl.ds(row, 1), :], sem).start()

    def wait_rows(buf, sem):
        pltpu.make_async_copy(buf, buf, sem).wait()

    def weight_copies(e):
        return (pltpu.make_async_copy(wgu_hbm.at[e], wgu_f, wsem.at[0]),
                pltpu.make_async_copy(wd_hbm.at[e], wd_f, wsem.at[1]))

    def dump_fill(h):
        rows = pl.ds(TOP_K * n_tok + h * te, te)
        return pltpu.make_async_copy(ys[ring - 1], y2_hbm.at[rows, :], zsem)

    def switch_weights(tile):
        e = texp_ref[tile]
        first = jnp.logical_or(tile == 0, texp_ref[jnp.maximum(tile - 1, 0)] != e)

        @pl.when(jnp.logical_and(first, tile < n_used))
        def _():
            for cp in weight_copies(e):
                cp.wait()
            wgu_s[...] = wgu_f[...].astype(BF16)
            wd_s[...] = wd_f[...].astype(BF16)
            nxt = lax.while_loop(
                lambda k: jnp.logical_and(k < n_used, texp_ref[jnp.minimum(k, n_tiles - 1)] == e),
                lambda k: k + 1, tile + 1)

            @pl.when(nxt < n_used)
            def _():
                for cp in weight_copies(texp_ref[jnp.minimum(nxt, n_tiles - 1)]):
                    cp.start(priority=1)

    def compute(xbuf, ybuf):
        words = jnp.concatenate([xbuf[:, s].reshape(te, V7X_LANES) for s in range(V7X_SUBLANES)], axis=1)
        xb16 = _unpack_bf16_pairs(words).astype(BF16)
        gu = jnp.dot(xb16, wgu_s[...], preferred_element_type=F32)
        a = (jax.nn.silu(gu[:, :de]) * gu[:, de:]).astype(BF16)
        ybuf[...] = _pack_bf16_pairs(jnp.dot(a, wd_s[...], preferred_element_type=F32))

    @pl.when(t0 < n_used)
    def _():
        @pl.when(p == 0)
        def _():
            for cp in weight_copies(texp_ref[0]):
                cp.start(priority=1)
            ys[ring - 1][...] = jnp.zeros(ys[ring - 1].shape, U32)
            for h in range(DUMP_TILES):
                dump_fill(h).start()
            for h in range(DUMP_TILES):
                dump_fill(h).wait()
            for m in range(ring - 1):
                gather(m, xs[m], gsem.at[m])

        for m in range(ring):
            tile = t0 + m
            nm = (m + ring - 1) % ring
            switch_weights(tile)
            wait_rows(xs[m], gsem.at[m])
            if m == ring - 1:
                wait_rows(ys[m], ssem.at[m])
            else:
                @pl.when(p > 0)
                def _(m=m):
                    wait_rows(ys[m], ssem.at[m])
            gather(tile + ring - 1, xs[nm], gsem.at[nm])
            scatter(jnp.where(tile == 0, n_tiles, tile - 1), ys[nm], ssem.at[nm])
            compute(xs[m], ys[m])

        @pl.when(t0 + ring >= n_used)
        def _():
            for m in range(ring - 1):
                wait_rows(ys[m], ssem.at[m])
            scatter(t0 + ring - 1, ys[ring - 1], ssem.at[ring - 1])
            wait_rows(ys[ring - 1], ssem.at[ring - 1])
            for m in range(ring - 1):
                wait_rows(xs[m], gsem.at[m])


def _experts(tab, texp, h2p, wgu, wd, n_tiles):
    t = h2p.shape[0]
    dp = h2p.shape[1] * h2p.shape[2]
    te = EXPERT_TILE
    _, d, n_gu = wgu.shape
    de = wd.shape[1]
    assert n_tiles % EXPERT_RING == 0 and t & (t - 1) == 0
    body = functools.partial(_expert_body, n_tiles=n_tiles)
    vmem = (d * n_gu + de * d) * (4 + 2) + 2 * EXPERT_RING * te * dp * 4 + te * (n_gu + 3 * d) * 4
    hbm = pl.BlockSpec(memory_space=pl.ANY)
    row_buf = pltpu.VMEM((te, dp), U32)
    tile_buf = pltpu.VMEM((te // V7X_SUBLANES,) + h2p.shape[1:2] + (V7X_SUBLANES, V7X_LANES), U32)
    return pl.pallas_call(
        body,
        grid_spec=pltpu.PrefetchScalarGridSpec(
            num_scalar_prefetch=2,
            grid=(n_tiles // EXPERT_RING,),
            in_specs=[hbm, hbm, hbm],
            out_specs=hbm,
            scratch_shapes=[tile_buf] * EXPERT_RING + [row_buf] * EXPERT_RING + [
                pltpu.VMEM((d, n_gu), F32), pltpu.VMEM((de, d), F32),
                pltpu.VMEM((d, n_gu), BF16), pltpu.VMEM((de, d), BF16),
                pltpu.SemaphoreType.DMA((EXPERT_RING,)), pltpu.SemaphoreType.DMA((EXPERT_RING,)),
                pltpu.SemaphoreType.DMA((2,)), pltpu.SemaphoreType.DMA,
            ],
        ),
        out_shape=jax.ShapeDtypeStruct((TOP_K * t + DUMP_TILES * te, dp), U32),
        compiler_params=_params(vmem, ("arbitrary",)),
        name="experts",
    )(tab, texp, h2p, wgu, wd)


def _combine_body(x1_ref, y0_ref, y1_ref, gcol_ref, mod_ref, modf_ref, gf_ref, o_ref):
    d = x1_ref.shape[1]
    gate2 = mod_ref[0][:, 5 * d:6 * d]
    modf = modf_ref[0]
    shift_f, scale_f = modf[:, 0:d], modf[:, d:2 * d]
    y0 = _unpack_bf16_pairs(y0_ref[...])
    y1 = _unpack_bf16_pairs(y1_ref[...])
    ffn = y0 * gcol_ref[:, 0:1] + y1 * gcol_ref[:, 1:2]
    x2 = x1_ref[...] + gate2 * ffn
    o_ref[...] = _rms(x2, gf_ref[...]) * (1.0 + scale_f) + shift_f


def _combine(x1, y2, gcol, mod3, modf3, gf, seq):
    t, d = x1.shape
    tm = COMBINE_TILE
    per_seq = seq // tm
    return pl.pallas_call(
        _combine_body,
        grid=(t // tm,),
        in_specs=[
            pl.BlockSpec((tm, d), lambda i: (i, 0)),
            pl.BlockSpec((tm, d // 2), lambda i: (i, 0)),
            pl.BlockSpec((tm, d // 2), lambda i: (t // tm + i, 0)),
            pl.BlockSpec((tm, V7X_LANES), lambda i: (i, 0)),
            pl.BlockSpec((1, 1, mod3.shape[2]), lambda i: (i // per_seq, 0, 0)),
            pl.BlockSpec((1, 1, modf3.shape[2]), lambda i: (i // per_seq, 0, 0)),
            pl.BlockSpec((1, d), lambda i: (0, 0)),
        ],
        out_specs=pl.BlockSpec((tm, d), lambda i: (i, 0)),
        out_shape=jax.ShapeDtypeStruct((t, d), F32),
        compiler_params=_params(2 * tm * (3 * d + V7X_LANES) * 4, ("arbitrary",)),
        name="combine",
    )(x1, y2, y2, gcol, mod3, modf3, gf)


def kernel(x, c, w_ada, b_ada, norm1_g, w_in, w_out, gmlp_w_s, gmlp_b_s, gmlp_v_gain, conv_w, conv_b,
           norm2_g, w_router_group, b_router_group, w_router_expert, b_router_expert, w_gate_up, w_down,
           w_ada_final, b_ada_final, norm_f_g):
    bsz, seq, d = x.shape
    depth = w_ada.shape[0]
    n_e = w_router_expert.shape[2]
    t = bsz * seq
    assert seq % TOKEN_TILE == 0 and TOKEN_TILE % CHUNK == 0 and seq % COMBINE_TILE == 0
    assert w_router_group.shape[2] == N_GROUPS <= V7X_SUBLANES and n_e == N_GROUPS * V7X_SUBLANES
    assert ROUTER_ROWS == n_e + V7X_SUBLANES
    n_tiles = (t * TOP_K + n_e * (EXPERT_TILE - 1)) // EXPERT_TILE
    n_tiles = -(-n_tiles // EXPERT_RING) * EXPERT_RING

    xt = x.reshape(t, d)
    for l in range(depth):
        mod3 = _ada(c, w_ada[l], b_ada[l]).reshape(bsz, 1, -1)
        wr = jnp.concatenate([w_router_expert[l], w_router_group[l]], axis=1)
        wrt = jnp.pad(wr, ((0, 0), (0, ROUTER_ROWS - wr.shape[1]))).T.astype(BF16)
        br = jnp.concatenate([b_router_expert[l], b_router_group[l]])
        br = jnp.pad(br, (0, ROUTER_ROWS - br.shape[0])).reshape(ROUTER_ROWS, 1)
        x1, eid, gcol, wf16, h2 = _mix(
            xt, mod3, norm1_g[l].reshape(1, d), norm2_g[l].reshape(1, d),
            w_in[l], w_out[l], gmlp_w_s[l], gmlp_b_s[l].T,
            gmlp_v_gain[l].reshape(1, -1), conv_w[l], conv_b[l].reshape(1, -1), wrt, br, w_ada_final, seq)
        modf3 = _ada(c, wf16, b_ada_final).reshape(bsz, 1, 2 * d)
        texp, tab = _dispatch(eid, n_e, n_tiles, t)
        h2 = h2.reshape((t,) + h2.shape[2:])
        y2 = _experts(tab, texp.reshape(-1), h2, w_gate_up[l], w_down[l], n_tiles)
        assert depth == 1
        xt = _combine(x1, y2, gcol, mod3, modf3, norm_f_g.reshape(1, d), seq)
    return xt.reshape(bsz, seq, d)
```
